```python
import math
import jax
import jax.numpy as jnp
from jax import lax
import numpy as np

D_MODEL = 1024
BATCH = 4
SEQ = 4096
DEPTH = 2

GRID_W = 64
CTX_LEN = 256
MIX_W = D_MODEL
ML_H = 4
ML_W = MIX_W // 4
ML_D = ML_W // ML_H
ML_CHUNK = 64
HG_H = 4
HG_W = MIX_W // 4
HG_K = HG_W // HG_H
HG_V = HG_W // HG_H
HG_CHUNK = 16
DA_H = 4
DA_W = MIX_W // 2
DA_V = DA_W // DA_H
DA_D = DA_V // 2
Q_BLOCK = 128
ROPE_BASE = 10000.0
FF_DENSE = ((8 * D_MODEL // 3 + 255) // 256) * 256
N_EXPERTS = 8
TOP_K = 2
FF_EXPERT = 7 * D_MODEL // 2
N_DENSE = (DEPTH + 1) // 2
N_MOE = DEPTH // 2
EPS = 1e-6
MASK_NEG = -1e30
LB_FLOOR = 1e-30
f32 = jnp.float32

PROJ_LAYOUT = (
    ('ml_q', ML_W), ('ml_k', ML_W), ('ml_v', ML_W), ('ml_o', ML_W), ('ml_gates', 4 * ML_H),
    ('hg_q', HG_W), ('hg_f', 2 * HG_W), ('hg_i', HG_W), ('hg_g', HG_W),
    ('da_q', DA_W), ('da_k', DA_W), ('da_v', DA_W),
)
PROJ_W = sum(w for _, w in PROJ_LAYOUT)

kernel_name = 'hybrid_mlstm_hgrn2_diffattn_prefix_dit'


def rms_norm(x, w):
    x32 = x.astype(f32)
    y = x32 * lax.rsqrt(jnp.mean(x32 * x32, axis=-1, keepdims=True) + EPS)
    return (y * w.astype(f32)).astype(x.dtype)


def head_rms_norm(x, w, n_heads):
    b, l, _ = x.shape
    return rms_norm(x.reshape(b, l, n_heads, -1), w).reshape(b, l, -1)


def modulate(h, shift, scale):
    return h * (1 + scale) + shift


def split_proj(p):
    out = {}
    start = 0
    for name, width in PROJ_LAYOUT:
        out[name] = p[..., start:start + width]
        start += width
    return out


def to_heads(t, n_heads):
    b, l, w = t.shape
    return t.astype(f32).reshape(b, l, n_heads, w // n_heads).transpose(0, 2, 1, 3)


def from_heads(t):
    b, h, l, d = t.shape
    return t.transpose(0, 2, 1, 3).reshape(b, l, h * d)


def chunk_time(a, lc):
    b, h, l = a.shape[:3]
    return a.reshape(b, h, l // lc, lc, *a.shape[3:])


def unchunk_time(a):
    b, h, n, lc = a.shape[:4]
    return a.reshape(b, h, n * lc, *a.shape[4:])


def scan_chunks(step, init, xs):
    xs = jax.tree_util.tree_map(lambda a: jnp.moveaxis(a, 2, 0), xs)
    final, starts = lax.scan(step, init, xs)
    starts = jax.tree_util.tree_map(lambda a: jnp.moveaxis(a, 0, 2), starts)
    return starts, final


def run_direction(states_fn, outputs_fn, lc, ctx_in, lat_in, init, reverse, need_ctx):
    def prep(arrs):
        return tuple(chunk_time(jnp.flip(a, axis=2) if reverse else a, lc) for a in arrs)

    def post(o):
        o = unchunk_time(o)
        return jnp.flip(o, axis=2) if reverse else o

    c_in = prep(ctx_in)
    c_starts, c_final = states_fn(c_in, init)
    out_c = post(outputs_fn(c_in, c_starts)) if need_ctx else None
    x_in = prep(lat_in)
    x_starts, _ = states_fn(x_in, c_final)
    out_x = post(outputs_fn(x_in, x_starts))
    return out_x, out_c


def mlstm_states(inputs, init):
    q, k, v, li, lf = inputs
    b = jnp.cumsum(lf, axis=-1)
    b_last = b[..., -1]
    a = b_last[..., None] - b + li
    m_loc = jnp.max(a, axis=-1)
    w = jnp.exp(a - m_loc[..., None])
    dC = jnp.einsum('bhncd,bhnce->bhnde', k * w[..., None], v)
    dn = jnp.einsum('bhncd,bhnc->bhnd', k, w)

    def step(carry, inp):
        C, n, m = carry
        dC_j, dn_j, bl_j, ml_j = inp
        m_new = jnp.maximum(bl_j + m, ml_j)
        s_old = jnp.exp(bl_j + m - m_new)
        s_new = jnp.exp(ml_j - m_new)
        C_new = s_old[..., None, None] * C + s_new[..., None, None] * dC_j
        n_new = s_old[..., None] * n + s_new[..., None] * dn_j
        return (C_new, n_new, m_new), (C, n, m)

    return scan_chunks(step, init, (dC, dn, b_last, m_loc))


def mlstm_outputs(inputs, starts):
    q, k, v, li, lf = inputs
    C0, n0, m0 = starts
    lc = q.shape[3]
    b = jnp.cumsum(lf, axis=-1)
    causal = jnp.tril(jnp.ones((lc, lc), dtype=bool))
    d = jnp.where(causal, b[..., :, None] - b[..., None, :] + li[..., None, :], MASK_NEG)
    inter = b + m0[..., None]
    m = jnp.maximum(inter, jnp.max(d, axis=-1))
    w_intra = jnp.exp(d - m[..., None])
    w_inter = jnp.exp(inter - m)
    s = jnp.einsum('bhntd,bhnsd->bhnts', q, k) * w_intra
    num = jnp.einsum('bhnts,bhnse->bhnte', s, v) + w_inter[..., None] * jnp.einsum('bhntd,bhnde->bhnte', q, C0)
    den = jnp.sum(s, axis=-1) + w_inter * jnp.einsum('bhntd,bhnd->bhnt', q, n0)
    return num / jnp.maximum(jnp.abs(den), jnp.exp(-m))[..., None]


def mlstm_mixer(pc, px, gate_b, norm_w, need_ctx):
    def prep(p):
        b, l, _ = p['ml_q'].shape
        q = to_heads(p['ml_q'], ML_H)
        k = to_heads(p['ml_k'], ML_H) * (ML_D ** -0.5)
        v = to_heads(p['ml_v'], ML_H)
        g = (p['ml_gates'].astype(f32) + gate_b.astype(f32)).reshape(b, l, 4, ML_H).transpose(2, 0, 3, 1)
        fwd = (q, k, v, g[0], jax.nn.log_sigmoid(g[1]))
        bwd = (q, k, v, g[2], jax.nn.log_sigmoid(g[3]))
        return fwd, bwd

    c_f, c_b = prep(pc)
    x_f, x_b = prep(px)
    bsz = px['ml_q'].shape[0]
    init = (jnp.zeros((bsz, ML_H, ML_D, ML_D), f32), jnp.zeros((bsz, ML_H, ML_D), f32), jnp.zeros((bsz, ML_H), f32))
    xf, cf = run_direction(mlstm_states, mlstm_outputs, ML_CHUNK, c_f, x_f, init, False, need_ctx)
    xb, cb = run_direction(mlstm_states, mlstm_outputs, ML_CHUNK, c_b, x_b, init, True, need_ctx)
    dtype = px['ml_q'].dtype

    def finish(h, p):
        return (jax.nn.sigmoid(p['ml_o'].astype(f32)) * head_rms_norm(from_heads(h), norm_w.reshape(ML_H, ML_D), ML_H)).astype(dtype)

    return finish(xf + xb, px), (finish(cf + cb, pc) if need_ctx else None)


def gla_states(inputs, init):
    q, k, v, g = inputs
    G = jnp.cumsum(g, axis=3)
    G_last = G[:, :, :, -1]
    dS = jnp.einsum('bhnck,bhnce->bhnke', k * jnp.exp(G_last[:, :, :, None] - G), v)

    def step(S, inp):
        dS_j, gl_j = inp
        return jnp.exp(gl_j)[..., None] * S + dS_j, S

    return scan_chunks(step, init, (dS, G_last))


def gla_outputs(inputs, S0):
    q, k, v, g = inputs
    lc = q.shape[3]
    G = jnp.cumsum(g, axis=3)
    causal = jnp.tril(jnp.ones((lc, lc), dtype=bool))
    decay = jnp.exp(jnp.where(causal[:, :, None], G[:, :, :, :, None, :] - G[:, :, :, None, :, :], MASK_NEG))
    a = jnp.einsum('bhntk,bhntsk,bhnsk->bhnts', q, decay, k)
    return jnp.einsum('bhnts,bhnse->bhnte', a, v) + jnp.einsum('bhntk,bhnke->bhnte', q * jnp.exp(G), S0)


def hgrn2_mixer(pc, px, lb, norm_w, need_ctx):
    lb = lb.astype(f32)
    log_lb = jnp.log(jnp.maximum(lb, LB_FLOOR))
    log_1m_lb = jnp.log1p(-lb)

    def prep(p):
        q = to_heads(jax.nn.silu(p['hg_q'].astype(f32)), HG_H)
        v = to_heads(p['hg_i'], HG_H)
        fr = p['hg_f'].astype(f32)
        dirs = []
        for fr_d in (fr[..., :HG_W], fr[..., HG_W:]):
            log_f = jnp.logaddexp(log_lb, log_1m_lb + jax.nn.log_sigmoid(fr_d))
            k = (1.0 - lb) * jax.nn.sigmoid(-fr_d)
            dirs.append((q, to_heads(k, HG_H), v, to_heads(log_f, HG_H)))
        return dirs

    c_f, c_b = prep(pc)
    x_f, x_b = prep(px)
    bsz = px['hg_q'].shape[0]
    init = jnp.zeros((bsz, HG_H, HG_K, HG_V), f32)
    xf, cf = run_direction(gla_states, gla_outputs, HG_CHUNK, c_f, x_f, init, False, need_ctx)
    xb, cb = run_direction(gla_states, gla_outputs, HG_CHUNK, c_b, x_b, init, True, need_ctx)
    dtype = px['hg_q'].dtype

    def finish(o, p):
        return (head_rms_norm(from_heads(o), norm_w.reshape(HG_H, HG_V), HG_H) * jax.nn.silu(p['hg_g'].astype(f32))).astype(dtype)

    return finish(xf + xb, px), (finish(cf + cb, pc) if need_ctx else None)


def axial_rope_tables(n_tokens):
    rows = n_tokens // GRID_W
    row = jnp.repeat(jnp.arange(rows), GRID_W).astype(f32)
    col = jnp.tile(jnp.arange(GRID_W), rows).astype(f32)
    n_freq = DA_D // 4
    inv = ROPE_BASE ** (-jnp.arange(n_freq, dtype=f32) / n_freq)
    ang = jnp.stack([row[:, None] * inv, col[:, None] * inv], axis=1)
    return jnp.cos(ang), jnp.sin(ang)


def apply_axial_rope(t, cos, sin):
    shape = t.shape
    t = t.reshape(*shape[:-1], 2, 2, DA_D // 4)
    t1, t2 = t[..., 0, :], t[..., 1, :]
    out = jnp.stack([t1 * cos - t2 * sin, t1 * sin + t2 * cos], axis=-2)
    return out.reshape(shape)


def diff_softmax_attend(q, k, v, lam):
    s = jnp.einsum('ibhqd,ibhkd->ibhqk', q, k).astype(f32) * (DA_D ** -0.5)
    p = jax.nn.softmax(s, axis=-1)
    return jnp.einsum('bhqk,bhkv->bhqv', p[0] - lam * p[1], v.astype(f32))


def diff_attn_blocks(q, k, v, lam):
    _, b, h, l, d = q.shape
    nb = l // Q_BLOCK
    qb = jnp.moveaxis(q.reshape(2, b, h, nb, Q_BLOCK, d), 3, 0)
    out = lax.map(lambda blk: diff_softmax_attend(blk, k, v, lam), qb)
    return jnp.moveaxis(out, 0, 2).reshape(b, h, l, -1)


def diff_attn_mixer(pc, px, lam_p, norm_w, lam_init, need_ctx):
    def qkv(p):
        b, l, _ = p['da_q'].shape
        q = p['da_q'].astype(f32).reshape(b, l, DA_H, 2, DA_D).transpose(3, 0, 2, 1, 4)
        k = p['da_k'].astype(f32).reshape(b, l, DA_H, 2, DA_D).transpose(3, 0, 2, 1, 4)
        v = to_heads(p['da_v'], DA_H)
        return q, k, v

    qc, kc, vc = qkv(pc)
    qx, kx, vx = qkv(px)
    cos, sin = axial_rope_tables(qx.shape[3])
    qx = apply_axial_rope(qx, cos, sin)
    kx = apply_axial_rope(kx, cos, sin)
    lp = lam_p.astype(f32)
    lam = jnp.exp(jnp.sum(lp[0] * lp[1])) - jnp.exp(jnp.sum(lp[2] * lp[3])) + lam_init
    k_all = jnp.concatenate([kx, kc], axis=3)
    v_all = jnp.concatenate([vx, vc], axis=2)
    dtype = px['da_q'].dtype

    def finish(o):
        return (head_rms_norm(from_heads(o), norm_w, DA_H) * (1.0 - lam_init)).astype(dtype)

    out_x = finish(diff_attn_blocks(qx, k_all, v_all, lam))
    out_c = finish(diff_softmax_attend(qc, kc, vc, lam)) if need_ctx else None
    return out_x, out_c


def swiglu(h, w_gate, w_up, w_down):
    return (jax.nn.silu(h @ w_gate) * (h @ w_up)) @ w_down


def moe_swiglu(h, router_w, router_b, w_gate, w_up, w_down):
    shape = h.shape
    t = h.reshape(-1, shape[-1])
    logits = (t @ router_w + router_b).astype(f32)
    top_v, top_i = lax.top_k(logits, TOP_K)
    top_g = jax.nn.softmax(top_v, axis=-1)
    gates = jnp.sum(jax.nn.one_hot(top_i, N_EXPERTS, dtype=f32) * top_g[..., None], axis=1)
    y = jnp.zeros(t.shape, f32)
    for e in range(N_EXPERTS):
        y = y + gates[:, e:e + 1] * swiglu(t, w_gate[e], w_up[e], w_down[e]).astype(f32)
    return y.reshape(shape).astype(h.dtype)


def setup_inputs(seed: int = 0) -> dict:
    key = jax.random.key(seed)
    ks = jax.random.split(key, 32)
    nrm = jax.random.normal
    D = D_MODEL
    fb = jnp.linspace(3.0, 6.0, ML_H)
    gate_base = jnp.stack([jnp.zeros(ML_H), fb, jnp.zeros(ML_H), fb])
    return {
        'x': nrm(ks[0], (BATCH, SEQ, D), f32),
        'c': nrm(ks[1], (BATCH, D), f32),
        'ctx': nrm(ks[2], (BATCH, CTX_LEN, D), f32),
        'c_ctx': nrm(ks[3], (D,), f32),
        'w_mod': nrm(ks[4], (DEPTH, D, 6 * D), f32) * (0.5 * D ** -0.5),
        'b_mod': nrm(ks[5], (DEPTH, 6 * D), f32) * 0.02,
        'norm_pre_mix': 1.0 + 0.05 * nrm(ks[6], (DEPTH, D), f32),
        'norm_post_mix': 1.0 + 0.05 * nrm(ks[7], (DEPTH, D), f32),
        'norm_pre_ffn': 1.0 + 0.05 * nrm(ks[8], (DEPTH, D), f32),
        'norm_post_ffn': 1.0 + 0.05 * nrm(ks[9], (DEPTH, D), f32),
        'w_in': nrm(ks[10], (DEPTH, D, PROJ_W), f32) * D ** -0.5,
        'w_out': nrm(ks[11], (DEPTH, MIX_W, D), f32) * MIX_W ** -0.5,
        'ml_gate_b': (gate_base[None] + 0.1 * nrm(ks[12], (DEPTH, 4, ML_H), f32)).reshape(DEPTH, 4 * ML_H),
        'ml_norm': 1.0 + 0.05 * nrm(ks[13], (DEPTH, ML_W), f32),
        'hg_lb_logits': nrm(ks[14], (DEPTH, HG_W), f32),
        'hg_norm': 1.0 + 0.05 * nrm(ks[15], (DEPTH, HG_W), f32),
        'da_lambda': 0.1 * nrm(ks[16], (DEPTH, 4, DA_D), f32),
        'da_norm': 1.0 + 0.05 * nrm(ks[17], (DEPTH, DA_V), f32),
        'ffn_w_gate': nrm(ks[18], (N_DENSE, D, FF_DENSE), f32) * D ** -0.5,
        'ffn_w_up': nrm(ks[19], (N_DENSE, D, FF_DENSE), f32) * D ** -0.5,
        'ffn_w_down': nrm(ks[20], (N_DENSE, FF_DENSE, D), f32) * FF_DENSE ** -0.5,
        'router_w': nrm(ks[21], (N_MOE, D, N_EXPERTS), f32) * D ** -0.5,
        'router_b': 0.01 * nrm(ks[22], (N_MOE, N_EXPERTS), f32),
        'moe_w_gate': nrm(ks[23], (N_MOE, N_EXPERTS, D, FF_EXPERT), f32) * D ** -0.5,
        'moe_w_up': nrm(ks[24], (N_MOE, N_EXPERTS, D, FF_EXPERT), f32) * D ** -0.5,
        'moe_w_down': nrm(ks[25], (N_MOE, N_EXPERTS, FF_EXPERT, D), f32) * FF_EXPERT ** -0.5,
    }


def reference(x, c, ctx, c_ctx, w_mod, b_mod, norm_pre_mix, norm_post_mix, norm_pre_ffn, norm_post_ffn,
              w_in, w_out, ml_gate_b, ml_norm, hg_lb_logits, hg_norm, da_lambda, da_norm,
              ffn_w_gate, ffn_w_up, ffn_w_down, router_w, router_b, moe_w_gate, moe_w_up, moe_w_down):
    lb_soft = jax.nn.softmax(hg_lb_logits.astype(f32), axis=0)
    lower_bounds = jnp.cumsum(lb_soft, axis=0) - lb_soft[0]
    for layer in range(DEPTH):
        need_ctx = layer < DEPTH - 1
        mod_x = jnp.split((jax.nn.silu(c) @ w_mod[layer] + b_mod[layer])[:, None, :], 6, axis=-1)
        mod_c = jnp.split(jax.nn.silu(c_ctx) @ w_mod[layer] + b_mod[layer], 6, axis=-1)
        lam_init = 0.8 - 0.6 * math.exp(-0.3 * layer)

        hx = modulate(rms_norm(x, norm_pre_mix[layer]), mod_x[0], mod_x[1])
        hc = modulate(rms_norm(ctx, norm_pre_mix[layer]), mod_c[0], mod_c[1])
        px = split_proj(hx @ w_in[layer])
        pc = split_proj(hc @ w_in[layer])
        ml_x, ml_c = mlstm_mixer(pc, px, ml_gate_b[layer], ml_norm[layer], need_ctx)
        hg_x, hg_c = hgrn2_mixer(pc, px, lower_bounds[layer], hg_norm[layer], need_ctx)
        da_x, da_c = diff_attn_mixer(pc, px, da_lambda[layer], da_norm[layer], lam_init, need_ctx)
        mix_x = jnp.concatenate([ml_x, hg_x, da_x], axis=-1) @ w_out[layer]
        x = x + mod_x[2] * rms_norm(mix_x, norm_post_mix[layer])
        if need_ctx:
            mix_c = jnp.concatenate([ml_c, hg_c, da_c], axis=-1) @ w_out[layer]
            ctx = ctx + mod_c[2] * rms_norm(mix_c, norm_post_mix[layer])

        def channel_mixer(h, layer=layer):
            i = layer // 2
            if layer % 2 == 0:
                return swiglu(h, ffn_w_gate[i], ffn_w_up[i], ffn_w_down[i])
            return moe_swiglu(h, router_w[i], router_b[i], moe_w_gate[i], moe_w_up[i], moe_w_down[i])

        hx = modulate(rms_norm(x, norm_pre_ffn[layer]), mod_x[3], mod_x[4])
        x = x + mod_x[5] * rms_norm(channel_mixer(hx), norm_post_ffn[layer])
        if need_ctx:
            hc = modulate(rms_norm(ctx, norm_pre_ffn[layer]), mod_c[3], mod_c[4])
            ctx = ctx + mod_c[5] * rms_norm(channel_mixer(hc), norm_post_ffn[layer])
    return x
```

```python
import functools
import math

import jax
import jax.numpy as jnp
from jax import lax
from jax.experimental import pallas as pl
from jax.experimental.pallas import tpu as pltpu

f32 = jnp.float32
bf16 = jnp.bfloat16

ML_H = 4
HG_H = 4
DA_H = 4
GRID_W = 64
ROPE_BASE = 10000.0
N_EXPERTS = 8
EPS = 1e-6
MASK_NEG = -1e30
LB_FLOOR = 1e-30

LANES = 128
ML_CHUNK = 128
HG_CHUNK = 16
HG_GROUP = 128
VMEM_LIMIT = 56 * 1024 * 1024


def _cparams(sem):
    return pltpu.CompilerParams(dimension_semantics=sem, vmem_limit_bytes=VMEM_LIMIT)


def _sigmoid(x):
    return 1.0 / (1.0 + jnp.exp(-x))


def _log_sigmoid(x):
    return jnp.minimum(x, 0.0) - jnp.log(1.0 + jnp.exp(-jnp.abs(x)))


def _split_hi_lo(a):
    hi = a.astype(bf16)
    lo = (a - hi.astype(f32)).astype(bf16)
    return hi, lo


def _dot(a, b):
    return jnp.dot(a, b, preferred_element_type=f32)


def _dot_nt(a, b):
    return lax.dot_general(a, b, (((1,), (1,)), ((), ())), preferred_element_type=f32)


def _dot_tn(a, b):
    return lax.dot_general(a, b, (((0,), (0,)), ((), ())), preferred_element_type=f32)


def _dot2(a_f32, b_bf16):
    hi, lo = _split_hi_lo(a_f32)
    return _dot(hi, b_bf16) + _dot(lo, b_bf16)


def _mod_kernel(s_ref, w_ref, b_ref, o_ref):
    s = s_ref[...]
    s = s * _sigmoid(s)
    o_ref[0] = _dot(s.astype(bf16), w_ref[0].astype(bf16)) + b_ref[0]


def _mod_vectors(c, c_ctx, w_mod, b_mod):
    depth, d, n = w_mod.shape
    bsz = c.shape[0]
    rows = 8
    s = jnp.zeros((rows, d), f32).at[:bsz].set(c).at[bsz].set(c_ctx)
    tn = n // 4
    out = pl.pallas_call(
        _mod_kernel,
        grid=(depth, n // tn),
        in_specs=[
            pl.BlockSpec((rows, d), lambda l, j: (0, 0)),
            pl.BlockSpec((1, d, tn), lambda l, j: (l, 0, j)),
            pl.BlockSpec((1, 1, tn), lambda l, j: (l, 0, j)),
        ],
        out_specs=pl.BlockSpec((1, rows, tn), lambda l, j: (l, 0, j)),
        out_shape=jax.ShapeDtypeStruct((depth, rows, n), f32),
        compiler_params=_cparams(("parallel", "parallel")),
        name="mod_vectors",
    )(s, w_mod, b_mod.reshape(depth, 1, n))
    return out


P_ML = (0, 1024)
P_MLG = (1024, 1280)
P_HG = (1280, 2048)
P_HGF = (2048, 2560)
P_DA = (2560, 4096)
P_TOTAL = 4096


def _prep_w_in(w):
    d = w.shape[0]
    ml = w[:, 0:1024]
    g = w[:, 1024:1040]
    z = jnp.zeros((d, LANES - 8), w.dtype)
    gates = jnp.concatenate([g[:, 0:8], z, g[:, 8:16], z], axis=1)
    hg_q = w[:, 1040:1296]
    hg_f = w[:, 1296:1808]
    hg_i = w[:, 1808:2064]
    hg_g = w[:, 2064:2320]
    da = w[:, 2320:3856]
    return jnp.concatenate([ml, gates, hg_q, hg_i, hg_g, hg_f, da], axis=1).astype(bf16)


def _inproj_kernel(*refs, rope):
    if rope:
        x_ref, mod_ref, nw_ref, w_ref, cos_ref, sin_ref = refs[:6]
        outs = refs[6:]
    else:
        x_ref, mod_ref, nw_ref, w_ref = refs[:4]
        outs = refs[4:]
    ml_ref, mlg_ref, hg_ref, hgf_ref, da_ref = outs
    d = x_ref.shape[-1]
    x = x_ref[0]
    ms = jnp.mean(x * x, axis=-1, keepdims=True)
    h = x * lax.rsqrt(ms + EPS) * nw_ref[...]
    mod = mod_ref[0]
    h = h * (1.0 + mod[:, d:2 * d]) + mod[:, 0:d]
    hb = h.astype(bf16)

    def proj(lo, hi):
        return _dot(hb, w_ref[:, lo:hi])

    ml_ref[0] = proj(*P_ML).astype(bf16)
    mlg_ref[0] = proj(*P_MLG)
    pq = proj(P_HG[0], P_HG[0] + 256)
    hg_ref[0, :, 0:256] = (pq * _sigmoid(pq)).astype(bf16)
    hg_ref[0, :, 256:768] = proj(P_HG[0] + 256, P_HG[1]).astype(bf16)
    hgf_ref[0] = proj(*P_HGF)
    q = proj(P_DA[0], P_DA[0] + 512)
    k = proj(P_DA[0] + 512, P_DA[0] + 1024)
    if rope:
        cos = jnp.concatenate([cos_ref[...]] * 4, axis=1)
        sin = jnp.concatenate([sin_ref[...]] * 4, axis=1)
        lane = lax.broadcasted_iota(jnp.int32, (1, 512), 1)
        first = (lane % 32) < 16

        def rot(t):
            sw = jnp.where(first, pltpu.roll(t, 512 - 16, 1), pltpu.roll(t, 16, 1))
            return t * cos + sw * sin

        q = rot(q)
        k = rot(k)
    da_ref[0, :, 0:512] = (q * 0.125).astype(bf16)
    da_ref[0, :, 512:1024] = k.astype(bf16)
    da_ref[0, :, 1024:1536] = proj(P_DA[0] + 1024, P_DA[1]).astype(bf16)


def _inproj(x, mod, mod_row0, per_batch, nw, w, tables, tm):
    bsz, l, d = x.shape
    rope = tables is not None
    nt = l // tm
    mod_map = (lambda b, i: (mod_row0 + b, 0, 0)) if per_batch else (lambda b, i: (mod_row0, 0, 0))
    in_specs = [
        pl.BlockSpec((1, tm, d), lambda b, i: (b, i, 0)),
        pl.BlockSpec((1, 1, mod.shape[-1]), mod_map),
        pl.BlockSpec((1, d), lambda b, i: (0, 0)),
        pl.BlockSpec((d, P_TOTAL), lambda b, i: (0, 0)),
    ]
    args = [x, mod, nw, w]
    if rope:
        in_specs += [pl.BlockSpec((tm, LANES), lambda b, i: (i, 0))] * 2
        args += list(tables)
    widths = (1024, 256, 768, 512, 1536)
    dtypes = (bf16, f32, bf16, f32, bf16)
    out_specs = [pl.BlockSpec((1, tm, wd), lambda b, i: (b, i, 0)) for wd in widths]
    out_shape = [jax.ShapeDtypeStruct((bsz, l, wd), dt) for wd, dt in zip(widths, dtypes)]
    return pl.pallas_call(
        functools.partial(_inproj_kernel, rope=rope),
        grid=(bsz, nt),
        in_specs=in_specs,
        out_specs=out_specs,
        out_shape=out_shape,
        compiler_params=_cparams(("parallel", "parallel")),
        name="inproj_x" if rope else "inproj_c",
    )(*args)


def _rope_tables(l):
    rows = l // GRID_W
    row = jnp.repeat(jnp.arange(rows), GRID_W).astype(f32)
    col = jnp.tile(jnp.arange(GRID_W), rows).astype(f32)
    n_freq = 16
    inv = ROPE_BASE ** (-jnp.arange(n_freq, dtype=f32) / n_freq)
    lane = jnp.arange(LANES)
    freq = inv[lane % n_freq]
    pos = jnp.where(((lane % 64) // 32 == 0)[None, :], row[:, None], col[:, None])
    ang = pos * freq[None, :]
    sign = jnp.where((lane % 32) < 16, -1.0, 1.0)[None, :]
    return jnp.cos(ang), jnp.sin(ang) * sign


ML_D = 64
ML_STATE_ROWS = ML_D + 8


def _mlstm_kernel(q_ref, k_ref, v_ref, g_ref, gb_ref, tri_ref, s0_ref, o_ref, sf_ref, st_ref):
    d = pl.program_id(0)
    j = pl.program_id(2)
    nj = pl.num_programs(2)
    lc = q_ref.shape[1]

    @pl.when(j == 0)
    def _():
        st_ref[...] = s0_ref[0, 0]

    g = g_ref[0] + gb_ref[0]
    lf = _log_sigmoid(g)
    hi, lo = _split_hi_lo(lf)
    bc = _dot(tri_ref[0], hi) + _dot(tri_ref[0], lo)
    bt = bc.T
    gt = g.T
    fwd = d == 0
    blast_row = jnp.where(fwd, bc[lc - 1:lc, :], bc[0:1, :])

    r = lax.broadcasted_iota(jnp.int32, (lc, lc), 0)
    c = lax.broadcasted_iota(jnp.int32, (lc, lc), 1)
    sgn = 1 - 2 * d
    mask = (r - c) * sgn >= 0
    lane = lax.broadcasted_iota(jnp.int32, (lc, ML_D), 1)
    ones_col = jnp.where(lane == 0, 1.0, 0.0).astype(bf16)

    for h in range(ML_H):
        b_col = bc[:, 4 + h:5 + h]
        b_row = bt[4 + h:5 + h, :]
        li_col = g[:, h:h + 1]
        li_row = gt[h:h + 1, :]
        blast = blast_row[:, 4 + h:5 + h]
        c_aug = st_ref[h, 0:ML_D, :]
        m0 = st_ref[h, ML_D:ML_D + 1, 0:1]

        dm = jnp.where(mask, b_col - b_row + li_row, MASK_NEG)
        inter = b_col + m0
        m = jnp.maximum(inter, jnp.max(dm, axis=-1, keepdims=True))
        w_intra = jnp.exp(dm - m)
        w_inter = jnp.exp(inter - m)

        qh = q_ref[0, :, h * ML_D:(h + 1) * ML_D]
        kh = k_ref[0, :, h * ML_D:(h + 1) * ML_D] * jnp.asarray(0.125, bf16)
        vh = v_ref[0, :, h * ML_D:(h + 1) * ML_D]
        v_aug = jnp.concatenate([vh, ones_col], axis=1)
        s = _dot_nt(qh, kh) * w_intra
        nd = _dot(s.astype(bf16), v_aug) + w_inter * _dot(qh, c_aug.astype(bf16))
        num = nd[:, 0:ML_D]
        den = nd[:, ML_D:ML_D + 1]
        o_ref[0, 0, :, h * ML_D:(h + 1) * ML_D] = num / jnp.maximum(jnp.abs(den), jnp.exp(-m))

        a_col = blast - b_col + li_col
        m_loc = jnp.max(a_col, axis=0, keepdims=True)
        w_col = jnp.exp(a_col - m_loc)
        kw = (kh.astype(f32) * w_col).astype(bf16)
        dc = _dot_tn(kw, v_aug)
        m_new = jnp.maximum(blast + m0, m_loc)
        s_old = jnp.exp(blast + m0 - m_new)
        s_new = jnp.exp(m_loc - m_new)
        st_ref[h, 0:ML_D, :] = s_old * c_aug + s_new * dc
        st_ref[h, ML_D:ML_STATE_ROWS, :] = jnp.broadcast_to(m_new, (8, LANES))

    @pl.when(j == nj - 1)
    def _():
        sf_ref[0, 0] = st_ref[...]


def _mlstm(ml, mlg, gate_b2, tri, s0):
    bsz, l, _ = ml.shape
    lc = min(ML_CHUNK, l)
    nj = l // lc

    def tmap(col):
        return lambda d, b, j: (b, jnp.where(d == 0, j, nj - 1 - j), col)

    out, sf = pl.pallas_call(
        _mlstm_kernel,
        grid=(2, bsz, nj),
        in_specs=[
            pl.BlockSpec((1, lc, 256), tmap(0)),
            pl.BlockSpec((1, lc, 256), tmap(1)),
            pl.BlockSpec((1, lc, 256), tmap(2)),
            pl.BlockSpec((1, lc, LANES), lambda d, b, j: (b, jnp.where(d == 0, j, nj - 1 - j), d)),
            pl.BlockSpec((1, 1, LANES), lambda d, b, j: (d, 0, 0)),
            pl.BlockSpec((1, lc, lc), lambda d, b, j: (d, 0, 0)),
            pl.BlockSpec((1, 1, ML_H, ML_STATE_ROWS, LANES), lambda d, b, j: (d, b, 0, 0, 0)),
        ],
        out_specs=[
            pl.BlockSpec((1, 1, lc, 256), lambda d, b, j: (d, b, jnp.where(d == 0, j, nj - 1 - j), 0)),
            pl.BlockSpec((1, 1, ML_H, ML_STATE_ROWS, LANES), lambda d, b, j: (d, b, 0, 0, 0)),
        ],
        out_shape=[
            jax.ShapeDtypeStruct((2, bsz, l, 256), f32),
            jax.ShapeDtypeStruct((2, bsz, ML_H, ML_STATE_ROWS, LANES), f32),
        ],
        scratch_shapes=[pltpu.VMEM((ML_H, ML_STATE_ROWS, LANES), f32)],
        compiler_params=_cparams(("parallel", "parallel", "arbitrary")),
        name="mlstm",
    )(ml, ml, ml, mlg, gate_b2, tri, s0)
    return out, sf


def _tri_pair(n, block):
    r = jnp.arange(n)[:, None]
    c = jnp.arange(n)[None, :]
    same = (r // block) == (c // block)
    lower = (same & (r >= c)).astype(bf16)
    upper = (same & (r <= c)).astype(bf16)
    return jnp.stack([lower, upper])


HG_W = 256


def _hgrn2_kernel(q_ref, v_ref, f_ref, lbc_ref, tri_ref, s0_ref, o_ref, sf_ref,
                  st_ref, g_scr, k_scr, p_scr):
    d = pl.program_id(0)
    j = pl.program_id(2)
    nj = pl.num_programs(2)
    lg = q_ref.shape[1]
    nck = lg // HG_CHUNK

    @pl.when(j == 0)
    def _():
        st_ref[...] = s0_ref[0, 0]

    fr = f_ref[0]
    log_lb = lbc_ref[0:1, :]
    log_1m_lb = lbc_ref[1:2, :]
    one_m_lb = lbc_ref[2:3, :]
    b = log_1m_lb + _log_sigmoid(fr)
    mx = jnp.maximum(log_lb, b)
    log_f = mx + jnp.log(1.0 + jnp.exp(-jnp.abs(log_lb - b)))
    k_scr[...] = one_m_lb * _sigmoid(-fr)
    hi, lo = _split_hi_lo(log_f)
    g_scr[...] = _dot(tri_ref[0], hi) + _dot(tri_ref[0], lo)

    fwd = d == 0
    sgn = 1 - 2 * d
    rows = lax.broadcasted_iota(jnp.int32, (HG_CHUNK, 1), 0)
    rr = lax.broadcasted_iota(jnp.int32, (HG_W, HG_W), 0) // 64
    cc = lax.broadcasted_iota(jnp.int32, (HG_W, HG_W), 1) // 64
    bd = rr == cc
    bd_b = jnp.where(bd, 1.0, 0.0).astype(bf16)

    def chunk(i, carry):
        ci = jnp.where(fwd, i, nck - 1 - i)
        r0 = pl.multiple_of(ci * HG_CHUNK, HG_CHUNK)
        gc = g_scr[pl.ds(r0, HG_CHUNK), :]
        kc = k_scr[pl.ds(r0, HG_CHUNK), :]
        qc = q_ref[0, pl.ds(r0, HG_CHUNK), :].astype(f32)
        vc = v_ref[0, pl.ds(r0, HG_CHUNK), :].astype(f32)
        for s in range(HG_CHUNK):
            valid = (rows - s) * sgn >= 0
            e = jnp.exp(jnp.where(valid, gc - gc[s:s + 1, :], MASK_NEG))
            p_scr[s * HG_CHUNK:(s + 1) * HG_CHUNK, :] = (qc * kc[s:s + 1, :] * e).astype(bf16)
        abc = _dot(p_scr[...], bd_b)
        o = jnp.zeros((HG_CHUNK, HG_W), f32)
        for s in range(HG_CHUNK):
            o = o + abc[s * HG_CHUNK:(s + 1) * HG_CHUNK, :] * vc[s:s + 1, :]
        st = st_ref[...]
        o = o + _dot_nt((qc * jnp.exp(gc)).astype(bf16), st.astype(bf16))
        o_ref[0, 0, pl.ds(r0, HG_CHUNK), :] = o
        g_last = jnp.where(fwd, gc[HG_CHUNK - 1:HG_CHUNK, :], gc[0:1, :])
        kt = (kc * jnp.exp(g_last - gc)).astype(bf16)
        dst = _dot_tn(vc.astype(bf16), kt)
        st_ref[...] = st * jnp.exp(g_last) + jnp.where(bd, dst, 0.0)
        return carry

    lax.fori_loop(0, nck, chunk, 0)

    @pl.when(j == nj - 1)
    def _():
        sf_ref[0, 0] = st_ref[...]


def _hgrn2(hg, hgf, lbc, tri, s0):
    bsz, l, _ = hg.shape
    lg = min(HG_GROUP, l)
    nj = l // lg

    def tmap(col):
        return lambda d, b, j: (b, jnp.where(d == 0, j, nj - 1 - j), col)

    out, sf = pl.pallas_call(
        _hgrn2_kernel,
        grid=(2, bsz, nj),
        in_specs=[
            pl.BlockSpec((1, lg, HG_W), tmap(0)),
            pl.BlockSpec((1, lg, HG_W), tmap(1)),
            pl.BlockSpec((1, lg, HG_W), lambda d, b, j: (b, jnp.where(d == 0, j, nj - 1 - j), d)),
            pl.BlockSpec((8, HG_W), lambda d, b, j: (0, 0)),
            pl.BlockSpec((1, lg, lg), lambda d, b, j: (d, 0, 0)),
            pl.BlockSpec((1, 1, HG_W, HG_W), lambda d, b, j: (d, b, 0, 0)),
        ],
        out_specs=[
            pl.BlockSpec((1, 1, lg, HG_W), lambda d, b, j: (d, b, jnp.where(d == 0, j, nj - 1 - j), 0)),
            pl.BlockSpec((1, 1, HG_W, HG_W), lambda d, b, j: (d, b, 0, 0)),
        ],
        out_shape=[
            jax.ShapeDtypeStruct((2, bsz, l, HG_W), f32),
            jax.ShapeDtypeStruct((2, bsz, HG_W, HG_W), f32),
        ],
        scratch_shapes=[
            pltpu.VMEM((HG_W, HG_W), f32),
            pltpu.VMEM((lg, HG_W), f32),
            pltpu.VMEM((lg, HG_W), f32),
            pltpu.VMEM((HG_CHUNK * HG_CHUNK, HG_W), bf16),
        ],
        compiler_params=_cparams(("parallel", "parallel", "arbitrary")),
        name="hgrn2",
    )(hg, hg, hgf, lbc, tri, s0)
    return out, sf


DA_V = 128


def _attn_kernel(*refs, has_x, lam_init):
    if has_x:
        q_ref, kx_ref, vx_ref, kc_ref, vc_ref, lp_ref, nw_ref, o_ref = refs
    else:
        q_ref, kc_ref, vc_ref, lp_ref, nw_ref, o_ref = refs
    tq = q_ref.shape[1]
    q = q_ref[0]
    lane = lax.broadcasted_iota(jnp.int32, (tq, DA_V), 1)
    zero = jnp.zeros_like(q)
    qq = jnp.concatenate([jnp.where(lane < 64, q, zero), jnp.where(lane >= 64, q, zero)], axis=0)
    sc = _dot_nt(qq, kc_ref[0])
    m = jnp.max(sc, axis=-1, keepdims=True)
    if has_x:
        sx = _dot_nt(qq, kx_ref[0])
        m = jnp.maximum(m, jnp.max(sx, axis=-1, keepdims=True))
    pc = jnp.exp(sc - m)
    l = jnp.sum(pc, axis=-1, keepdims=True)
    acc = _dot(pc.astype(bf16), vc_ref[0])
    if has_x:
        px = jnp.exp(sx - m)
        l = l + jnp.sum(px, axis=-1, keepdims=True)
        acc = acc + _dot(px.astype(bf16), vx_ref[0])
    lp = lp_ref[...]
    lam = (jnp.exp(jnp.sum(lp[0:1] * lp[1:2], axis=-1, keepdims=True))
           - jnp.exp(jnp.sum(lp[2:3] * lp[3:4], axis=-1, keepdims=True)) + lam_init)
    o = acc[0:tq] / l[0:tq] - lam * (acc[tq:2 * tq] / l[tq:2 * tq])
    ms = jnp.mean(o * o, axis=-1, keepdims=True)
    o_ref[0] = (o * lax.rsqrt(ms + EPS) * nw_ref[...] * (1.0 - lam_init)).astype(bf16)


def _attn(da_q, da_c, lam_p, nw, lam_init, tq):
    has_x = da_q is not da_c
    bsz, l, _ = da_q.shape
    lctx = da_c.shape[1]
    nq = l // tq
    in_specs = [pl.BlockSpec((1, tq, DA_V), lambda b, h, i: (b, i, h))]
    args = [da_q]
    if has_x:
        in_specs += [
            pl.BlockSpec((1, l, DA_V), lambda b, h, i: (b, 0, 4 + h)),
            pl.BlockSpec((1, l, DA_V), lambda b, h, i: (b, 0, 8 + h)),
        ]
        args += [da_q, da_q]
    in_specs += [
        pl.BlockSpec((1, lctx, DA_V), lambda b, h, i: (b, 0, 4 + h)),
        pl.BlockSpec((1, lctx, DA_V), lambda b, h, i: (b, 0, 8 + h)),
        pl.BlockSpec((4, 64), lambda b, h, i: (0, 0)),
        pl.BlockSpec((1, DA_V), lambda b, h, i: (0, 0)),
    ]
    args += [da_c, da_c, lam_p, nw]
    return pl.pallas_call(
        functools.partial(_attn_kernel, has_x=has_x, lam_init=lam_init),
        grid=(bsz, DA_H, nq),
        in_specs=in_specs,
        out_specs=pl.BlockSpec((1, tq, DA_V), lambda b, h, i: (b, i, h)),
        out_shape=jax.ShapeDtypeStruct((bsz, l, DA_H * DA_V), bf16),
        compiler_params=_cparams(("parallel", "parallel", "arbitrary")),
        name="attn_x" if has_x else "attn_c",
    )(*args)


def _head_ms(y, bd_b):
    return _dot2(y * y, bd_b) * (1.0 / 64.0)


def _outproj_kernel(x_ref, mod_ref, mlf_ref, mlb_ref, mlo_ref, hgf_ref, hgb_ref, hgg_ref, da_ref,
                    mlw_ref, hgw_ref, w_ref, nw_ref, o_ref):
    d = x_ref.shape[-1]
    rr = lax.broadcasted_iota(jnp.int32, (256, 256), 0) // 64
    cc = lax.broadcasted_iota(jnp.int32, (256, 256), 1) // 64
    bd_b = jnp.where(rr == cc, 1.0, 0.0).astype(bf16)
    ml = mlf_ref[0, 0] + mlb_ref[0, 0]
    ml = ml * lax.rsqrt(_head_ms(ml, bd_b) + EPS) * mlw_ref[...]
    ml = _sigmoid(mlo_ref[0].astype(f32)) * ml
    hg = hgf_ref[0, 0] + hgb_ref[0, 0]
    hg = hg * lax.rsqrt(_head_ms(hg, bd_b) + EPS) * hgw_ref[...]
    gg = hgg_ref[0].astype(f32)
    hg = hg * (gg * _sigmoid(gg))
    mix = (_dot(ml.astype(bf16), w_ref[0:256, :]) + _dot(hg.astype(bf16), w_ref[256:512, :])
           + _dot(da_ref[0], w_ref[512:1024, :]))
    ms = jnp.mean(mix * mix, axis=-1, keepdims=True)
    y = mix * lax.rsqrt(ms + EPS) * nw_ref[...]
    o_ref[0] = x_ref[0] + mod_ref[0][:, 2 * d:3 * d] * y


def _outproj(x, mod, mod_row0, per_batch, ml_out, ml, hg_out, hg, da_out, mlw, hgw, w, nw, tm, name):
    bsz, l, d = x.shape
    nt = l // tm
    mod_map = (lambda b, i: (mod_row0 + b, 0, 0)) if per_batch else (lambda b, i: (mod_row0, 0, 0))
    return pl.pallas_call(
        _outproj_kernel,
        grid=(bsz, nt),
        in_specs=[
            pl.BlockSpec((1, tm, d), lambda b, i: (b, i, 0)),
            pl.BlockSpec((1, 1, mod.shape[-1]), mod_map),
            pl.BlockSpec((1, 1, tm, 256), lambda b, i: (0, b, i, 0)),
            pl.BlockSpec((1, 1, tm, 256), lambda b, i: (1, b, i, 0)),
            pl.BlockSpec((1, tm, 256), lambda b, i: (b, i, 3)),
            pl.BlockSpec((1, 1, tm, 256), lambda b, i: (0, b, i, 0)),
            pl.BlockSpec((1, 1, tm, 256), lambda b, i: (1, b, i, 0)),
            pl.BlockSpec((1, tm, 256), lambda b, i: (b, i, 2)),
            pl.BlockSpec((1, tm, 512), lambda b, i: (b, i, 0)),
            pl.BlockSpec((1, 256), lambda b, i: (0, 0)),
            pl.BlockSpec((1, 256), lambda b, i: (0, 0)),
            pl.BlockSpec((d, d), lambda b, i: (0, 0)),
            pl.BlockSpec((1, d), lambda b, i: (0, 0)),
        ],
        out_specs=pl.BlockSpec((1, tm, d), lambda b, i: (b, i, 0)),
        out_shape=jax.ShapeDtypeStruct((bsz, l, d), f32),
        compiler_params=_cparams(("parallel", "parallel")),
        name=name,
    )(x, mod, ml_out, ml_out, ml, hg_out, hg_out, hg, da_out, mlw, hgw, w, nw)


def _ffn_kernel(*refs, moe):
    if moe:
        x_ref, mod_ref, nw1_ref, rw_ref, rb_ref, wg_ref, wu_ref, wd_ref, nw2_ref, o_ref, h_scr, acc_scr, gate_scr = refs
    else:
        x_ref, mod_ref, nw1_ref, wg_ref, wu_ref, wd_ref, nw2_ref, o_ref, h_scr, acc_scr = refs
    e = pl.program_id(2)
    j = pl.program_id(3)
    ne = pl.num_programs(2)
    nj = pl.num_programs(3)
    d = x_ref.shape[-1]
    tm = x_ref.shape[1]

    @pl.when((e == 0) & (j == 0))
    def _():
        x = x_ref[0]
        ms = jnp.mean(x * x, axis=-1, keepdims=True)
        h = x * lax.rsqrt(ms + EPS) * nw1_ref[...]
        mod = mod_ref[0]
        h = h * (1.0 + mod[:, 4 * d:5 * d]) + mod[:, 3 * d:4 * d]
        h_scr[...] = h.astype(bf16)
        acc_scr[...] = jnp.zeros_like(acc_scr)
        if moe:
            hh, hl = _split_hi_lo(h)
            rw = rw_ref[...]
            wh, wl = _split_hi_lo(rw)
            logits = _dot(hh, wh) + _dot(hl, wh) + _dot(hh, wl) + rb_ref[...]
            lane = lax.broadcasted_iota(jnp.int32, (tm, LANES), 1)
            real = lane < N_EXPERTS
            lg = jnp.where(real, logits, MASK_NEG)
            m1 = jnp.max(lg, axis=-1, keepdims=True)
            i1 = jnp.min(jnp.where(lg == m1, lane, LANES), axis=-1, keepdims=True)
            lg2 = jnp.where(lane == i1, MASK_NEG, lg)
            m2 = jnp.max(lg2, axis=-1, keepdims=True)
            i2 = jnp.min(jnp.where(lg2 == m2, lane, LANES), axis=-1, keepdims=True)
            e2 = jnp.exp(m2 - m1)
            g1 = 1.0 / (1.0 + e2)
            g2 = e2 / (1.0 + e2)
            gate_scr[...] = jnp.where(lane == i1, g1, 0.0) + jnp.where(lane == i2, g2, 0.0)

    hb = h_scr[...]
    a = _dot(hb, wg_ref[0])
    u = _dot(hb, wu_ref[0])
    act = (a * _sigmoid(a)) * u
    if moe:
        lane = lax.broadcasted_iota(jnp.int32, (tm, LANES), 1)
        ge = jnp.sum(jnp.where(lane == e, gate_scr[...], 0.0), axis=-1, keepdims=True)
        act = act * ge
    acc_scr[...] += _dot(act.astype(bf16), wd_ref[0])

    @pl.when((e == ne - 1) & (j == nj - 1))
    def _():
        y = acc_scr[...]
        ms = jnp.mean(y * y, axis=-1, keepdims=True)
        y = y * lax.rsqrt(ms + EPS) * nw2_ref[...]
        o_ref[0] = x_ref[0] + mod_ref[0][:, 5 * d:6 * d] * y


def _ffn(x, mod, mod_row0, per_batch, nw1, wg, wu, wd, nw2, tm, tf, router=None, name="ffn"):
    bsz, l, d = x.shape
    ne, _, ff = wg.shape
    nt = l // tm
    nf = ff // tf
    moe = router is not None
    mod_map = (lambda b, i, e, j: (mod_row0 + b, 0, 0)) if per_batch else (lambda b, i, e, j: (mod_row0, 0, 0))
    in_specs = [
        pl.BlockSpec((1, tm, d), lambda b, i, e, j: (b, i, 0)),
        pl.BlockSpec((1, 1, mod.shape[-1]), mod_map),
        pl.BlockSpec((1, d), lambda b, i, e, j: (0, 0)),
    ]
    args = [x, mod, nw1]
    scratch = [pltpu.VMEM((tm, d), bf16), pltpu.VMEM((tm, d), f32)]
    if moe:
        in_specs += [
            pl.BlockSpec((d, LANES), lambda b, i, e, j: (0, 0)),
            pl.BlockSpec((1, LANES), lambda b, i, e, j: (0, 0)),
        ]
        args += list(router)
        scratch += [pltpu.VMEM((tm, LANES), f32)]
    in_specs += [
        pl.BlockSpec((1, d, tf), lambda b, i, e, j: (e, 0, j)),
        pl.BlockSpec((1, d, tf), lambda b, i, e, j: (e, 0, j)),
        pl.BlockSpec((1, tf, d), lambda b, i, e, j: (e, j, 0)),
        pl.BlockSpec((1, d), lambda b, i, e, j: (0, 0)),
    ]
    args += [wg, wu, wd, nw2]
    return pl.pallas_call(
        functools.partial(_ffn_kernel, moe=moe),
        grid=(bsz, nt, ne, nf),
        in_specs=in_specs,
        out_specs=pl.BlockSpec((1, tm, d), lambda b, i, e, j: (b, i, 0)),
        out_shape=jax.ShapeDtypeStruct((bsz, l, d), f32),
        scratch_shapes=scratch,
        compiler_params=_cparams(("parallel", "parallel", "arbitrary", "arbitrary")),
        name=name,
    )(*args)


def _pick_tile(n, prefs):
    for t in prefs:
        if n % t == 0:
            return t
    return n


def kernel(x, c, ctx, c_ctx, w_mod, b_mod, norm_pre_mix, norm_post_mix, norm_pre_ffn, norm_post_ffn, w_in, w_out, ml_gate_b, ml_norm, hg_lb_logits, hg_norm, da_lambda, da_norm, ffn_w_gate, ffn_w_up, ffn_w_down, router_w, router_b, moe_w_gate, moe_w_up, moe_w_down):
    depth = w_mod.shape[0]
    bsz, l, d = x.shape
    lctx = ctx.shape[1]

    mod = _mod_vectors(c, c_ctx, w_mod, b_mod)
    rows = mod.shape[1]
    mod = mod.reshape(depth * rows, 1, 6 * d)

    lb_soft = jax.nn.softmax(hg_lb_logits.astype(f32), axis=0)
    lower_bounds = jnp.cumsum(lb_soft, axis=0) - lb_soft[0]

    tables = _rope_tables(l)
    ml_lc = min(ML_CHUNK, lctx)
    tri_ml_x = _tri_pair(min(ML_CHUNK, l), min(ML_CHUNK, l))
    tri_ml_c = _tri_pair(ml_lc, ml_lc)
    tri_hg_x = _tri_pair(min(HG_GROUP, l), HG_CHUNK)
    tri_hg_c = _tri_pair(min(HG_GROUP, lctx), HG_CHUNK)

    tm_x = _pick_tile(l, (512, 256, 128))
    tm_c = _pick_tile(lctx, (256, 128))
    tq = _pick_tile(l, (256, 128))

    for layer in range(depth):
        need_ctx = layer < depth - 1
        lam_init = 0.8 - 0.6 * math.exp(-0.3 * layer)
        row0 = layer * rows
        w_in_l = _prep_w_in(w_in[layer])
        w_out_l = w_out[layer].astype(bf16)
        nw_pre = norm_pre_mix[layer].reshape(1, d)

        pc = _inproj(ctx, mod, row0 + bsz, False, nw_pre, w_in_l, None, tm_c)
        px = _inproj(x, mod, row0, True, nw_pre, w_in_l, tables, tm_x)
        ml_c, mlg_c, hg_c, hgf_c, da_c = pc
        ml_x, mlg_x, hg_x, hgf_x, da_x = px

        gb = ml_gate_b[layer].astype(f32)
        z = jnp.zeros((LANES - 8,), f32)
        gate_b2 = jnp.stack([jnp.concatenate([gb[0:8], z]), jnp.concatenate([gb[8:16], z])]).reshape(2, 1, LANES)
        s0 = jnp.zeros((2, bsz, ML_H, ML_STATE_ROWS, LANES), f32)
        mlo_c, s_ml = _mlstm(ml_c, mlg_c, gate_b2, tri_ml_c, s0)
        mlo_x, _ = _mlstm(ml_x, mlg_x, gate_b2, tri_ml_x, s_ml)

        lb = lower_bounds[layer]
        lbc = jnp.zeros((8, HG_W), f32)
        lbc = lbc.at[0].set(jnp.log(jnp.maximum(lb, LB_FLOOR))).at[1].set(jnp.log1p(-lb)).at[2].set(1.0 - lb)
        h0 = jnp.zeros((2, bsz, HG_W, HG_W), f32)
        hgo_c, s_hg = _hgrn2(hg_c, hgf_c, lbc, tri_hg_c, h0)
        hgo_x, _ = _hgrn2(hg_x, hgf_x, lbc, tri_hg_x, s_hg)

        lam_p = da_lambda[layer].astype(f32)
        da_nw = da_norm[layer].reshape(1, DA_V)
        dao_x = _attn(da_x, da_c, lam_p, da_nw, lam_init, tq)

        mlw = ml_norm[layer].reshape(1, 256)
        hgw = hg_norm[layer].reshape(1, 256)
        nw_post = norm_post_mix[layer].reshape(1, d)
        x = _outproj(x, mod, row0, True, mlo_x, ml_x, hgo_x, hg_x, dao_x, mlw, hgw, w_out_l, nw_post, tm_x, "outproj_x")
        if need_ctx:
            dao_c = _attn(da_c, da_c, lam_p, da_nw, lam_init, lctx)
            ctx = _outproj(ctx, mod, row0 + bsz, False, mlo_c, ml_c, hgo_c, hg_c, dao_c, mlw, hgw, w_out_l, nw_post, tm_c, "outproj_c")

        i = layer // 2
        nw1 = norm_pre_ffn[layer].reshape(1, d)
        nw2 = norm_post_ffn[layer].reshape(1, d)
        if layer % 2 == 0:
            wg = ffn_w_gate[i].astype(bf16)[None]
            wu = ffn_w_up[i].astype(bf16)[None]
            wd = ffn_w_down[i].astype(bf16)[None]
            router = None
        else:
            wg = moe_w_gate[i].astype(bf16)
            wu = moe_w_up[i].astype(bf16)
            wd = moe_w_down[i].astype(bf16)
            rw = jnp.zeros((d, LANES), f32).at[:, :N_EXPERTS].set(router_w[i])
            rb = jnp.zeros((1, LANES), f32).at[0, :N_EXPERTS].set(router_b[i])
            router = (rw, rb)
        ff = wg.shape[-1]
        tf = _pick_tile(ff, (896, 1408, 512, 256))
        tm_f = _pick_tile(l, (1024, 512, 256, 128) if tf <= 896 else (512, 256, 128))
        x = _ffn(x, mod, row0, True, nw1, wg, wu, wd, nw2, tm_f, tf, router, name="ffn_x%d" % layer)
        if need_ctx:
            ctx = _ffn(ctx, mod, row0 + bsz, False, nw1, wg, wu, wd, nw2, tm_c, tf, router, name="ffn_c%d" % layer)
    return x
```

```python
import functools
import math

import jax
import jax.numpy as jnp
from jax import lax
from jax.experimental import pallas as pl
from jax.experimental.pallas import tpu as pltpu

f32 = jnp.float32
bf16 = jnp.bfloat16

ML_H = 4
HG_H = 4
DA_H = 4
GRID_W = 64
ROPE_BASE = 10000.0
N_EXPERTS = 8
EPS = 1e-6
MASK_NEG = -1e30
LB_FLOOR = 1e-30

LANES = 128
ML_CHUNK = 128
HG_CHUNK = 16
HG_GROUP = 128
VMEM_LIMIT = 56 * 1024 * 1024


def _cparams(sem):
    return pltpu.CompilerParams(dimension_semantics=sem, vmem_limit_bytes=VMEM_LIMIT)


def _sigmoid(x):
    return 1.0 / (1.0 + jnp.exp(-x))


def _log_sigmoid(x):
    return jnp.minimum(x, 0.0) - jnp.log(1.0 + jnp.exp(-jnp.abs(x)))


def _split_hi_lo(a):
    hi = a.astype(bf16)
    lo = (a - hi.astype(f32)).astype(bf16)
    return hi, lo


def _dot(a, b):
    return jnp.dot(a, b, preferred_element_type=f32)


def _dot_nt(a, b):
    return lax.dot_general(a, b, (((1,), (1,)), ((), ())), preferred_element_type=f32)


def _dot_tn(a, b):
    return lax.dot_general(a, b, (((0,), (0,)), ((), ())), preferred_element_type=f32)


def _dot2(a_f32, b_bf16):
    hi, lo = _split_hi_lo(a_f32)
    return _dot(hi, b_bf16) + _dot(lo, b_bf16)


def _mod_kernel(s_ref, w_ref, b_ref, o_ref):
    s = s_ref[...]
    s = s * _sigmoid(s)
    o_ref[0] = _dot(s.astype(bf16), w_ref[0].astype(bf16)) + b_ref[0]


def _mod_vectors(c, c_ctx, w_mod, b_mod):
    depth, d, n = w_mod.shape
    bsz = c.shape[0]
    rows = 8
    s = jnp.zeros((rows, d), f32).at[:bsz].set(c).at[bsz].set(c_ctx)
    tn = n // 4
    out = pl.pallas_call(
        _mod_kernel,
        grid=(depth, n // tn),
        in_specs=[
            pl.BlockSpec((rows, d), lambda l, j: (0, 0)),
            pl.BlockSpec((1, d, tn), lambda l, j: (l, 0, j)),
            pl.BlockSpec((1, 1, tn), lambda l, j: (l, 0, j)),
        ],
        out_specs=pl.BlockSpec((1, rows, tn), lambda l, j: (l, 0, j)),
        out_shape=jax.ShapeDtypeStruct((depth, rows, n), f32),
        compiler_params=_cparams(("parallel", "parallel")),
        name="mod_vectors",
    )(s, w_mod, b_mod.reshape(depth, 1, n))
    return out


P_ML = (0, 1024)
P_MLG = (1024, 1280)
P_HG = (1280, 2048)
P_HGF = (2048, 2560)
P_DA = (2560, 4096)
P_TOTAL = 4096


def _prep_w_in(w):
    d = w.shape[0]
    ml = w[:, 0:1024]
    g = w[:, 1024:1040]
    z = jnp.zeros((d, LANES - 8), w.dtype)
    gates = jnp.concatenate([g[:, 0:8], z, g[:, 8:16], z], axis=1)
    hg_q = w[:, 1040:1296]
    hg_f = w[:, 1296:1808]
    hg_i = w[:, 1808:2064]
    hg_g = w[:, 2064:2320]
    da = w[:, 2320:3856]
    return jnp.concatenate([ml, gates, hg_q, hg_i, hg_g, hg_f, da], axis=1).astype(bf16)


def _inproj_kernel(*refs, rope):
    if rope:
        x_ref, mod_ref, nw_ref, w_ref, cos_ref, sin_ref = refs[:6]
        outs = refs[6:]
    else:
        x_ref, mod_ref, nw_ref, w_ref = refs[:4]
        outs = refs[4:]
    ml_ref, mlg_ref, hg_ref, hgf_ref, da_ref = outs
    d = x_ref.shape[-1]
    x = x_ref[0]
    ms = jnp.mean(x * x, axis=-1, keepdims=True)
    h = x * lax.rsqrt(ms + EPS) * nw_ref[...]
    mod = mod_ref[0]
    h = h * (1.0 + mod[:, d:2 * d]) + mod[:, 0:d]
    hb = h.astype(bf16)

    def proj(lo, hi):
        return _dot(hb, w_ref[:, lo:hi])

    ml_ref[0] = proj(*P_ML).astype(bf16)
    mlg_ref[0] = proj(*P_MLG)
    pq = proj(P_HG[0], P_HG[0] + 256)
    hg_ref[0, :, 0:256] = (pq * _sigmoid(pq)).astype(bf16)
    hg_ref[0, :, 256:768] = proj(P_HG[0] + 256, P_HG[1]).astype(bf16)
    hgf_ref[0] = proj(*P_HGF)
    q = proj(P_DA[0], P_DA[0] + 512)
    k = proj(P_DA[0] + 512, P_DA[0] + 1024)
    if rope:
        cos = jnp.concatenate([cos_ref[...]] * 4, axis=1)
        sin = jnp.concatenate([sin_ref[...]] * 4, axis=1)
        lane = lax.broadcasted_iota(jnp.int32, (1, 512), 1)
        first = (lane % 32) < 16

        def rot(t):
            sw = jnp.where(first, pltpu.roll(t, 512 - 16, 1), pltpu.roll(t, 16, 1))
            return t * cos + sw * sin

        q = rot(q)
        k = rot(k)
    da_ref[0, :, 0:512] = (q * 0.125).astype(bf16)
    da_ref[0, :, 512:1024] = k.astype(bf16)
    da_ref[0, :, 1024:1536] = proj(P_DA[0] + 1024, P_DA[1]).astype(bf16)


def _inproj(x, mod, mod_row0, per_batch, nw, w, tables, tm):
    bsz, l, d = x.shape
    rope = tables is not None
    nt = l // tm
    mod_map = (lambda b, i: (mod_row0 + b, 0, 0)) if per_batch else (lambda b, i: (mod_row0, 0, 0))
    in_specs = [
        pl.BlockSpec((1, tm, d), lambda b, i: (b, i, 0)),
        pl.BlockSpec((1, 1, mod.shape[-1]), mod_map),
        pl.BlockSpec((1, d), lambda b, i: (0, 0)),
        pl.BlockSpec((d, P_TOTAL), lambda b, i: (0, 0)),
    ]
    args = [x, mod, nw, w]
    if rope:
        in_specs += [pl.BlockSpec((tm, LANES), lambda b, i: (i, 0))] * 2
        args += list(tables)
    widths = (1024, 256, 768, 512, 1536)
    dtypes = (bf16, f32, bf16, f32, bf16)
    out_specs = [pl.BlockSpec((1, tm, wd), lambda b, i: (b, i, 0)) for wd in widths]
    out_shape = [jax.ShapeDtypeStruct((bsz, l, wd), dt) for wd, dt in zip(widths, dtypes)]
    return pl.pallas_call(
        functools.partial(_inproj_kernel, rope=rope),
        grid=(bsz, nt),
        in_specs=in_specs,
        out_specs=out_specs,
        out_shape=out_shape,
        compiler_params=_cparams(("parallel", "parallel")),
        name="inproj_x" if rope else "inproj_c",
    )(*args)


def _rope_tables(l):
    rows = l // GRID_W
    row = jnp.repeat(jnp.arange(rows), GRID_W).astype(f32)
    col = jnp.tile(jnp.arange(GRID_W), rows).astype(f32)
    n_freq = 16
    inv = ROPE_BASE ** (-jnp.arange(n_freq, dtype=f32) / n_freq)
    lane = jnp.arange(LANES)
    freq = inv[lane % n_freq]
    pos = jnp.where(((lane % 64) // 32 == 0)[None, :], row[:, None], col[:, None])
    ang = pos * freq[None, :]
    sign = jnp.where((lane % 32) < 16, -1.0, 1.0)[None, :]
    return jnp.cos(ang), jnp.sin(ang) * sign


ML_D = 64
ML_STATE_ROWS = ML_D + 8


def _mlstm_kernel(q_ref, k_ref, v_ref, g_ref, gb_ref, tri_ref, s0_ref, o_ref, sf_ref, st_ref):
    d = pl.program_id(0)
    j = pl.program_id(2)
    nj = pl.num_programs(2)
    lc = q_ref.shape[1]

    @pl.when(j == 0)
    def _():
        st_ref[...] = s0_ref[0, 0]

    g = g_ref[0] + gb_ref[0]
    lf = _log_sigmoid(g)
    hi, lo = _split_hi_lo(lf)
    bc = _dot(tri_ref[0], hi) + _dot(tri_ref[0], lo)
    bt = bc.T
    gt = g.T
    fwd = d == 0
    blast_row = jnp.where(fwd, bc[lc - 1:lc, :], bc[0:1, :])

    r = lax.broadcasted_iota(jnp.int32, (lc, lc), 0)
    c = lax.broadcasted_iota(jnp.int32, (lc, lc), 1)
    sgn = 1 - 2 * d
    mask = (r - c) * sgn >= 0
    lane = lax.broadcasted_iota(jnp.int32, (lc, ML_D), 1)
    ones_col = jnp.where(lane == 0, 1.0, 0.0).astype(bf16)

    for h in range(ML_H):
        b_col = bc[:, 4 + h:5 + h]
        b_row = bt[4 + h:5 + h, :]
        li_col = g[:, h:h + 1]
        li_row = gt[h:h + 1, :]
        blast = blast_row[:, 4 + h:5 + h]
        c_aug = st_ref[h, 0:ML_D, :]
        m0 = st_ref[h, ML_D:ML_D + 1, 0:1]

        dm = jnp.where(mask, b_col - b_row + li_row, MASK_NEG)
        inter = b_col + m0
        m = jnp.maximum(inter, jnp.max(dm, axis=-1, keepdims=True))
        w_intra = jnp.exp(dm - m)
        w_inter = jnp.exp(inter - m)

        qh = q_ref[0, :, h * ML_D:(h + 1) * ML_D]
        kh = k_ref[0, :, h * ML_D:(h + 1) * ML_D] * jnp.asarray(0.125, bf16)
        vh = v_ref[0, :, h * ML_D:(h + 1) * ML_D]
        v_aug = jnp.concatenate([vh, ones_col], axis=1)
        s = _dot_nt(qh, kh) * w_intra
        nd = _dot(s.astype(bf16), v_aug) + w_inter * _dot(qh, c_aug.astype(bf16))
        num = nd[:, 0:ML_D]
        den = nd[:, ML_D:ML_D + 1]
        o_ref[0, 0, :, h * ML_D:(h + 1) * ML_D] = num / jnp.maximum(jnp.abs(den), jnp.exp(-m))

        a_col = blast - b_col + li_col
        m_loc = jnp.max(a_col, axis=0, keepdims=True)
        w_col = jnp.exp(a_col - m_loc)
        kw = (kh.astype(f32) * w_col).astype(bf16)
        dc = _dot_tn(kw, v_aug)
        m_new = jnp.maximum(blast + m0, m_loc)
        s_old = jnp.exp(blast + m0 - m_new)
        s_new = jnp.exp(m_loc - m_new)
        st_ref[h, 0:ML_D, :] = s_old * c_aug + s_new * dc
        st_ref[h, ML_D:ML_STATE_ROWS, :] = jnp.broadcast_to(m_new, (8, LANES))

    @pl.when(j == nj - 1)
    def _():
        sf_ref[0, 0] = st_ref[...]


def _mlstm(ml, mlg, gate_b2, tri, s0):
    bsz, l, _ = ml.shape
    lc = min(ML_CHUNK, l)
    nj = l // lc

    def tmap(col):
        return lambda d, b, j: (b, jnp.where(d == 0, j, nj - 1 - j), col)

    out, sf = pl.pallas_call(
        _mlstm_kernel,
        grid=(2, bsz, nj),
        in_specs=[
            pl.BlockSpec((1, lc, 256), tmap(0)),
            pl.BlockSpec((1, lc, 256), tmap(1)),
            pl.BlockSpec((1, lc, 256), tmap(2)),
            pl.BlockSpec((1, lc, LANES), lambda d, b, j: (b, jnp.where(d == 0, j, nj - 1 - j), d)),
            pl.BlockSpec((1, 1, LANES), lambda d, b, j: (d, 0, 0)),
            pl.BlockSpec((1, lc, lc), lambda d, b, j: (d, 0, 0)),
            pl.BlockSpec((1, 1, ML_H, ML_STATE_ROWS, LANES), lambda d, b, j: (d, b, 0, 0, 0)),
        ],
        out_specs=[
            pl.BlockSpec((1, 1, lc, 256), lambda d, b, j: (d, b, jnp.where(d == 0, j, nj - 1 - j), 0)),
            pl.BlockSpec((1, 1, ML_H, ML_STATE_ROWS, LANES), lambda d, b, j: (d, b, 0, 0, 0)),
        ],
        out_shape=[
            jax.ShapeDtypeStruct((2, bsz, l, 256), f32),
            jax.ShapeDtypeStruct((2, bsz, ML_H, ML_STATE_ROWS, LANES), f32),
        ],
        scratch_shapes=[pltpu.VMEM((ML_H, ML_STATE_ROWS, LANES), f32)],
        compiler_params=_cparams(("parallel", "parallel", "arbitrary")),
        name="mlstm",
    )(ml, ml, ml, mlg, gate_b2, tri, s0)
    return out, sf


def _tri_pair(n, block):
    r = jnp.arange(n)[:, None]
    c = jnp.arange(n)[None, :]
    same = (r // block) == (c // block)
    lower = (same & (r >= c)).astype(bf16)
    upper = (same & (r <= c)).astype(bf16)
    return jnp.stack([lower, upper])


HG_W = 256


def _hgrn2_kernel(q_ref, v_ref, f_ref, lbc_ref, tri_ref, s0_ref, o_ref, sf_ref,
                  st_ref, g_scr, k_scr, p_scr):
    d = pl.program_id(0)
    j = pl.program_id(2)
    nj = pl.num_programs(2)
    lg = q_ref.shape[1]
    nck = lg // HG_CHUNK

    @pl.when(j == 0)
    def _():
        st_ref[...] = s0_ref[0, 0]

    fr = f_ref[0]
    log_lb = lbc_ref[0:1, :]
    log_1m_lb = lbc_ref[1:2, :]
    one_m_lb = lbc_ref[2:3, :]
    b = log_1m_lb + _log_sigmoid(fr)
    mx = jnp.maximum(log_lb, b)
    log_f = mx + jnp.log(1.0 + jnp.exp(-jnp.abs(log_lb - b)))
    k_scr[...] = one_m_lb * _sigmoid(-fr)
    hi, lo = _split_hi_lo(log_f)
    g_scr[...] = _dot(tri_ref[0], hi) + _dot(tri_ref[0], lo)

    fwd = d == 0
    sgn = 1 - 2 * d
    rows = lax.broadcasted_iota(jnp.int32, (HG_CHUNK, 1), 0)
    rr = lax.broadcasted_iota(jnp.int32, (HG_W, HG_W), 0) // 64
    cc = lax.broadcasted_iota(jnp.int32, (HG_W, HG_W), 1) // 64
    bd = rr == cc
    bd_b = jnp.where(bd, 1.0, 0.0).astype(bf16)

    def chunk(i, carry):
        ci = jnp.where(fwd, i, nck - 1 - i)
        r0 = pl.multiple_of(ci * HG_CHUNK, HG_CHUNK)
        gc = g_scr[pl.ds(r0, HG_CHUNK), :]
        kc = k_scr[pl.ds(r0, HG_CHUNK), :]
        qc = q_ref[0, pl.ds(r0, HG_CHUNK), :].astype(f32)
        vc = v_ref[0, pl.ds(r0, HG_CHUNK), :].astype(f32)
        for s in range(HG_CHUNK):
            valid = (rows - s) * sgn >= 0
            e = jnp.exp(jnp.where(valid, gc - gc[s:s + 1, :], MASK_NEG))
            p_scr[s * HG_CHUNK:(s + 1) * HG_CHUNK, :] = (qc * kc[s:s + 1, :] * e).astype(bf16)
        abc = _dot(p_scr[...], bd_b)
        o = jnp.zeros((HG_CHUNK, HG_W), f32)
        for s in range(HG_CHUNK):
            o = o + abc[s * HG_CHUNK:(s + 1) * HG_CHUNK, :] * vc[s:s + 1, :]
        st = st_ref[...]
        o = o + _dot_nt((qc * jnp.exp(gc)).astype(bf16), st.astype(bf16))
        o_ref[0, 0, pl.ds(r0, HG_CHUNK), :] = o
        g_last = jnp.where(fwd, gc[HG_CHUNK - 1:HG_CHUNK, :], gc[0:1, :])
        kt = (kc * jnp.exp(g_last - gc)).astype(bf16)
        dst = _dot_tn(vc.astype(bf16), kt)
        st_ref[...] = st * jnp.exp(g_last) + jnp.where(bd, dst, 0.0)
        return carry

    lax.fori_loop(0, nck, chunk, 0)

    @pl.when(j == nj - 1)
    def _():
        sf_ref[0, 0] = st_ref[...]


def _hgrn2(hg, hgf, lbc, tri, s0):
    bsz, l, _ = hg.shape
    lg = min(HG_GROUP, l)
    nj = l // lg

    def tmap(col):
        return lambda d, b, j: (b, jnp.where(d == 0, j, nj - 1 - j), col)

    out, sf = pl.pallas_call(
        _hgrn2_kernel,
        grid=(2, bsz, nj),
        in_specs=[
            pl.BlockSpec((1, lg, HG_W), tmap(0)),
            pl.BlockSpec((1, lg, HG_W), tmap(1)),
            pl.BlockSpec((1, lg, HG_W), lambda d, b, j: (b, jnp.where(d == 0, j, nj - 1 - j), d)),
            pl.BlockSpec((8, HG_W), lambda d, b, j: (0, 0)),
            pl.BlockSpec((1, lg, lg), lambda d, b, j: (d, 0, 0)),
            pl.BlockSpec((1, 1, HG_W, HG_W), lambda d, b, j: (d, b, 0, 0)),
        ],
        out_specs=[
            pl.BlockSpec((1, 1, lg, HG_W), lambda d, b, j: (d, b, jnp.where(d == 0, j, nj - 1 - j), 0)),
            pl.BlockSpec((1, 1, HG_W, HG_W), lambda d, b, j: (d, b, 0, 0)),
        ],
        out_shape=[
            jax.ShapeDtypeStruct((2, bsz, l, HG_W), f32),
            jax.ShapeDtypeStruct((2, bsz, HG_W, HG_W), f32),
        ],
        scratch_shapes=[
            pltpu.VMEM((HG_W, HG_W), f32),
            pltpu.VMEM((lg, HG_W), f32),
            pltpu.VMEM((lg, HG_W), f32),
            pltpu.VMEM((HG_CHUNK * HG_CHUNK, HG_W), bf16),
        ],
        compiler_params=_cparams(("parallel", "parallel", "arbitrary")),
        name="hgrn2",
    )(hg, hg, hgf, lbc, tri, s0)
    return out, sf


DA_V = 128


def _attn_kernel(*refs, has_x, lam_init, kg, nq):
    if has_x:
        q_ref, kx_ref, kc_ref, vx_ref, vc_ref, lp_ref, nw_ref, o_ref, s0_scr, s1_scr, m0_scr, m1_scr, va_scr = refs
        lx = kx_ref.shape[1]
    else:
        q_ref, kc_ref, vc_ref, lp_ref, nw_ref, o_ref, s0_scr, s1_scr, m0_scr, m1_scr, va_scr = refs
        lx = 0
    tq = q_ref.shape[1]
    lctx = kc_ref.shape[1]
    t = pl.program_id(0)

    @pl.when(t == 0)
    def _():
        s1_scr[...] = jnp.zeros(s1_scr.shape, f32)
        m1_scr[...] = jnp.zeros(m1_scr.shape, f32)

    @pl.when((t == 0) | ((t - 1) % nq == 0))
    def _():
        def ones_col(n):
            lane = lax.broadcasted_iota(jnp.int32, (n, DA_V), 1)
            return jnp.where(lane == 0, 1.0, 0.0).astype(bf16)

        if has_x:
            va_scr[0:lx, 0:DA_V] = vx_ref[0]
            va_scr[0:lx, DA_V:2 * DA_V] = ones_col(lx)
        va_scr[lx:lx + lctx, 0:DA_V] = vc_ref[0]
        va_scr[lx:lx + lctx, DA_V:2 * DA_V] = ones_col(lctx)

    groups = ([(kx_ref, g * kg, kg, g * kg) for g in range(lx // kg)] if has_x else []) + [(kc_ref, 0, lctx, lx)]

    def step(sa_scr, ma_scr, sb_scr, mb_scr):
        q = q_ref[0]
        lane = lax.broadcasted_iota(jnp.int32, (tq, DA_V), 1)
        zero = jnp.zeros_like(q)
        qq = jnp.concatenate([jnp.where(lane < 64, q, zero), jnp.where(lane >= 64, q, zero)], axis=0)
        m128 = None
        for ref, r0, n, c0 in groups:
            s = _dot_nt(qq, ref[0, r0:r0 + n, :])
            sa_scr[:, c0:c0 + n] = s
            for c in range(n // LANES):
                blk = s[:, c * LANES:(c + 1) * LANES]
                m128 = blk if m128 is None else jnp.maximum(m128, blk)
        ma_scr[...] = jnp.broadcast_to(jnp.max(m128, axis=-1, keepdims=True), m128.shape)
        m_prev = mb_scr[...]
        acc = None
        for ref, r0, n, c0 in groups:
            m_rep = jnp.concatenate([m_prev] * (n // LANES), axis=1)
            p = jnp.exp(sb_scr[:, c0:c0 + n] - m_rep).astype(bf16)
            part = _dot(p, va_scr[c0:c0 + n, :])
            acc = part if acc is None else acc + part
        l = acc[:, DA_V:DA_V + 1]
        acc = acc[:, 0:DA_V]
        lp = lp_ref[...]
        lam = (jnp.exp(jnp.sum(lp[0:1] * lp[1:2], axis=-1, keepdims=True))
               - jnp.exp(jnp.sum(lp[2:3] * lp[3:4], axis=-1, keepdims=True)) + lam_init)
        o = acc[0:tq] / l[0:tq] - lam * (acc[tq:2 * tq] / l[tq:2 * tq])
        ms = jnp.mean(o * o, axis=-1, keepdims=True)
        o_ref[0] = (o * lax.rsqrt(ms + EPS) * nw_ref[...] * (1.0 - lam_init)).astype(bf16)

    @pl.when(t % 2 == 0)
    def _():
        step(s0_scr, m0_scr, s1_scr, m1_scr)

    @pl.when(t % 2 == 1)
    def _():
        step(s1_scr, m1_scr, s0_scr, m0_scr)


def _attn(da_q, da_c, lam_p, nw, lam_init, tq):
    has_x = da_q is not da_c
    bsz, l, _ = da_q.shape
    lctx = da_c.shape[1]
    nq = l // tq
    nt = bsz * DA_H * nq
    lx = l if has_x else 0
    kg = _pick_tile(l, (1024, 512, 256, 128))

    def split(t):
        return t // (DA_H * nq), (t // nq) % DA_H, t % nq

    def cur(col0, whole):
        def index(t):
            b, h, i = split(jnp.minimum(t, nt - 1))
            return (b, 0 if whole else i, col0 + h)
        return index

    def prev(col0, whole):
        def index(t):
            b, h, i = split(jnp.maximum(t - 1, 0))
            return (b, 0 if whole else i, col0 + h)
        return index

    in_specs = [pl.BlockSpec((1, tq, DA_V), cur(0, False))]
    args = [da_q]
    if has_x:
        in_specs += [pl.BlockSpec((1, l, DA_V), cur(4, True))]
        args += [da_q]
    in_specs += [pl.BlockSpec((1, lctx, DA_V), cur(4, True))]
    args += [da_c]
    if has_x:
        in_specs += [pl.BlockSpec((1, l, DA_V), prev(8, True))]
        args += [da_q]
    in_specs += [
        pl.BlockSpec((1, lctx, DA_V), prev(8, True)),
        pl.BlockSpec((4, 64), lambda t: (0, 0)),
        pl.BlockSpec((1, DA_V), lambda t: (0, 0)),
    ]
    args += [da_c, lam_p, nw]
    return pl.pallas_call(
        functools.partial(_attn_kernel, has_x=has_x, lam_init=lam_init, kg=kg, nq=nq),
        grid=(nt + 1,),
        in_specs=in_specs,
        out_specs=pl.BlockSpec((1, tq, DA_V), prev(0, False)),
        out_shape=jax.ShapeDtypeStruct((bsz, l, DA_H * DA_V), bf16),
        scratch_shapes=[
            pltpu.VMEM((2 * tq, lx + lctx), f32),
            pltpu.VMEM((2 * tq, lx + lctx), f32),
            pltpu.VMEM((2 * tq, LANES), f32),
            pltpu.VMEM((2 * tq, LANES), f32),
            pltpu.VMEM((lx + lctx, 2 * DA_V), bf16),
        ],
        compiler_params=_cparams(("arbitrary",)),
        name="attn_x" if has_x else "attn_c",
    )(*args)


def _head_ms(y, bd_b):
    return _dot2(y * y, bd_b) * (1.0 / 64.0)


def _outproj_kernel(x_ref, mod_ref, mlf_ref, mlb_ref, mlo_ref, hgf_ref, hgb_ref, hgg_ref, da_ref,
                    mlw_ref, hgw_ref, w_ref, nw_ref, o_ref):
    d = x_ref.shape[-1]
    rr = lax.broadcasted_iota(jnp.int32, (256, 256), 0) // 64
    cc = lax.broadcasted_iota(jnp.int32, (256, 256), 1) // 64
    bd_b = jnp.where(rr == cc, 1.0, 0.0).astype(bf16)
    ml = mlf_ref[0, 0] + mlb_ref[0, 0]
    ml = ml * lax.rsqrt(_head_ms(ml, bd_b) + EPS) * mlw_ref[...]
    ml = _sigmoid(mlo_ref[0].astype(f32)) * ml
    hg = hgf_ref[0, 0] + hgb_ref[0, 0]
    hg = hg * lax.rsqrt(_head_ms(hg, bd_b) + EPS) * hgw_ref[...]
    gg = hgg_ref[0].astype(f32)
    hg = hg * (gg * _sigmoid(gg))
    mix = (_dot(ml.astype(bf16), w_ref[0:256, :]) + _dot(hg.astype(bf16), w_ref[256:512, :])
           + _dot(da_ref[0], w_ref[512:1024, :]))
    ms = jnp.mean(mix * mix, axis=-1, keepdims=True)
    y = mix * lax.rsqrt(ms + EPS) * nw_ref[...]
    o_ref[0] = x_ref[0] + mod_ref[0][:, 2 * d:3 * d] * y


def _outproj(x, mod, mod_row0, per_batch, ml_out, ml, hg_out, hg, da_out, mlw, hgw, w, nw, tm, name):
    bsz, l, d = x.shape
    nt = l // tm
    mod_map = (lambda b, i: (mod_row0 + b, 0, 0)) if per_batch else (lambda b, i: (mod_row0, 0, 0))
    return pl.pallas_call(
        _outproj_kernel,
        grid=(bsz, nt),
        in_specs=[
            pl.BlockSpec((1, tm, d), lambda b, i: (b, i, 0)),
            pl.BlockSpec((1, 1, mod.shape[-1]), mod_map),
            pl.BlockSpec((1, 1, tm, 256), lambda b, i: (0, b, i, 0)),
            pl.BlockSpec((1, 1, tm, 256), lambda b, i: (1, b, i, 0)),
            pl.BlockSpec((1, tm, 256), lambda b, i: (b, i, 3)),
            pl.BlockSpec((1, 1, tm, 256), lambda b, i: (0, b, i, 0)),
            pl.BlockSpec((1, 1, tm, 256), lambda b, i: (1, b, i, 0)),
            pl.BlockSpec((1, tm, 256), lambda b, i: (b, i, 2)),
            pl.BlockSpec((1, tm, 512), lambda b, i: (b, i, 0)),
            pl.BlockSpec((1, 256), lambda b, i: (0, 0)),
            pl.BlockSpec((1, 256), lambda b, i: (0, 0)),
            pl.BlockSpec((d, d), lambda b, i: (0, 0)),
            pl.BlockSpec((1, d), lambda b, i: (0, 0)),
        ],
        out_specs=pl.BlockSpec((1, tm, d), lambda b, i: (b, i, 0)),
        out_shape=jax.ShapeDtypeStruct((bsz, l, d), f32),
        compiler_params=_cparams(("parallel", "parallel")),
        name=name,
    )(x, mod, ml_out, ml_out, ml, hg_out, hg_out, hg, da_out, mlw, hgw, w, nw)


def _ffn_kernel(*refs, moe):
    if moe:
        x_ref, mod_ref, nw1_ref, rw_ref, rb_ref, wg_ref, wu_ref, wd_ref, nw2_ref, o_ref, h_scr, acc_scr, gate_scr = refs
    else:
        x_ref, mod_ref, nw1_ref, wg_ref, wu_ref, wd_ref, nw2_ref, o_ref, h_scr, acc_scr = refs
    e = pl.program_id(2)
    j = pl.program_id(3)
    ne = pl.num_programs(2)
    nj = pl.num_programs(3)
    d = x_ref.shape[-1]
    tm = x_ref.shape[1]

    @pl.when((e == 0) & (j == 0))
    def _():
        x = x_ref[0]
        ms = jnp.mean(x * x, axis=-1, keepdims=True)
        h = x * lax.rsqrt(ms + EPS) * nw1_ref[...]
        mod = mod_ref[0]
        h = h * (1.0 + mod[:, 4 * d:5 * d]) + mod[:, 3 * d:4 * d]
        h_scr[...] = h.astype(bf16)
        acc_scr[...] = jnp.zeros_like(acc_scr)
        if moe:
            hh, hl = _split_hi_lo(h)
            rw = rw_ref[...]
            wh, wl = _split_hi_lo(rw)
            logits = _dot(hh, wh) + _dot(hl, wh) + _dot(hh, wl) + rb_ref[...]
            lane = lax.broadcasted_iota(jnp.int32, (tm, LANES), 1)
            real = lane < N_EXPERTS
            lg = jnp.where(real, logits, MASK_NEG)
            m1 = jnp.max(lg, axis=-1, keepdims=True)
            i1 = jnp.min(jnp.where(lg == m1, lane, LANES), axis=-1, keepdims=True)
            lg2 = jnp.where(lane == i1, MASK_NEG, lg)
            m2 = jnp.max(lg2, axis=-1, keepdims=True)
            i2 = jnp.min(jnp.where(lg2 == m2, lane, LANES), axis=-1, keepdims=True)
            e2 = jnp.exp(m2 - m1)
            g1 = 1.0 / (1.0 + e2)
            g2 = e2 / (1.0 + e2)
            gate_scr[...] = jnp.where(lane == i1, g1, 0.0) + jnp.where(lane == i2, g2, 0.0)

    hb = h_scr[...]
    a = _dot(hb, wg_ref[0])
    u = _dot(hb, wu_ref[0])
    act = (a * _sigmoid(a)) * u
    if moe:
        lane = lax.broadcasted_iota(jnp.int32, (tm, LANES), 1)
        ge = jnp.sum(jnp.where(lane == e, gate_scr[...], 0.0), axis=-1, keepdims=True)
        act = act * ge
    acc_scr[...] += _dot(act.astype(bf16), wd_ref[0])

    @pl.when((e == ne - 1) & (j == nj - 1))
    def _():
        y = acc_scr[...]
        ms = jnp.mean(y * y, axis=-1, keepdims=True)
        y = y * lax.rsqrt(ms + EPS) * nw2_ref[...]
        o_ref[0] = x_ref[0] + mod_ref[0][:, 5 * d:6 * d] * y


def _ffn(x, mod, mod_row0, per_batch, nw1, wg, wu, wd, nw2, tm, tf, router=None, name="ffn"):
    bsz, l, d = x.shape
    ne, _, ff = wg.shape
    nt = l // tm
    nf = ff // tf
    moe = router is not None
    mod_map = (lambda b, i, e, j: (mod_row0 + b, 0, 0)) if per_batch else (lambda b, i, e, j: (mod_row0, 0, 0))
    in_specs = [
        pl.BlockSpec((1, tm, d), lambda b, i, e, j: (b, i, 0)),
        pl.BlockSpec((1, 1, mod.shape[-1]), mod_map),
        pl.BlockSpec((1, d), lambda b, i, e, j: (0, 0)),
    ]
    args = [x, mod, nw1]
    scratch = [pltpu.VMEM((tm, d), bf16), pltpu.VMEM((tm, d), f32)]
    if moe:
        in_specs += [
            pl.BlockSpec((d, LANES), lambda b, i, e, j: (0, 0)),
            pl.BlockSpec((1, LANES), lambda b, i, e, j: (0, 0)),
        ]
        args += list(router)
        scratch += [pltpu.VMEM((tm, LANES), f32)]
    in_specs += [
        pl.BlockSpec((1, d, tf), lambda b, i, e, j: (e, 0, j)),
        pl.BlockSpec((1, d, tf), lambda b, i, e, j: (e, 0, j)),
        pl.BlockSpec((1, tf, d), lambda b, i, e, j: (e, j, 0)),
        pl.BlockSpec((1, d), lambda b, i, e, j: (0, 0)),
    ]
    args += [wg, wu, wd, nw2]
    return pl.pallas_call(
        functools.partial(_ffn_kernel, moe=moe),
        grid=(bsz, nt, ne, nf),
        in_specs=in_specs,
        out_specs=pl.BlockSpec((1, tm, d), lambda b, i, e, j: (b, i, 0)),
        out_shape=jax.ShapeDtypeStruct((bsz, l, d), f32),
        scratch_shapes=scratch,
        compiler_params=_cparams(("parallel", "parallel", "arbitrary", "arbitrary")),
        name=name,
    )(*args)


MOE_TM = 1024
MOE_TS = 512
META_I1, META_I2, META_R1, META_R2, META_G1, META_G2 = range(6)


def _router_kernel(x_ref, mod_ref, nw_ref, rw_ref, rb_ref, tri_ref, h_ref, meta_ref, cnt_ref, base_scr):
    d = x_ref.shape[-1]
    tm = x_ref.shape[1]

    @pl.when((pl.program_id(0) == 0) & (pl.program_id(1) == 0))
    def _():
        base_scr[...] = jnp.zeros_like(base_scr)

    x = x_ref[0]
    ms = jnp.mean(x * x, axis=-1, keepdims=True)
    h = x * lax.rsqrt(ms + EPS) * nw_ref[...]
    mod = mod_ref[0]
    h = h * (1.0 + mod[:, 4 * d:5 * d]) + mod[:, 3 * d:4 * d]
    h_ref[0] = h
    hh, hl = _split_hi_lo(h)
    wh, wl = _split_hi_lo(rw_ref[...])
    logits = _dot(hh, wh) + _dot(hl, wh) + _dot(hh, wl) + rb_ref[...]
    lane = lax.broadcasted_iota(jnp.int32, (tm, LANES), 1)
    lg = jnp.where(lane < N_EXPERTS, logits, MASK_NEG)
    m1 = jnp.max(lg, axis=-1, keepdims=True)
    i1 = jnp.min(jnp.where(lg == m1, lane, LANES), axis=-1, keepdims=True)
    lg2 = jnp.where(lane == i1, MASK_NEG, lg)
    m2 = jnp.max(lg2, axis=-1, keepdims=True)
    i2 = jnp.min(jnp.where(lg2 == m2, lane, LANES), axis=-1, keepdims=True)
    e2 = jnp.exp(m2 - m1)
    g1 = 1.0 / (1.0 + e2)
    g2 = e2 / (1.0 + e2)
    oh = jnp.where((lane == i1) | (lane == i2), 1.0, 0.0)
    pos = base_scr[...] + _dot(tri_ref[...], oh.astype(bf16))
    r1 = jnp.sum(jnp.where(lane == i1, pos, 0.0), axis=-1, keepdims=True)
    r2 = jnp.sum(jnp.where(lane == i2, pos, 0.0), axis=-1, keepdims=True)
    meta = jnp.zeros((tm, LANES), f32)
    for k, val in ((META_I1, i1.astype(f32)), (META_I2, i2.astype(f32)), (META_R1, r1), (META_R2, r2),
                   (META_G1, g1), (META_G2, g2)):
        meta = jnp.where(lane == k, val, meta)
    meta_ref[0] = meta
    new_base = pos[tm - 1:tm, :] + oh[tm - 1:tm, :]
    base_scr[...] = new_base
    cnt_ref[...] = jnp.broadcast_to(new_base, cnt_ref.shape)


def _moe_route(x, mod, mod_row0, nw1, rw, rb, tm):
    bsz, l, d = x.shape
    nt = l // tm
    r = jnp.arange(tm)
    tri = (r[:, None] > r[None, :]).astype(bf16)
    return pl.pallas_call(
        _router_kernel,
        grid=(bsz, nt),
        in_specs=[
            pl.BlockSpec((1, tm, d), lambda b, i: (b, i, 0)),
            pl.BlockSpec((1, 1, mod.shape[-1]), lambda b, i: (mod_row0 + b, 0, 0)),
            pl.BlockSpec((1, d), lambda b, i: (0, 0)),
            pl.BlockSpec((d, LANES), lambda b, i: (0, 0)),
            pl.BlockSpec((1, LANES), lambda b, i: (0, 0)),
            pl.BlockSpec((tm, tm), lambda b, i: (0, 0)),
        ],
        out_specs=[
            pl.BlockSpec((1, tm, d), lambda b, i: (b, i, 0)),
            pl.BlockSpec((1, tm, LANES), lambda b, i: (b, i, 0)),
            pl.BlockSpec((8, LANES), lambda b, i: (0, 0)),
        ],
        out_shape=[
            jax.ShapeDtypeStruct((bsz, l, d), f32),
            jax.ShapeDtypeStruct((bsz, l, LANES), f32),
            jax.ShapeDtypeStruct((8, LANES), f32),
        ],
        scratch_shapes=[pltpu.VMEM((1, LANES), f32)],
        compiler_params=_cparams(("arbitrary", "arbitrary")),
        name="moe_route",
    )(x, mod, nw1, rw, rb, tri)


def _row_copy(src_ref, src_row, dst_ref, dst_row, sem):
    return pltpu.make_async_copy(src_ref.at[pl.ds(src_row, 1)], dst_ref.at[pl.ds(dst_row, 1)], sem)


def _scatter_kernel(dest_ref, h_ref, xs_in_ref, xs_ref, sem):
    del xs_in_ref
    i = pl.program_id(0)
    ts = h_ref.shape[0]
    base = i * (2 * ts)

    def issue(r, carry):
        _row_copy(h_ref, r, xs_ref, dest_ref[base + r], sem).start()
        _row_copy(h_ref, r, xs_ref, dest_ref[base + ts + r], sem).start()
        return carry

    lax.fori_loop(0, ts, issue, 0, unroll=8)

    def drain(r, carry):
        _row_copy(h_ref, 0, xs_ref, 0, sem).wait()
        _row_copy(h_ref, 0, xs_ref, 0, sem).wait()
        return carry

    lax.fori_loop(0, ts, drain, 0, unroll=8)


def _moe_scatter(h2, dest, n_rows, ts):
    t, d = h2.shape
    xs0 = jnp.zeros((n_rows, d), f32)
    return pl.pallas_call(
        _scatter_kernel,
        grid_spec=pltpu.PrefetchScalarGridSpec(
            num_scalar_prefetch=1,
            grid=(t // ts,),
            in_specs=[
                pl.BlockSpec((ts, d), lambda i, dest: (i, 0)),
                pl.BlockSpec(memory_space=pl.ANY),
            ],
            out_specs=pl.BlockSpec(memory_space=pl.ANY),
            scratch_shapes=[pltpu.SemaphoreType.DMA],
        ),
        out_shape=jax.ShapeDtypeStruct((n_rows, d), f32),
        input_output_aliases={2: 0},
        compiler_params=_cparams(("arbitrary",)),
        name="moe_scatter",
    )(dest, h2, xs0)


def _expert_kernel(te_ref, tv_ref, x_ref, wg_ref, wu_ref, wd_ref, o_ref, acc_scr):
    del te_ref
    j = pl.program_id(0)
    f = pl.program_id(1)
    nf = pl.num_programs(1)
    valid = tv_ref[j] == 1

    @pl.when(valid)
    def _():
        xb = x_ref[...].astype(bf16)
        a = _dot(xb, wg_ref[0])
        u = _dot(xb, wu_ref[0])
        part = _dot(((a * _sigmoid(a)) * u).astype(bf16), wd_ref[0])

        @pl.when(f == 0)
        def _():
            acc_scr[...] = part

        @pl.when(f > 0)
        def _():
            acc_scr[...] += part

        @pl.when(f == nf - 1)
        def _():
            o_ref[...] = acc_scr[...]

    @pl.when(jnp.logical_not(valid))
    def _():
        o_ref[...] = jnp.zeros_like(o_ref)


def _moe_experts(xs, te, tv, wg, wu, wd, tm, tf):
    n_rows, d = xs.shape
    ff = wg.shape[-1]
    nf = ff // tf
    n_tiles = te.shape[0]

    def fsel(j, f, tv):
        return jnp.where(tv[j] == 1, f, nf - 1)

    return pl.pallas_call(
        _expert_kernel,
        grid_spec=pltpu.PrefetchScalarGridSpec(
            num_scalar_prefetch=2,
            grid=(n_tiles, nf),
            in_specs=[
                pl.BlockSpec((tm, d), lambda j, f, te, tv: (j, 0)),
                pl.BlockSpec((1, d, tf), lambda j, f, te, tv: (te[j], 0, fsel(j, f, tv))),
                pl.BlockSpec((1, d, tf), lambda j, f, te, tv: (te[j], 0, fsel(j, f, tv))),
                pl.BlockSpec((1, tf, d), lambda j, f, te, tv: (te[j], fsel(j, f, tv), 0)),
            ],
            out_specs=pl.BlockSpec((tm, d), lambda j, f, te, tv: (j, 0)),
            scratch_shapes=[pltpu.VMEM((tm, d), f32)],
        ),
        out_shape=jax.ShapeDtypeStruct((n_rows, d), f32),
        compiler_params=_cparams(("arbitrary", "arbitrary")),
        name="moe_experts",
    )(te, tv, xs, wg, wu, wd)


def _combine_kernel(dest_ref, x_ref, mod_ref, meta_ref, nw_ref, ys_ref, o_ref, y1_scr, y2_scr, sem):
    i = pl.program_id(0)
    ts = x_ref.shape[0]
    d = x_ref.shape[1]
    base = i * (2 * ts)

    def issue(r, carry):
        _row_copy(ys_ref, dest_ref[base + r], y1_scr, r, sem).start()
        _row_copy(ys_ref, dest_ref[base + ts + r], y2_scr, r, sem).start()
        return carry

    lax.fori_loop(0, ts, issue, 0, unroll=8)

    def drain(r, carry):
        _row_copy(ys_ref, 0, y1_scr, 0, sem).wait()
        _row_copy(ys_ref, 0, y2_scr, 0, sem).wait()
        return carry

    lax.fori_loop(0, ts, drain, 0, unroll=8)

    meta = meta_ref[...]
    y = meta[:, META_G1:META_G1 + 1] * y1_scr[...] + meta[:, META_G2:META_G2 + 1] * y2_scr[...]
    ms = jnp.mean(y * y, axis=-1, keepdims=True)
    y = y * lax.rsqrt(ms + EPS) * nw_ref[...]
    o_ref[...] = x_ref[...] + mod_ref[0][:, 5 * d:6 * d] * y


def _moe_combine(x2, mod, mod_row0, tokens_per_batch, meta2, nw2, ys, dest, ts):
    t, d = x2.shape
    per_b = tokens_per_batch // ts
    return pl.pallas_call(
        _combine_kernel,
        grid_spec=pltpu.PrefetchScalarGridSpec(
            num_scalar_prefetch=1,
            grid=(t // ts,),
            in_specs=[
                pl.BlockSpec((ts, d), lambda i, dest: (i, 0)),
                pl.BlockSpec((1, 1, mod.shape[-1]), lambda i, dest: (mod_row0 + i // per_b, 0, 0)),
                pl.BlockSpec((ts, LANES), lambda i, dest: (i, 0)),
                pl.BlockSpec((1, d), lambda i, dest: (0, 0)),
                pl.BlockSpec(memory_space=pl.ANY),
            ],
            out_specs=pl.BlockSpec((ts, d), lambda i, dest: (i, 0)),
            scratch_shapes=[pltpu.VMEM((ts, d), f32), pltpu.VMEM((ts, d), f32), pltpu.SemaphoreType.DMA],
        ),
        out_shape=jax.ShapeDtypeStruct((t, d), f32),
        compiler_params=_cparams(("arbitrary",)),
        name="moe_combine",
    )(dest, x2, mod, meta2, nw2, ys)


def _moe_sparse(x, mod, mod_row0, nw1, rw, rb, wg, wu, wd, nw2):
    bsz, l, d = x.shape
    t = bsz * l
    ne = wg.shape[0]
    tm_r = _pick_tile(l, (1024, 512, 256, 128))
    tm = min(MOE_TM, t)
    ts = _pick_tile(l, (MOE_TS, 256, 128))
    tf = _pick_tile(wg.shape[-1], (896, 512, 256, 128))

    h, meta, cnt = _moe_route(x, mod, mod_row0, nw1, rw, rb, tm_r)
    meta2 = meta.reshape(t, LANES)
    counts = cnt[0, :ne].astype(jnp.int32)
    tiles_e = (counts + tm - 1) // tm
    ends = jnp.cumsum(tiles_e)
    offset = (ends - tiles_e) * tm
    n_tiles = (2 * t) // tm + ne
    i1 = meta2[:, META_I1].astype(jnp.int32)
    i2 = meta2[:, META_I2].astype(jnp.int32)
    dest1 = jnp.take(offset, i1) + meta2[:, META_R1].astype(jnp.int32)
    dest2 = jnp.take(offset, i2) + meta2[:, META_R2].astype(jnp.int32)
    dest = jnp.stack([dest1.reshape(t // ts, ts), dest2.reshape(t // ts, ts)], axis=1).reshape(-1)
    jt = jnp.arange(n_tiles, dtype=jnp.int32)
    total = ends[ne - 1]
    tv = (jt < total).astype(jnp.int32)
    te_all = jnp.minimum(jnp.sum((jt[:, None] >= ends[None, :]).astype(jnp.int32), axis=1), ne - 1)
    te_last = jnp.take(te_all, jnp.maximum(total - 1, 0))
    te = jnp.where(tv == 1, te_all, te_last)
    n_rows = n_tiles * tm

    xs = _moe_scatter(h.reshape(t, d), dest, n_rows, ts)
    ys = _moe_experts(xs, te, tv, wg, wu, wd, tm, tf)
    out = _moe_combine(x.reshape(t, d), mod, mod_row0, l, meta2, nw2, ys, dest, ts)
    return out.reshape(bsz, l, d)


def _pick_tile(n, prefs):
    for t in prefs:
        if n % t == 0:
            return t
    return n


def kernel(x, c, ctx, c_ctx, w_mod, b_mod, norm_pre_mix, norm_post_mix, norm_pre_ffn, norm_post_ffn, w_in, w_out, ml_gate_b, ml_norm, hg_lb_logits, hg_norm, da_lambda, da_norm, ffn_w_gate, ffn_w_up, ffn_w_down, router_w, router_b, moe_w_gate, moe_w_up, moe_w_down):
    depth = w_mod.shape[0]
    bsz, l, d = x.shape
    lctx = ctx.shape[1]

    mod = _mod_vectors(c, c_ctx, w_mod, b_mod)
    rows = mod.shape[1]
    mod = mod.reshape(depth * rows, 1, 6 * d)

    lb_soft = jax.nn.softmax(hg_lb_logits.astype(f32), axis=0)
    lower_bounds = jnp.cumsum(lb_soft, axis=0) - lb_soft[0]

    tables = _rope_tables(l)
    ml_lc = min(ML_CHUNK, lctx)
    tri_ml_x = _tri_pair(min(ML_CHUNK, l), min(ML_CHUNK, l))
    tri_ml_c = _tri_pair(ml_lc, ml_lc)
    tri_hg_x = _tri_pair(min(HG_GROUP, l), HG_CHUNK)
    tri_hg_c = _tri_pair(min(HG_GROUP, lctx), HG_CHUNK)

    tm_x = _pick_tile(l, (512, 256, 128))
    tm_c = _pick_tile(lctx, (256, 128))
    tq = _pick_tile(l, (256, 128))

    for layer in range(depth):
        need_ctx = layer < depth - 1
        lam_init = 0.8 - 0.6 * math.exp(-0.3 * layer)
        row0 = layer * rows
        w_in_l = _prep_w_in(w_in[layer])
        w_out_l = w_out[layer].astype(bf16)
        nw_pre = norm_pre_mix[layer].reshape(1, d)

        pc = _inproj(ctx, mod, row0 + bsz, False, nw_pre, w_in_l, None, tm_c)
        px = _inproj(x, mod, row0, True, nw_pre, w_in_l, tables, tm_x)
        ml_c, mlg_c, hg_c, hgf_c, da_c = pc
        ml_x, mlg_x, hg_x, hgf_x, da_x = px

        gb = ml_gate_b[layer].astype(f32)
        z = jnp.zeros((LANES - 8,), f32)
        gate_b2 = jnp.stack([jnp.concatenate([gb[0:8], z]), jnp.concatenate([gb[8:16], z])]).reshape(2, 1, LANES)
        s0 = jnp.zeros((2, bsz, ML_H, ML_STATE_ROWS, LANES), f32)
        mlo_c, s_ml = _mlstm(ml_c, mlg_c, gate_b2, tri_ml_c, s0)
        mlo_x, _ = _mlstm(ml_x, mlg_x, gate_b2, tri_ml_x, s_ml)

        lb = lower_bounds[layer]
        lbc = jnp.zeros((8, HG_W), f32)
        lbc = lbc.at[0].set(jnp.log(jnp.maximum(lb, LB_FLOOR))).at[1].set(jnp.log1p(-lb)).at[2].set(1.0 - lb)
        h0 = jnp.zeros((2, bsz, HG_W, HG_W), f32)
        hgo_c, s_hg = _hgrn2(hg_c, hgf_c, lbc, tri_hg_c, h0)
        hgo_x, _ = _hgrn2(hg_x, hgf_x, lbc, tri_hg_x, s_hg)

        lam_p = da_lambda[layer].astype(f32)
        da_nw = da_norm[layer].reshape(1, DA_V)
        dao_x = _attn(da_x, da_c, lam_p, da_nw, lam_init, tq)

        mlw = ml_norm[layer].reshape(1, 256)
        hgw = hg_norm[layer].reshape(1, 256)
        nw_post = norm_post_mix[layer].reshape(1, d)
        x = _outproj(x, mod, row0, True, mlo_x, ml_x, hgo_x, hg_x, dao_x, mlw, hgw, w_out_l, nw_post, tm_x, "outproj_x")
        if need_ctx:
            dao_c = _attn(da_c, da_c, lam_p, da_nw, lam_init, lctx)
            ctx = _outproj(ctx, mod, row0 + bsz, False, mlo_c, ml_c, hgo_c, hg_c, dao_c, mlw, hgw, w_out_l, nw_post, tm_c, "outproj_c")

        i = layer // 2
        nw1 = norm_pre_ffn[layer].reshape(1, d)
        nw2 = norm_post_ffn[layer].reshape(1, d)
        if layer % 2 == 0:
            wg = ffn_w_gate[i].astype(bf16)[None]
            wu = ffn_w_up[i].astype(bf16)[None]
            wd = ffn_w_down[i].astype(bf16)[None]
            router = None
        else:
            wg = moe_w_gate[i].astype(bf16)
            wu = moe_w_up[i].astype(bf16)
            wd = moe_w_down[i].astype(bf16)
            rw = jnp.zeros((d, LANES), f32).at[:, :N_EXPERTS].set(router_w[i])
            rb = jnp.zeros((1, LANES), f32).at[0, :N_EXPERTS].set(router_b[i])
            router = (rw, rb)
        ff = wg.shape[-1]
        tf = _pick_tile(ff, (896, 1408, 512, 256))
        tm_f = _pick_tile(l, (1024, 512, 256, 128) if tf <= 896 else (512, 256, 128))
        if router is None:
            x = _ffn(x, mod, row0, True, nw1, wg, wu, wd, nw2, tm_f, tf, None, name="ffn_x%d" % layer)
        else:
            x = _moe_sparse(x, mod, row0, nw1, rw, rb, wg, wu, wd, nw2)
        if need_ctx:
            ctx = _ffn(ctx, mod, row0 + bsz, False, nw1, wg, wu, wd, nw2, tm_c, tf, router, name="ffn_c%d" % layer)
    return x
```

```python
import functools
import math

import numpy as np
import jax
import jax.numpy as jnp
from jax import lax
from jax.experimental import pallas as pl
from jax.experimental.pallas import tpu as pltpu

f32 = jnp.float32
bf16 = jnp.bfloat16

ML_H = 4
HG_H = 4
DA_H = 4
GRID_W = 64
ROPE_BASE = 10000.0
N_EXPERTS = 8
EPS = 1e-6
MASK_NEG = -1e30
LB_FLOOR = 1e-30

LANES = 128
ML_CHUNK = 256
HG_CHUNK = 16
HG_GROUP = 256
VMEM_LIMIT = 56 * 1024 * 1024


def _cparams(sem):
    return pltpu.CompilerParams(dimension_semantics=sem, vmem_limit_bytes=VMEM_LIMIT)


def _sigmoid(x):
    return 1.0 / (1.0 + jnp.exp(-x))


def _log_sigmoid(x):
    return jnp.minimum(x, 0.0) - jnp.log(1.0 + jnp.exp(-jnp.abs(x)))


def _split_hi_lo(a):
    hi = a.astype(bf16)
    lo = (a - hi.astype(f32)).astype(bf16)
    return hi, lo


def _dot(a, b):
    return jnp.dot(a, b, preferred_element_type=f32)


def _dot_nt(a, b):
    return lax.dot_general(a, b, (((1,), (1,)), ((), ())), preferred_element_type=f32)


def _dot_tn(a, b):
    return lax.dot_general(a, b, (((0,), (0,)), ((), ())), preferred_element_type=f32)


def _dot2(a_f32, b_bf16):
    hi, lo = _split_hi_lo(a_f32)
    return _dot(hi, b_bf16) + _dot(lo, b_bf16)


def _mod_kernel(s_ref, w_ref, b_ref, o_ref):
    s = s_ref[...]
    s = s * _sigmoid(s)
    o_ref[0] = _dot(s.astype(bf16), w_ref[0].astype(bf16)) + b_ref[0]


def _mod_vectors(c, c_ctx, w_mod, b_mod):
    depth, d, n = w_mod.shape
    bsz = c.shape[0]
    rows = 8
    s = jnp.zeros((rows, d), f32).at[:bsz].set(c).at[bsz].set(c_ctx)
    tn = n // 4
    out = pl.pallas_call(
        _mod_kernel,
        grid=(depth, n // tn),
        in_specs=[
            pl.BlockSpec((rows, d), lambda l, j: (0, 0)),
            pl.BlockSpec((1, d, tn), lambda l, j: (l, 0, j)),
            pl.BlockSpec((1, 1, tn), lambda l, j: (l, 0, j)),
        ],
        out_specs=pl.BlockSpec((1, rows, tn), lambda l, j: (l, 0, j)),
        out_shape=jax.ShapeDtypeStruct((depth, rows, n), f32),
        compiler_params=_cparams(("parallel", "parallel")),
        name="mod_vectors",
    )(s, w_mod, b_mod.reshape(depth, 1, n))
    return out


P_ML = (0, 1024)
P_MLG = (1024, 1280)
P_HG = (1280, 2048)
P_HGF = (2048, 2560)
P_DA = (2560, 4096)
P_TOTAL = 4096


def _prep_w_in(w):
    d = w.shape[0]
    ml = w[:, 0:1024]
    g = w[:, 1024:1040]
    z = jnp.zeros((d, LANES - 8), w.dtype)
    gates = jnp.concatenate([g[:, 0:8], z, g[:, 8:16], z], axis=1)
    hg_q = w[:, 1040:1296]
    hg_f = w[:, 1296:1808]
    hg_i = w[:, 1808:2064]
    hg_g = w[:, 2064:2320]
    da = w[:, 2320:3856]
    return jnp.concatenate([ml, gates, hg_q, hg_i, hg_g, hg_f, da], axis=1).astype(bf16)


def _inproj_kernel(*refs, rope):
    if rope:
        x_ref, mod_ref, nw_ref, w_ref, cos_ref, sin_ref = refs[:6]
        outs = refs[6:]
    else:
        x_ref, mod_ref, nw_ref, w_ref = refs[:4]
        outs = refs[4:]
    ml_ref, mlg_ref, hg_ref, hgf_ref, da_ref = outs
    d = x_ref.shape[-1]
    x = x_ref[0]
    ms = jnp.mean(x * x, axis=-1, keepdims=True)
    h = x * lax.rsqrt(ms + EPS) * nw_ref[...]
    mod = mod_ref[0]
    h = h * (1.0 + mod[:, d:2 * d]) + mod[:, 0:d]
    hb = h.astype(bf16)

    def proj(lo, hi):
        return _dot(hb, w_ref[:, lo:hi])

    ml_ref[0] = proj(*P_ML).astype(bf16)
    mlg_ref[0] = proj(*P_MLG)
    pq = proj(P_HG[0], P_HG[0] + 256)
    hg_ref[0, :, 0:256] = (pq * _sigmoid(pq)).astype(bf16)
    hg_ref[0, :, 256:768] = proj(P_HG[0] + 256, P_HG[1]).astype(bf16)
    hgf_ref[0] = proj(*P_HGF)
    q = proj(P_DA[0], P_DA[0] + 512)
    k = proj(P_DA[0] + 512, P_DA[0] + 1024)
    if rope:
        cos = jnp.concatenate([cos_ref[...]] * 4, axis=1)
        sin = jnp.concatenate([sin_ref[...]] * 4, axis=1)
        lane = lax.broadcasted_iota(jnp.int32, (1, 512), 1)
        first = (lane % 32) < 16

        def rot(t):
            sw = jnp.where(first, pltpu.roll(t, 512 - 16, 1), pltpu.roll(t, 16, 1))
            return t * cos + sw * sin

        q = rot(q)
        k = rot(k)
    da_ref[0, :, 0:512] = (q * 0.125).astype(bf16)
    da_ref[0, :, 512:1024] = k.astype(bf16)
    da_ref[0, :, 1024:1536] = proj(P_DA[0] + 1024, P_DA[1]).astype(bf16)


def _inproj(x, mod, mod_row0, per_batch, nw, w, tables, tm):
    bsz, l, d = x.shape
    rope = tables is not None
    nt = l // tm
    mod_map = (lambda b, i: (mod_row0 + b, 0, 0)) if per_batch else (lambda b, i: (mod_row0, 0, 0))
    in_specs = [
        pl.BlockSpec((1, tm, d), lambda b, i: (b, i, 0)),
        pl.BlockSpec((1, 1, mod.shape[-1]), mod_map),
        pl.BlockSpec((1, d), lambda b, i: (0, 0)),
        pl.BlockSpec((d, P_TOTAL), lambda b, i: (0, 0)),
    ]
    args = [x, mod, nw, w]
    if rope:
        in_specs += [pl.BlockSpec((tm, LANES), lambda b, i: (i, 0))] * 2
        args += list(tables)
    widths = (1024, 256, 768, 512, 1536)
    dtypes = (bf16, f32, bf16, f32, bf16)
    out_specs = [pl.BlockSpec((1, tm, wd), lambda b, i: (b, i, 0)) for wd in widths]
    out_shape = [jax.ShapeDtypeStruct((bsz, l, wd), dt) for wd, dt in zip(widths, dtypes)]
    return pl.pallas_call(
        functools.partial(_inproj_kernel, rope=rope),
        grid=(bsz, nt),
        in_specs=in_specs,
        out_specs=out_specs,
        out_shape=out_shape,
        compiler_params=_cparams(("parallel", "parallel")),
        name="inproj_x" if rope else "inproj_c",
    )(*args)


def _rope_tables(l):
    rows = l // GRID_W
    row = jnp.repeat(jnp.arange(rows), GRID_W).astype(f32)
    col = jnp.tile(jnp.arange(GRID_W), rows).astype(f32)
    n_freq = 16
    inv = ROPE_BASE ** (-jnp.arange(n_freq, dtype=f32) / n_freq)
    lane = jnp.arange(LANES)
    freq = inv[lane % n_freq]
    pos = jnp.where(((lane % 64) // 32 == 0)[None, :], row[:, None], col[:, None])
    ang = pos * freq[None, :]
    sign = jnp.where((lane % 32) < 16, -1.0, 1.0)[None, :]
    return jnp.cos(ang), jnp.sin(ang) * sign


ML_D = 64


def _mlstm_kernel(q_ref, k_ref, v_ref, g_ref, gb_ref, tri_ref, sel_ref, cn0_ref, m0_ref,
                  o_ref, cnf_ref, mf_ref, cn_scr, m_scr):
    d = pl.program_id(0)
    j = pl.program_id(2)
    nj = pl.num_programs(2)
    lc = q_ref.shape[1]
    nrep = lc // LANES

    @pl.when(j == 0)
    def _():
        cn_scr[...] = cn0_ref[0, 0]
        m_scr[...] = m0_ref[0, 0]

    g = g_ref[0] + gb_ref[0]
    lf = _log_sigmoid(g)
    hi, lo = _split_hi_lo(lf)
    bc = _dot(tri_ref[0], hi) + _dot(tri_ref[0], lo)
    z = pltpu.roll(g, 4, 1) - bc
    zt = z.T
    sel = sel_ref[...]
    yrep = _dot2(bc, sel)
    zrep = _dot2(z, sel)
    fwd = d == 0
    r = lax.broadcasted_iota(jnp.int32, (lc, lc), 0)
    c = lax.broadcasted_iota(jnp.int32, (lc, lc), 1)
    mask = (r - c) * (1 - 2 * d) >= 0
    lane = lax.broadcasted_iota(jnp.int32, (lc, LANES), 1)
    ones_b = jnp.ones((lc, LANES), bf16)

    def rep(x, n):
        return jnp.concatenate([x] * n, axis=1)

    for p in range(ML_H // 2):
        qp = q_ref[0, :, p * LANES:(p + 1) * LANES]
        kp = k_ref[0, :, p * LANES:(p + 1) * LANES] * jnp.asarray(0.125, bf16)
        vp = v_ref[0, :, p * LANES:(p + 1) * LANES]
        outs = []
        for hh in range(2):
            h = 2 * p + hh
            half = (lane < ML_D) if hh == 0 else (lane >= ML_D)
            b_rep = yrep[:, h * LANES:(h + 1) * LANES]
            ab_rep = zrep[:, h * LANES:(h + 1) * LANES]
            blast = jnp.where(fwd, b_rep[lc - 1:lc, :], b_rep[0:1, :])
            m0 = m_scr[h, 0:1, :]
            z_row = zt[4 + h:5 + h, :]

            dm = jnp.where(mask, rep(b_rep, nrep) + z_row, MASK_NEG)
            inter = b_rep + m0
            m = jnp.maximum(inter, jnp.max(dm, axis=-1, keepdims=True))
            w_intra = jnp.exp(dm - rep(m, nrep))
            w_inter = jnp.exp(inter - m)

            qm = jnp.where(half, qp, jnp.zeros_like(qp))
            vo = jnp.concatenate([jnp.where(half, vp, jnp.zeros_like(vp)), ones_b], axis=1)
            s = _dot_nt(qm, kp) * w_intra
            cn = cn_scr[h]
            nd = _dot(s.astype(bf16), vo) + rep(w_inter, 2) * _dot(qm, cn.astype(bf16))
            den = nd[:, LANES:2 * LANES]
            outs.append(nd[:, 0:LANES] / jnp.maximum(jnp.abs(den), jnp.exp(-m)))

            a_rep = blast + ab_rep
            m_loc = jnp.max(a_rep, axis=0, keepdims=True)
            w = jnp.exp(a_rep - m_loc)
            kw = jnp.where(half, kp.astype(f32) * w, 0.0).astype(bf16)
            dcn = _dot_tn(kw, vo)
            m_new = jnp.maximum(blast + m0, m_loc)
            s_old = jnp.exp(blast + m0 - m_new)
            s_new = jnp.exp(m_loc - m_new)
            cn_scr[h] = rep(s_old, 2) * cn + rep(s_new, 2) * dcn
            m_scr[h] = jnp.broadcast_to(m_new, (8, LANES))
        o_ref[0, 0, :, p * LANES:(p + 1) * LANES] = outs[0] + outs[1]

    @pl.when(j == nj - 1)
    def _():
        cnf_ref[0, 0] = cn_scr[...]
        mf_ref[0, 0] = m_scr[...]


def _mlstm_sel():
    sel = np.zeros((LANES, ML_H * LANES), np.float32)
    for h in range(ML_H):
        sel[4 + h, h * LANES:(h + 1) * LANES] = 1.0
    return jnp.asarray(sel, dtype=bf16)


def _mlstm(ml, mlg, gate_b2, tri, sel, state):
    bsz, l, _ = ml.shape
    lc = min(ML_CHUNK, l)
    nj = l // lc
    cn0, m0 = state

    def tmap(col):
        return lambda d, b, j: (b, jnp.where(d == 0, j, nj - 1 - j), col)

    cn_spec = pl.BlockSpec((1, 1, ML_H, LANES, 2 * LANES), lambda d, b, j: (d, b, 0, 0, 0))
    m_spec = pl.BlockSpec((1, 1, ML_H, 8, LANES), lambda d, b, j: (d, b, 0, 0, 0))
    out, cnf, mf = pl.pallas_call(
        _mlstm_kernel,
        grid=(2, bsz, nj),
        in_specs=[
            pl.BlockSpec((1, lc, 256), tmap(0)),
            pl.BlockSpec((1, lc, 256), tmap(1)),
            pl.BlockSpec((1, lc, 256), tmap(2)),
            pl.BlockSpec((1, lc, LANES), lambda d, b, j: (b, jnp.where(d == 0, j, nj - 1 - j), d)),
            pl.BlockSpec((1, 1, LANES), lambda d, b, j: (d, 0, 0)),
            pl.BlockSpec((1, lc, lc), lambda d, b, j: (d, 0, 0)),
            pl.BlockSpec((LANES, ML_H * LANES), lambda d, b, j: (0, 0)),
            cn_spec,
            m_spec,
        ],
        out_specs=[
            pl.BlockSpec((1, 1, lc, 256), lambda d, b, j: (d, b, jnp.where(d == 0, j, nj - 1 - j), 0)),
            cn_spec,
            m_spec,
        ],
        out_shape=[
            jax.ShapeDtypeStruct((2, bsz, l, 256), f32),
            jax.ShapeDtypeStruct((2, bsz, ML_H, LANES, 2 * LANES), f32),
            jax.ShapeDtypeStruct((2, bsz, ML_H, 8, LANES), f32),
        ],
        scratch_shapes=[pltpu.VMEM((ML_H, LANES, 2 * LANES), f32), pltpu.VMEM((ML_H, 8, LANES), f32)],
        compiler_params=_cparams(("parallel", "parallel", "arbitrary")),
        name="mlstm",
    )(ml, ml, ml, mlg, gate_b2, tri, sel, cn0, m0)
    return out, (cnf, mf)


def _tri_pair(n, block):
    r = np.arange(n)[:, None]
    c = np.arange(n)[None, :]
    same = (r // block) == (c // block)
    lower = (same & (r >= c)).astype(np.float32)
    upper = (same & (r <= c)).astype(np.float32)
    return jnp.asarray(np.stack([lower, upper]), dtype=bf16)


HG_W = 256


def _hgrn2_kernel(q_ref, v_ref, f_ref, lbc_ref, tri_ref, bias_ref, s0_ref, o_ref, sf_ref, st_ref, p_scr):
    d = pl.program_id(0)
    j = pl.program_id(2)
    nj = pl.num_programs(2)
    lg = q_ref.shape[1]
    nck = lg // HG_CHUNK
    hw = HG_W // 2

    @pl.when(j == 0)
    def _():
        st_ref[...] = s0_ref[0, 0]

    fr = f_ref[0]
    log_lb = lbc_ref[0:1, :]
    log_1m_lb = lbc_ref[1:2, :]
    one_m_lb = lbc_ref[2:3, :]
    t = jnp.exp(-jnp.abs(fr))
    b = log_1m_lb + (jnp.minimum(fr, 0.0) - jnp.log(1.0 + t))
    kk = one_m_lb * (jnp.where(fr >= 0.0, t, 1.0) / (1.0 + t))
    log_f = jnp.maximum(log_lb, b) + jnp.log(1.0 + jnp.exp(-jnp.abs(log_lb - b)))
    hi, lo = _split_hi_lo(log_f)
    g = _dot(tri_ref[0], hi) + _dot(tri_ref[0], lo)
    q = q_ref[0].astype(f32)
    qe = (q * jnp.exp(g)).astype(bf16)

    rr = lax.broadcasted_iota(jnp.int32, (hw, hw), 0) // 64
    cc = lax.broadcasted_iota(jnp.int32, (hw, hw), 1) // 64
    same_head = rr == cc
    bd_b = jnp.where(same_head, 1.0, 0.0).astype(bf16)

    sub = 8
    nsub = HG_CHUNK // sub

    def run(order, last_row, fwd):
        def block_kind(s, tb):
            lo, hi = tb * sub, tb * sub + sub - 1
            if fwd:
                return "all" if lo >= s else ("none" if hi < s else "some")
            return "all" if hi <= s else ("none" if lo > s else "some")

        for c in order:
            r0 = c * HG_CHUNK
            gc = g[r0:r0 + HG_CHUNK]
            kc = kk[r0:r0 + HG_CHUNK]
            qc = q[r0:r0 + HG_CHUNK]
            vcb = v_ref[0, r0:r0 + HG_CHUNK, :]
            vc = vcb.astype(f32)
            for s in range(HG_CHUNK):
                blocks = []
                for tb in range(nsub):
                    kind = block_kind(s, tb)
                    if kind == "none":
                        blocks.append(jnp.zeros((sub, HG_W), f32))
                        continue
                    diff = gc[tb * sub:(tb + 1) * sub] - gc[s:s + 1, :]
                    if kind == "some":
                        diff = diff + bias_ref[0, s * HG_CHUNK + tb * sub:s * HG_CHUNK + (tb + 1) * sub, :]
                    blocks.append(qc[tb * sub:(tb + 1) * sub] * kc[s:s + 1, :] * jnp.exp(diff))
                p_scr[c, s * HG_CHUNK:(s + 1) * HG_CHUNK, :] = jnp.concatenate(blocks, axis=0).astype(bf16)
            abc = jnp.concatenate([_dot(p_scr[c, :, 0:hw], bd_b), _dot(p_scr[c, :, hw:HG_W], bd_b)], axis=1)
            o_blocks = []
            for tb in range(nsub):
                acc = None
                for s in range(HG_CHUNK):
                    if block_kind(s, tb) == "none":
                        continue
                    term = abc[s * HG_CHUNK + tb * sub:s * HG_CHUNK + (tb + 1) * sub, :] * vc[s:s + 1, :]
                    acc = term if acc is None else acc + term
                o_blocks.append(acc)
            o = jnp.concatenate(o_blocks, axis=0)
            st = st_ref[...]
            stb = st.astype(bf16)
            qec = qe[r0:r0 + HG_CHUNK]
            o = o + jnp.concatenate([_dot_nt(qec[:, 0:hw], stb[0:hw]), _dot_nt(qec[:, hw:HG_W], stb[hw:HG_W])], axis=1)
            o_ref[0, 0, r0:r0 + HG_CHUNK, :] = o
            g_last = gc[last_row:last_row + 1, :]
            kt = (kc * jnp.exp(g_last - gc)).astype(bf16)
            dec = jnp.exp(g_last)
            d_lo = _dot_tn(vcb[:, 0:hw], kt[:, 0:hw])
            d_hi = _dot_tn(vcb[:, hw:HG_W], kt[:, hw:HG_W])
            st_ref[0:hw, :] = st[0:hw] * dec[:, 0:hw] + jnp.where(same_head, d_lo, 0.0)
            st_ref[hw:HG_W, :] = st[hw:HG_W] * dec[:, hw:HG_W] + jnp.where(same_head, d_hi, 0.0)

    @pl.when(d == 0)
    def _():
        run(range(nck), HG_CHUNK - 1, True)

    @pl.when(d == 1)
    def _():
        run(range(nck - 1, -1, -1), 0, False)

    @pl.when(j == nj - 1)
    def _():
        sf_ref[0, 0] = st_ref[...]


def _hgrn2(hg, hgf, lbc, tri, bias, s0):
    bsz, l, _ = hg.shape
    lg = min(HG_GROUP, l)
    nj = l // lg
    hw = HG_W // 2

    def tmap(col):
        return lambda d, b, j: (b, jnp.where(d == 0, j, nj - 1 - j), col)

    out, sf = pl.pallas_call(
        _hgrn2_kernel,
        grid=(2, bsz, nj),
        in_specs=[
            pl.BlockSpec((1, lg, HG_W), tmap(0)),
            pl.BlockSpec((1, lg, HG_W), tmap(1)),
            pl.BlockSpec((1, lg, HG_W), lambda d, b, j: (b, jnp.where(d == 0, j, nj - 1 - j), d)),
            pl.BlockSpec((8, HG_W), lambda d, b, j: (0, 0)),
            pl.BlockSpec((1, lg, lg), lambda d, b, j: (d, 0, 0)),
            pl.BlockSpec((1, HG_CHUNK * HG_CHUNK, HG_W), lambda d, b, j: (d, 0, 0)),
            pl.BlockSpec((1, 1, HG_W, hw), lambda d, b, j: (d, b, 0, 0)),
        ],
        out_specs=[
            pl.BlockSpec((1, 1, lg, HG_W), lambda d, b, j: (d, b, jnp.where(d == 0, j, nj - 1 - j), 0)),
            pl.BlockSpec((1, 1, HG_W, hw), lambda d, b, j: (d, b, 0, 0)),
        ],
        out_shape=[
            jax.ShapeDtypeStruct((2, bsz, l, HG_W), f32),
            jax.ShapeDtypeStruct((2, bsz, HG_W, hw), f32),
        ],
        scratch_shapes=[
            pltpu.VMEM((HG_W, hw), f32),
            pltpu.VMEM((lg // HG_CHUNK, HG_CHUNK * HG_CHUNK, HG_W), bf16),
        ],
        compiler_params=_cparams(("parallel", "parallel", "arbitrary")),
        name="hgrn2",
    )(hg, hg, hgf, lbc, tri, bias, s0)
    return out, sf


def _hgrn2_bias():
    idx = np.arange(HG_CHUNK * HG_CHUNK)
    s, t = idx // HG_CHUNK, idx % HG_CHUNK
    fwd = np.where(t >= s, 0.0, MASK_NEG).astype(np.float32)
    bwd = np.where(t <= s, 0.0, MASK_NEG).astype(np.float32)
    return jnp.asarray(np.broadcast_to(np.stack([fwd, bwd])[:, :, None], (2, HG_CHUNK * HG_CHUNK, HG_W)))


DA_V = 128


def _attn_kernel(*refs, has_x, lam_init, kg, nq):
    if has_x:
        q_ref, kx_ref, kc_ref, vx_ref, vc_ref, lp_ref, nw_ref, o_ref, s0_scr, s1_scr, m0_scr, m1_scr, va_scr = refs
        lx = kx_ref.shape[1]
    else:
        q_ref, kc_ref, vc_ref, lp_ref, nw_ref, o_ref, s0_scr, s1_scr, m0_scr, m1_scr, va_scr = refs
        lx = 0
    tq = q_ref.shape[1]
    lctx = kc_ref.shape[1]
    t = pl.program_id(0)

    @pl.when(t == 0)
    def _():
        s1_scr[...] = jnp.zeros(s1_scr.shape, f32)
        m1_scr[...] = jnp.zeros(m1_scr.shape, f32)

    @pl.when((t == 0) | ((t - 1) % nq == 0))
    def _():
        def ones_col(n):
            lane = lax.broadcasted_iota(jnp.int32, (n, DA_V), 1)
            return jnp.where(lane == 0, 1.0, 0.0).astype(bf16)

        if has_x:
            va_scr[0:lx, 0:DA_V] = vx_ref[0]
            va_scr[0:lx, DA_V:2 * DA_V] = ones_col(lx)
        va_scr[lx:lx + lctx, 0:DA_V] = vc_ref[0]
        va_scr[lx:lx + lctx, DA_V:2 * DA_V] = ones_col(lctx)

    groups = ([(kx_ref, g * kg, kg, g * kg) for g in range(lx // kg)] if has_x else []) + [(kc_ref, 0, lctx, lx)]

    def step(sa_scr, ma_scr, sb_scr, mb_scr):
        q = q_ref[0]
        lane = lax.broadcasted_iota(jnp.int32, (tq, DA_V), 1)
        zero = jnp.zeros_like(q)
        qq = jnp.concatenate([jnp.where(lane < 64, q, zero), jnp.where(lane >= 64, q, zero)], axis=0)
        m128 = None
        for ref, r0, n, c0 in groups:
            s = _dot_nt(qq, ref[0, r0:r0 + n, :])
            sa_scr[:, c0:c0 + n] = s
            for c in range(n // LANES):
                blk = s[:, c * LANES:(c + 1) * LANES]
                m128 = blk if m128 is None else jnp.maximum(m128, blk)
        ma_scr[...] = jnp.broadcast_to(jnp.max(m128, axis=-1, keepdims=True), m128.shape)
        m_prev = mb_scr[...]
        acc = None
        for ref, r0, n, c0 in groups:
            m_rep = jnp.concatenate([m_prev] * (n // LANES), axis=1)
            p = jnp.exp(sb_scr[:, c0:c0 + n] - m_rep).astype(bf16)
            part = _dot(p, va_scr[c0:c0 + n, :])
            acc = part if acc is None else acc + part
        l = acc[:, DA_V:DA_V + 1]
        acc = acc[:, 0:DA_V]
        lp = lp_ref[...]
        lam = (jnp.exp(jnp.sum(lp[0:1] * lp[1:2], axis=-1, keepdims=True))
               - jnp.exp(jnp.sum(lp[2:3] * lp[3:4], axis=-1, keepdims=True)) + lam_init)
        o = acc[0:tq] / l[0:tq] - lam * (acc[tq:2 * tq] / l[tq:2 * tq])
        ms = jnp.mean(o * o, axis=-1, keepdims=True)
        o_ref[0] = (o * lax.rsqrt(ms + EPS) * nw_ref[...] * (1.0 - lam_init)).astype(bf16)

    @pl.when(t % 2 == 0)
    def _():
        step(s0_scr, m0_scr, s1_scr, m1_scr)

    @pl.when(t % 2 == 1)
    def _():
        step(s1_scr, m1_scr, s0_scr, m0_scr)


def _attn(da_q, da_c, lam_p, nw, lam_init, tq):
    has_x = da_q is not da_c
    bsz, l, _ = da_q.shape
    lctx = da_c.shape[1]
    nq = l // tq
    nt = bsz * DA_H * nq
    lx = l if has_x else 0
    kg = _pick_tile(l, (1024, 512, 256, 128))

    def split(t):
        return t // (DA_H * nq), (t // nq) % DA_H, t % nq

    def cur(col0, whole):
        def index(t):
            b, h, i = split(jnp.minimum(t, nt - 1))
            return (b, 0 if whole else i, col0 + h)
        return index

    def prev(col0, whole):
        def index(t):
            b, h, i = split(jnp.maximum(t - 1, 0))
            return (b, 0 if whole else i, col0 + h)
        return index

    in_specs = [pl.BlockSpec((1, tq, DA_V), cur(0, False))]
    args = [da_q]
    if has_x:
        in_specs += [pl.BlockSpec((1, l, DA_V), cur(4, True))]
        args += [da_q]
    in_specs += [pl.BlockSpec((1, lctx, DA_V), cur(4, True))]
    args += [da_c]
    if has_x:
        in_specs += [pl.BlockSpec((1, l, DA_V), prev(8, True))]
        args += [da_q]
    in_specs += [
        pl.BlockSpec((1, lctx, DA_V), prev(8, True)),
        pl.BlockSpec((4, 64), lambda t: (0, 0)),
        pl.BlockSpec((1, DA_V), lambda t: (0, 0)),
    ]
    args += [da_c, lam_p, nw]
    return pl.pallas_call(
        functools.partial(_attn_kernel, has_x=has_x, lam_init=lam_init, kg=kg, nq=nq),
        grid=(nt + 1,),
        in_specs=in_specs,
        out_specs=pl.BlockSpec((1, tq, DA_V), prev(0, False)),
        out_shape=jax.ShapeDtypeStruct((bsz, l, DA_H * DA_V), bf16),
        scratch_shapes=[
            pltpu.VMEM((2 * tq, lx + lctx), f32),
            pltpu.VMEM((2 * tq, lx + lctx), f32),
            pltpu.VMEM((2 * tq, LANES), f32),
            pltpu.VMEM((2 * tq, LANES), f32),
            pltpu.VMEM((lx + lctx, 2 * DA_V), bf16),
        ],
        compiler_params=_cparams(("arbitrary",)),
        name="attn_x" if has_x else "attn_c",
    )(*args)


def _head_ms(y, bd_b):
    return _dot2(y * y, bd_b) * (1.0 / 64.0)


def _outproj_kernel(x_ref, mod_ref, mlf_ref, mlb_ref, mlo_ref, hgf_ref, hgb_ref, hgg_ref, da_ref,
                    mlw_ref, hgw_ref, w_ref, nw_ref, o_ref):
    d = x_ref.shape[-1]
    rr = lax.broadcasted_iota(jnp.int32, (256, 256), 0) // 64
    cc = lax.broadcasted_iota(jnp.int32, (256, 256), 1) // 64
    bd_b = jnp.where(rr == cc, 1.0, 0.0).astype(bf16)
    ml = mlf_ref[0, 0] + mlb_ref[0, 0]
    ml = ml * lax.rsqrt(_head_ms(ml, bd_b) + EPS) * mlw_ref[...]
    ml = _sigmoid(mlo_ref[0].astype(f32)) * ml
    hg = hgf_ref[0, 0] + hgb_ref[0, 0]
    hg = hg * lax.rsqrt(_head_ms(hg, bd_b) + EPS) * hgw_ref[...]
    gg = hgg_ref[0].astype(f32)
    hg = hg * (gg * _sigmoid(gg))
    mix = (_dot(ml.astype(bf16), w_ref[0:256, :]) + _dot(hg.astype(bf16), w_ref[256:512, :])
           + _dot(da_ref[0], w_ref[512:1024, :]))
    ms = jnp.mean(mix * mix, axis=-1, keepdims=True)
    y = mix * lax.rsqrt(ms + EPS) * nw_ref[...]
    o_ref[0] = x_ref[0] + mod_ref[0][:, 2 * d:3 * d] * y


def _outproj(x, mod, mod_row0, per_batch, ml_out, ml, hg_out, hg, da_out, mlw, hgw, w, nw, tm, name):
    bsz, l, d = x.shape
    nt = l // tm
    mod_map = (lambda b, i: (mod_row0 + b, 0, 0)) if per_batch else (lambda b, i: (mod_row0, 0, 0))
    return pl.pallas_call(
        _outproj_kernel,
        grid=(bsz, nt),
        in_specs=[
            pl.BlockSpec((1, tm, d), lambda b, i: (b, i, 0)),
            pl.BlockSpec((1, 1, mod.shape[-1]), mod_map),
            pl.BlockSpec((1, 1, tm, 256), lambda b, i: (0, b, i, 0)),
            pl.BlockSpec((1, 1, tm, 256), lambda b, i: (1, b, i, 0)),
            pl.BlockSpec((1, tm, 256), lambda b, i: (b, i, 3)),
            pl.BlockSpec((1, 1, tm, 256), lambda b, i: (0, b, i, 0)),
            pl.BlockSpec((1, 1, tm, 256), lambda b, i: (1, b, i, 0)),
            pl.BlockSpec((1, tm, 256), lambda b, i: (b, i, 2)),
            pl.BlockSpec((1, tm, 512), lambda b, i: (b, i, 0)),
            pl.BlockSpec((1, 256), lambda b, i: (0, 0)),
            pl.BlockSpec((1, 256), lambda b, i: (0, 0)),
            pl.BlockSpec((d, d), lambda b, i: (0, 0)),
            pl.BlockSpec((1, d), lambda b, i: (0, 0)),
        ],
        out_specs=pl.BlockSpec((1, tm, d), lambda b, i: (b, i, 0)),
        out_shape=jax.ShapeDtypeStruct((bsz, l, d), f32),
        compiler_params=_cparams(("parallel", "parallel")),
        name=name,
    )(x, mod, ml_out, ml_out, ml, hg_out, hg_out, hg, da_out, mlw, hgw, w, nw)


def _ffn_kernel(*refs, moe):
    if moe:
        x_ref, mod_ref, nw1_ref, rw_ref, rb_ref, wg_ref, wu_ref, wd_ref, nw2_ref, o_ref, h_scr, acc_scr, gate_scr = refs
    else:
        x_ref, mod_ref, nw1_ref, wg_ref, wu_ref, wd_ref, nw2_ref, o_ref, h_scr, acc_scr = refs
    e = pl.program_id(2)
    j = pl.program_id(3)
    ne = pl.num_programs(2)
    nj = pl.num_programs(3)
    d = x_ref.shape[-1]
    tm = x_ref.shape[1]

    @pl.when((e == 0) & (j == 0))
    def _():
        x = x_ref[0]
        ms = jnp.mean(x * x, axis=-1, keepdims=True)
        h = x * lax.rsqrt(ms + EPS) * nw1_ref[...]
        mod = mod_ref[0]
        h = h * (1.0 + mod[:, 4 * d:5 * d]) + mod[:, 3 * d:4 * d]
        h_scr[...] = h.astype(bf16)
        acc_scr[...] = jnp.zeros_like(acc_scr)
        if moe:
            hh, hl = _split_hi_lo(h)
            rw = rw_ref[...]
            wh, wl = _split_hi_lo(rw)
            logits = _dot(hh, wh) + _dot(hl, wh) + _dot(hh, wl) + rb_ref[...]
            lane = lax.broadcasted_iota(jnp.int32, (tm, LANES), 1)
            real = lane < N_EXPERTS
            lg = jnp.where(real, logits, MASK_NEG)
            m1 = jnp.max(lg, axis=-1, keepdims=True)
            i1 = jnp.min(jnp.where(lg == m1, lane, LANES), axis=-1, keepdims=True)
            lg2 = jnp.where(lane == i1, MASK_NEG, lg)
            m2 = jnp.max(lg2, axis=-1, keepdims=True)
            i2 = jnp.min(jnp.where(lg2 == m2, lane, LANES), axis=-1, keepdims=True)
            e2 = jnp.exp(m2 - m1)
            g1 = 1.0 / (1.0 + e2)
            g2 = e2 / (1.0 + e2)
            gate_scr[...] = jnp.where(lane == i1, g1, 0.0) + jnp.where(lane == i2, g2, 0.0)

    hb = h_scr[...]
    a = _dot(hb, wg_ref[0])
    u = _dot(hb, wu_ref[0])
    act = (a * _sigmoid(a)) * u
    if moe:
        lane = lax.broadcasted_iota(jnp.int32, (tm, LANES), 1)
        ge = jnp.sum(jnp.where(lane == e, gate_scr[...], 0.0), axis=-1, keepdims=True)
        act = act * ge
    acc_scr[...] += _dot(act.astype(bf16), wd_ref[0])

    @pl.when((e == ne - 1) & (j == nj - 1))
    def _():
        y = acc_scr[...]
        ms = jnp.mean(y * y, axis=-1, keepdims=True)
        y = y * lax.rsqrt(ms + EPS) * nw2_ref[...]
        o_ref[0] = x_ref[0] + mod_ref[0][:, 5 * d:6 * d] * y


def _ffn(x, mod, mod_row0, per_batch, nw1, wg, wu, wd, nw2, tm, tf, router=None, name="ffn"):
    bsz, l, d = x.shape
    ne, _, ff = wg.shape
    nt = l // tm
    nf = ff // tf
    moe = router is not None
    mod_map = (lambda b, i, e, j: (mod_row0 + b, 0, 0)) if per_batch else (lambda b, i, e, j: (mod_row0, 0, 0))
    in_specs = [
        pl.BlockSpec((1, tm, d), lambda b, i, e, j: (b, i, 0)),
        pl.BlockSpec((1, 1, mod.shape[-1]), mod_map),
        pl.BlockSpec((1, d), lambda b, i, e, j: (0, 0)),
    ]
    args = [x, mod, nw1]
    scratch = [pltpu.VMEM((tm, d), bf16), pltpu.VMEM((tm, d), f32)]
    if moe:
        in_specs += [
            pl.BlockSpec((d, LANES), lambda b, i, e, j: (0, 0)),
            pl.BlockSpec((1, LANES), lambda b, i, e, j: (0, 0)),
        ]
        args += list(router)
        scratch += [pltpu.VMEM((tm, LANES), f32)]
    in_specs += [
        pl.BlockSpec((1, d, tf), lambda b, i, e, j: (e, 0, j)),
        pl.BlockSpec((1, d, tf), lambda b, i, e, j: (e, 0, j)),
        pl.BlockSpec((1, tf, d), lambda b, i, e, j: (e, j, 0)),
        pl.BlockSpec((1, d), lambda b, i, e, j: (0, 0)),
    ]
    args += [wg, wu, wd, nw2]
    return pl.pallas_call(
        functools.partial(_ffn_kernel, moe=moe),
        grid=(bsz, nt, ne, nf),
        in_specs=in_specs,
        out_specs=pl.BlockSpec((1, tm, d), lambda b, i, e, j: (b, i, 0)),
        out_shape=jax.ShapeDtypeStruct((bsz, l, d), f32),
        scratch_shapes=scratch,
        compiler_params=_cparams(("parallel", "parallel", "arbitrary", "arbitrary")),
        name=name,
    )(*args)


MOE_TM = 512
MOE_TS = 512
META_I1, META_I2, META_R1, META_R2, META_G1, META_G2 = range(6)


def _router_kernel(x_ref, mod_ref, nw_ref, rw_ref, rb_ref, tri_ref, h_ref, meta_ref, cnt_ref, base_scr):
    d = x_ref.shape[-1]
    tm = x_ref.shape[1]

    @pl.when((pl.program_id(0) == 0) & (pl.program_id(1) == 0))
    def _():
        base_scr[...] = jnp.zeros_like(base_scr)

    x = x_ref[0]
    ms = jnp.mean(x * x, axis=-1, keepdims=True)
    h = x * lax.rsqrt(ms + EPS) * nw_ref[...]
    mod = mod_ref[0]
    h = h * (1.0 + mod[:, 4 * d:5 * d]) + mod[:, 3 * d:4 * d]
    h_ref[0] = h
    hh, hl = _split_hi_lo(h)
    wh, wl = _split_hi_lo(rw_ref[...])
    logits = _dot(hh, wh) + _dot(hl, wh) + _dot(hh, wl) + rb_ref[...]
    lane = lax.broadcasted_iota(jnp.int32, (tm, LANES), 1)
    lg = jnp.where(lane < N_EXPERTS, logits, MASK_NEG)
    m1 = jnp.max(lg, axis=-1, keepdims=True)
    i1 = jnp.min(jnp.where(lg == m1, lane, LANES), axis=-1, keepdims=True)
    lg2 = jnp.where(lane == i1, MASK_NEG, lg)
    m2 = jnp.max(lg2, axis=-1, keepdims=True)
    i2 = jnp.min(jnp.where(lg2 == m2, lane, LANES), axis=-1, keepdims=True)
    e2 = jnp.exp(m2 - m1)
    g1 = 1.0 / (1.0 + e2)
    g2 = e2 / (1.0 + e2)
    oh = jnp.where((lane == i1) | (lane == i2), 1.0, 0.0)
    pos = base_scr[...] + _dot(tri_ref[...], oh.astype(bf16))
    r1 = jnp.sum(jnp.where(lane == i1, pos, 0.0), axis=-1, keepdims=True)
    r2 = jnp.sum(jnp.where(lane == i2, pos, 0.0), axis=-1, keepdims=True)
    meta = jnp.zeros((tm, LANES), f32)
    for k, val in ((META_I1, i1.astype(f32)), (META_I2, i2.astype(f32)), (META_R1, r1), (META_R2, r2),
                   (META_G1, g1), (META_G2, g2)):
        meta = jnp.where(lane == k, val, meta)
    meta_ref[0] = meta
    new_base = pos[tm - 1:tm, :] + oh[tm - 1:tm, :]
    base_scr[...] = new_base
    cnt_ref[...] = jnp.broadcast_to(new_base, cnt_ref.shape)


def _moe_route(x, mod, mod_row0, nw1, rw, rb, tm):
    bsz, l, d = x.shape
    nt = l // tm
    r = jnp.arange(tm)
    tri = (r[:, None] > r[None, :]).astype(bf16)
    return pl.pallas_call(
        _router_kernel,
        grid=(bsz, nt),
        in_specs=[
            pl.BlockSpec((1, tm, d), lambda b, i: (b, i, 0)),
            pl.BlockSpec((1, 1, mod.shape[-1]), lambda b, i: (mod_row0 + b, 0, 0)),
            pl.BlockSpec((1, d), lambda b, i: (0, 0)),
            pl.BlockSpec((d, LANES), lambda b, i: (0, 0)),
            pl.BlockSpec((1, LANES), lambda b, i: (0, 0)),
            pl.BlockSpec((tm, tm), lambda b, i: (0, 0)),
        ],
        out_specs=[
            pl.BlockSpec((1, tm, d), lambda b, i: (b, i, 0)),
            pl.BlockSpec((1, tm, LANES), lambda b, i: (b, i, 0)),
            pl.BlockSpec((8, LANES), lambda b, i: (0, 0)),
        ],
        out_shape=[
            jax.ShapeDtypeStruct((bsz, l, d), f32),
            jax.ShapeDtypeStruct((bsz, l, LANES), f32),
            jax.ShapeDtypeStruct((8, LANES), f32),
        ],
        scratch_shapes=[pltpu.VMEM((1, LANES), f32)],
        compiler_params=_cparams(("arbitrary", "arbitrary")),
        name="moe_route",
    )(x, mod, nw1, rw, rb, tri)


def _row_copy(src_ref, src_row, dst_ref, dst_row, sem):
    return pltpu.make_async_copy(src_ref.at[pl.ds(src_row, 1)], dst_ref.at[pl.ds(dst_row, 1)], sem)


def _scatter_kernel(dest_ref, h_ref, xs_in_ref, xs_ref, sem):
    del xs_in_ref
    i = pl.program_id(0)
    ts = h_ref.shape[0]
    base = i * (2 * ts)

    def issue(r, carry):
        _row_copy(h_ref, r, xs_ref, dest_ref[base + r], sem).start()
        _row_copy(h_ref, r, xs_ref, dest_ref[base + ts + r], sem).start()
        return carry

    lax.fori_loop(0, ts, issue, 0, unroll=8)

    def drain(r, carry):
        _row_copy(h_ref, 0, xs_ref, 0, sem).wait()
        _row_copy(h_ref, 0, xs_ref, 0, sem).wait()
        return carry

    lax.fori_loop(0, ts, drain, 0, unroll=8)


def _moe_scatter(h2, dest, n_rows, ts):
    t, d = h2.shape
    xs0 = jnp.zeros((n_rows, d), f32)
    return pl.pallas_call(
        _scatter_kernel,
        grid_spec=pltpu.PrefetchScalarGridSpec(
            num_scalar_prefetch=1,
            grid=(t // ts,),
            in_specs=[
                pl.BlockSpec((ts, d), lambda i, dest: (i, 0)),
                pl.BlockSpec(memory_space=pl.ANY),
            ],
            out_specs=pl.BlockSpec(memory_space=pl.ANY),
            scratch_shapes=[pltpu.SemaphoreType.DMA],
        ),
        out_shape=jax.ShapeDtypeStruct((n_rows, d), f32),
        input_output_aliases={2: 0},
        compiler_params=_cparams(("arbitrary",)),
        name="moe_scatter",
    )(dest, h2, xs0)


def _expert_kernel(te_ref, tv_ref, x_ref, wg_ref, wu_ref, wd_ref, o_ref, acc_scr):
    del te_ref
    j = pl.program_id(0)
    f = pl.program_id(1)
    nf = pl.num_programs(1)
    valid = tv_ref[j] == 1

    @pl.when(valid)
    def _():
        xb = x_ref[...].astype(bf16)
        a = _dot(xb, wg_ref[0])
        u = _dot(xb, wu_ref[0])
        part = _dot(((a * _sigmoid(a)) * u).astype(bf16), wd_ref[0])

        @pl.when(f == 0)
        def _():
            acc_scr[...] = part

        @pl.when(f > 0)
        def _():
            acc_scr[...] += part

        @pl.when(f == nf - 1)
        def _():
            o_ref[...] = acc_scr[...]

    @pl.when(jnp.logical_not(valid))
    def _():
        o_ref[...] = jnp.zeros_like(o_ref)


def _moe_experts(xs, te, tv, wg, wu, wd, tm, tf):
    n_rows, d = xs.shape
    ff = wg.shape[-1]
    nf = ff // tf
    n_tiles = te.shape[0]

    def fsel(j, f, tv):
        return jnp.where(tv[j] == 1, f, nf - 1)

    return pl.pallas_call(
        _expert_kernel,
        grid_spec=pltpu.PrefetchScalarGridSpec(
            num_scalar_prefetch=2,
            grid=(n_tiles, nf),
            in_specs=[
                pl.BlockSpec((tm, d), lambda j, f, te, tv: (j, 0)),
                pl.BlockSpec((1, d, tf), lambda j, f, te, tv: (te[j], 0, fsel(j, f, tv))),
                pl.BlockSpec((1, d, tf), lambda j, f, te, tv: (te[j], 0, fsel(j, f, tv))),
                pl.BlockSpec((1, tf, d), lambda j, f, te, tv: (te[j], fsel(j, f, tv), 0)),
            ],
            out_specs=pl.BlockSpec((tm, d), lambda j, f, te, tv: (j, 0)),
            scratch_shapes=[pltpu.VMEM((tm, d), f32)],
        ),
        out_shape=jax.ShapeDtypeStruct((n_rows, d), f32),
        compiler_params=_cparams(("arbitrary", "arbitrary")),
        name="moe_experts",
    )(te, tv, xs, wg, wu, wd)


def _combine_kernel(dest_ref, x_ref, mod_ref, meta_ref, nw_ref, ys_ref, o_ref, y1_scr, y2_scr, sem):
    i = pl.program_id(0)
    ts = x_ref.shape[0]
    d = x_ref.shape[1]
    base = i * (2 * ts)

    def issue(r, carry):
        _row_copy(ys_ref, dest_ref[base + r], y1_scr, r, sem).start()
        _row_copy(ys_ref, dest_ref[base + ts + r], y2_scr, r, sem).start()
        return carry

    lax.fori_loop(0, ts, issue, 0, unroll=8)

    def drain(r, carry):
        _row_copy(ys_ref, 0, y1_scr, 0, sem).wait()
        _row_copy(ys_ref, 0, y2_scr, 0, sem).wait()
        return carry

    lax.fori_loop(0, ts, drain, 0, unroll=8)

    meta = meta_ref[...]
    y = meta[:, META_G1:META_G1 + 1] * y1_scr[...] + meta[:, META_G2:META_G2 + 1] * y2_scr[...]
    ms = jnp.mean(y * y, axis=-1, keepdims=True)
    y = y * lax.rsqrt(ms + EPS) * nw_ref[...]
    o_ref[...] = x_ref[...] + mod_ref[0][:, 5 * d:6 * d] * y


def _moe_combine(x2, mod, mod_row0, tokens_per_batch, meta2, nw2, ys, dest, ts):
    t, d = x2.shape
    per_b = tokens_per_batch // ts
    return pl.pallas_call(
        _combine_kernel,
        grid_spec=pltpu.PrefetchScalarGridSpec(
            num_scalar_prefetch=1,
            grid=(t // ts,),
            in_specs=[
                pl.BlockSpec((ts, d), lambda i, dest: (i, 0)),
                pl.BlockSpec((1, 1, mod.shape[-1]), lambda i, dest: (mod_row0 + i // per_b, 0, 0)),
                pl.BlockSpec((ts, LANES), lambda i, dest: (i, 0)),
                pl.BlockSpec((1, d), lambda i, dest: (0, 0)),
                pl.BlockSpec(memory_space=pl.ANY),
            ],
            out_specs=pl.BlockSpec((ts, d), lambda i, dest: (i, 0)),
            scratch_shapes=[pltpu.VMEM((ts, d), f32), pltpu.VMEM((ts, d), f32), pltpu.SemaphoreType.DMA],
        ),
        out_shape=jax.ShapeDtypeStruct((t, d), f32),
        compiler_params=_cparams(("arbitrary",)),
        name="moe_combine",
    )(dest, x2, mod, meta2, nw2, ys)


def _moe_sparse(x, mod, mod_row0, nw1, rw, rb, wg, wu, wd, nw2):
    bsz, l, d = x.shape
    t = bsz * l
    ne = wg.shape[0]
    tm_r = _pick_tile(l, (1024, 512, 256, 128))
    tm = min(MOE_TM, t)
    ts = _pick_tile(l, (MOE_TS, 256, 128))
    tf = _pick_tile(wg.shape[-1], (896, 512, 256, 128))

    h, meta, cnt = _moe_route(x, mod, mod_row0, nw1, rw, rb, tm_r)
    meta2 = meta.reshape(t, LANES)
    counts = cnt[0, :ne].astype(jnp.int32)
    tiles_e = (counts + tm - 1) // tm
    ends = jnp.cumsum(tiles_e)
    offset = (ends - tiles_e) * tm
    n_tiles = (2 * t) // tm + ne
    i1 = meta2[:, META_I1].astype(jnp.int32)
    i2 = meta2[:, META_I2].astype(jnp.int32)
    dest1 = jnp.take(offset, i1) + meta2[:, META_R1].astype(jnp.int32)
    dest2 = jnp.take(offset, i2) + meta2[:, META_R2].astype(jnp.int32)
    dest = jnp.stack([dest1.reshape(t // ts, ts), dest2.reshape(t // ts, ts)], axis=1).reshape(-1)
    jt = jnp.arange(n_tiles, dtype=jnp.int32)
    total = ends[ne - 1]
    tv = (jt < total).astype(jnp.int32)
    te_all = jnp.minimum(jnp.sum((jt[:, None] >= ends[None, :]).astype(jnp.int32), axis=1), ne - 1)
    te_last = jnp.take(te_all, jnp.maximum(total - 1, 0))
    te = jnp.where(tv == 1, te_all, te_last)
    n_rows = n_tiles * tm

    xs = _moe_scatter(h.reshape(t, d), dest, n_rows, ts)
    ys = _moe_experts(xs, te, tv, wg, wu, wd, tm, tf)
    out = _moe_combine(x.reshape(t, d), mod, mod_row0, l, meta2, nw2, ys, dest, ts)
    return out.reshape(bsz, l, d)


def _pick_tile(n, prefs):
    for t in prefs:
        if n % t == 0:
            return t
    return n


def kernel(x, c, ctx, c_ctx, w_mod, b_mod, norm_pre_mix, norm_post_mix, norm_pre_ffn, norm_post_ffn, w_in, w_out, ml_gate_b, ml_norm, hg_lb_logits, hg_norm, da_lambda, da_norm, ffn_w_gate, ffn_w_up, ffn_w_down, router_w, router_b, moe_w_gate, moe_w_up, moe_w_down):
    depth = w_mod.shape[0]
    bsz, l, d = x.shape
    lctx = ctx.shape[1]

    mod = _mod_vectors(c, c_ctx, w_mod, b_mod)
    rows = mod.shape[1]
    mod = mod.reshape(depth * rows, 1, 6 * d)

    lb_soft = jax.nn.softmax(hg_lb_logits.astype(f32), axis=0)
    lower_bounds = jnp.cumsum(lb_soft, axis=0) - lb_soft[0]

    tables = _rope_tables(l)
    ml_lc = min(ML_CHUNK, lctx)
    tri_ml_x = _tri_pair(min(ML_CHUNK, l), min(ML_CHUNK, l))
    tri_ml_c = _tri_pair(ml_lc, ml_lc)
    tri_hg_x = _tri_pair(min(HG_GROUP, l), HG_CHUNK)
    tri_hg_c = _tri_pair(min(HG_GROUP, lctx), HG_CHUNK)
    hg_bias = _hgrn2_bias()
    ml_sel = _mlstm_sel()

    tm_x = _pick_tile(l, (512, 256, 128))
    tm_c = _pick_tile(lctx, (256, 128))
    tq = _pick_tile(l, (256, 128))

    for layer in range(depth):
        need_ctx = layer < depth - 1
        lam_init = 0.8 - 0.6 * math.exp(-0.3 * layer)
        row0 = layer * rows
        w_in_l = _prep_w_in(w_in[layer])
        w_out_l = w_out[layer].astype(bf16)
        nw_pre = norm_pre_mix[layer].reshape(1, d)

        pc = _inproj(ctx, mod, row0 + bsz, False, nw_pre, w_in_l, None, tm_c)
        px = _inproj(x, mod, row0, True, nw_pre, w_in_l, tables, tm_x)
        ml_c, mlg_c, hg_c, hgf_c, da_c = pc
        ml_x, mlg_x, hg_x, hgf_x, da_x = px

        gb = ml_gate_b[layer].astype(f32)
        z = jnp.zeros((LANES - 8,), f32)
        gate_b2 = jnp.stack([jnp.concatenate([gb[0:8], z]), jnp.concatenate([gb[8:16], z])]).reshape(2, 1, LANES)
        s0 = (jnp.zeros((2, bsz, ML_H, LANES, 2 * LANES), f32), jnp.zeros((2, bsz, ML_H, 8, LANES), f32))
        mlo_c, s_ml = _mlstm(ml_c, mlg_c, gate_b2, tri_ml_c, ml_sel, s0)
        mlo_x, _ = _mlstm(ml_x, mlg_x, gate_b2, tri_ml_x, ml_sel, s_ml)

        lb = lower_bounds[layer]
        lbc = jnp.zeros((8, HG_W), f32)
        lbc = lbc.at[0].set(jnp.log(jnp.maximum(lb, LB_FLOOR))).at[1].set(jnp.log1p(-lb)).at[2].set(1.0 - lb)
        h0 = jnp.zeros((2, bsz, HG_W, HG_W // 2), f32)
        hgo_c, s_hg = _hgrn2(hg_c, hgf_c, lbc, tri_hg_c, hg_bias, h0)
        hgo_x, _ = _hgrn2(hg_x, hgf_x, lbc, tri_hg_x, hg_bias, s_hg)

        lam_p = da_lambda[layer].astype(f32)
        da_nw = da_norm[layer].reshape(1, DA_V)
        dao_x = _attn(da_x, da_c, lam_p, da_nw, lam_init, tq)

        mlw = ml_norm[layer].reshape(1, 256)
        hgw = hg_norm[layer].reshape(1, 256)
        nw_post = norm_post_mix[layer].reshape(1, d)
        x = _outproj(x, mod, row0, True, mlo_x, ml_x, hgo_x, hg_x, dao_x, mlw, hgw, w_out_l, nw_post, tm_x, "outproj_x")
        if need_ctx:
            dao_c = _attn(da_c, da_c, lam_p, da_nw, lam_init, lctx)
            ctx = _outproj(ctx, mod, row0 + bsz, False, mlo_c, ml_c, hgo_c, hg_c, dao_c, mlw, hgw, w_out_l, nw_post, tm_c, "outproj_c")

        i = layer // 2
        nw1 = norm_pre_ffn[layer].reshape(1, d)
        nw2 = norm_post_ffn[layer].reshape(1, d)
        if layer % 2 == 0:
            wg = ffn_w_gate[i].astype(bf16)[None]
            wu = ffn_w_up[i].astype(bf16)[None]
            wd = ffn_w_down[i].astype(bf16)[None]
            router = None
        else:
            wg = moe_w_gate[i].astype(bf16)
            wu = moe_w_up[i].astype(bf16)
            wd = moe_w_down[i].astype(bf16)
            rw = jnp.zeros((d, LANES), f32).at[:, :N_EXPERTS].set(router_w[i])
            rb = jnp.zeros((1, LANES), f32).at[0, :N_EXPERTS].set(router_b[i])
            router = (rw, rb)
        ff = wg.shape[-1]
        tf = _pick_tile(ff, (896, 1408, 512, 256))
        tm_f = _pick_tile(l, (1024, 512, 256, 128) if tf <= 896 else (512, 256, 128))
        if router is None:
            x = _ffn(x, mod, row0, True, nw1, wg, wu, wd, nw2, tm_f, tf, None, name="ffn_x%d" % layer)
        else:
            x = _moe_sparse(x, mod, row0, nw1, rw, rb, wg, wu, wd, nw2)
        if need_ctx:
            ctx = _ffn(ctx, mod, row0 + bsz, False, nw1, wg, wu, wd, nw2, tm_c, tf, router, name="ffn_c%d" % layer)
    return x
```

```python
import functools
import math

import numpy as np
import jax
import jax.numpy as jnp
from jax import lax
from jax.experimental import pallas as pl
from jax.experimental.pallas import tpu as pltpu

f32 = jnp.float32
bf16 = jnp.bfloat16

ML_H = 4
HG_H = 4
DA_H = 4
GRID_W = 64
ROPE_BASE = 10000.0
N_EXPERTS = 8
EPS = 1e-6
MASK_NEG = -1e30
LB_FLOOR = 1e-30

LANES = 128
ML_CHUNK = 256
HG_CHUNK = 16
HG_GROUP = 256
VMEM_LIMIT = 56 * 1024 * 1024


def _cparams(sem):
    return pltpu.CompilerParams(dimension_semantics=sem, vmem_limit_bytes=VMEM_LIMIT)


def _sigmoid(x):
    return 1.0 / (1.0 + jnp.exp(-x))


def _log_sigmoid(x):
    return jnp.minimum(x, 0.0) - jnp.log(1.0 + jnp.exp(-jnp.abs(x)))


def _split_hi_lo(a):
    hi = a.astype(bf16)
    lo = (a - hi.astype(f32)).astype(bf16)
    return hi, lo


def _dot(a, b):
    return jnp.dot(a, b, preferred_element_type=f32)


def _dot_nt(a, b):
    return lax.dot_general(a, b, (((1,), (1,)), ((), ())), preferred_element_type=f32)


def _dot_tn(a, b):
    return lax.dot_general(a, b, (((0,), (0,)), ((), ())), preferred_element_type=f32)


def _dot2(a_f32, b_bf16):
    hi, lo = _split_hi_lo(a_f32)
    return _dot(hi, b_bf16) + _dot(lo, b_bf16)


def _mod_kernel(s_ref, w_ref, b_ref, o_ref):
    s = s_ref[...]
    s = s * _sigmoid(s)
    o_ref[0] = _dot(s.astype(bf16), w_ref[0].astype(bf16)) + b_ref[0]


def _mod_vectors(c, c_ctx, w_mod, b_mod):
    depth, d, n = w_mod.shape
    bsz = c.shape[0]
    rows = 8
    s = jnp.zeros((rows, d), f32).at[:bsz].set(c).at[bsz].set(c_ctx)
    tn = n // 4
    out = pl.pallas_call(
        _mod_kernel,
        grid=(depth, n // tn),
        in_specs=[
            pl.BlockSpec((rows, d), lambda l, j: (0, 0)),
            pl.BlockSpec((1, d, tn), lambda l, j: (l, 0, j)),
            pl.BlockSpec((1, 1, tn), lambda l, j: (l, 0, j)),
        ],
        out_specs=pl.BlockSpec((1, rows, tn), lambda l, j: (l, 0, j)),
        out_shape=jax.ShapeDtypeStruct((depth, rows, n), f32),
        compiler_params=_cparams(("parallel", "parallel")),
        name="mod_vectors",
    )(s, w_mod, b_mod.reshape(depth, 1, n))
    return out


P_ML = (0, 1024)
P_MLG = (1024, 1280)
P_HG = (1280, 2048)
P_HGF = (2048, 2560)
P_DA = (2560, 4096)
P_TOTAL = 4096


def _prep_w_in(w):
    d = w.shape[0]
    ml = w[:, 0:1024]
    g = w[:, 1024:1040]
    z = jnp.zeros((d, LANES - 8), w.dtype)
    gates = jnp.concatenate([g[:, 0:8], z, g[:, 8:16], z], axis=1)
    hg_q = w[:, 1040:1296]
    hg_f = w[:, 1296:1808]
    hg_i = w[:, 1808:2064]
    hg_g = w[:, 2064:2320]
    da = w[:, 2320:3856]
    return jnp.concatenate([ml, gates, hg_q, hg_i, hg_g, hg_f, da], axis=1).astype(bf16)


def _inproj_kernel(*refs, rope):
    if rope:
        x_ref, mod_ref, nw_ref, w_ref, cos_ref, sin_ref = refs[:6]
        outs = refs[6:]
    else:
        x_ref, mod_ref, nw_ref, w_ref = refs[:4]
        outs = refs[4:]
    ml_ref, mlg_ref, hg_ref, hgf_ref, da_ref = outs
    d = x_ref.shape[-1]
    x = x_ref[0]
    ms = jnp.mean(x * x, axis=-1, keepdims=True)
    h = x * lax.rsqrt(ms + EPS) * nw_ref[...]
    mod = mod_ref[0]
    h = h * (1.0 + mod[:, d:2 * d]) + mod[:, 0:d]
    hb = h.astype(bf16)

    def proj(lo, hi):
        return _dot(hb, w_ref[:, lo:hi])

    ml_ref[0] = proj(*P_ML).astype(bf16)
    mlg_ref[0] = proj(*P_MLG)
    pq = proj(P_HG[0], P_HG[0] + 256)
    hg_ref[0, :, 0:256] = (pq * _sigmoid(pq)).astype(bf16)
    hg_ref[0, :, 256:768] = proj(P_HG[0] + 256, P_HG[1]).astype(bf16)
    hgf_ref[0] = proj(*P_HGF)
    q = proj(P_DA[0], P_DA[0] + 512)
    k = proj(P_DA[0] + 512, P_DA[0] + 1024)
    if rope:
        cos = jnp.concatenate([cos_ref[...]] * 4, axis=1)
        sin = jnp.concatenate([sin_ref[...]] * 4, axis=1)
        lane = lax.broadcasted_iota(jnp.int32, (1, 512), 1)
        first = (lane % 32) < 16

        def rot(t):
            sw = jnp.where(first, pltpu.roll(t, 512 - 16, 1), pltpu.roll(t, 16, 1))
            return t * cos + sw * sin

        q = rot(q)
        k = rot(k)
    da_ref[0, :, 0:512] = (q * 0.125).astype(bf16)
    da_ref[0, :, 512:1024] = k.astype(bf16)
    da_ref[0, :, 1024:1536] = proj(P_DA[0] + 1024, P_DA[1]).astype(bf16)


def _inproj(x, mod, mod_row0, per_batch, nw, w, tables, tm):
    bsz, l, d = x.shape
    rope = tables is not None
    nt = l // tm
    mod_map = (lambda b, i: (mod_row0 + b, 0, 0)) if per_batch else (lambda b, i: (mod_row0, 0, 0))
    in_specs = [
        pl.BlockSpec((1, tm, d), lambda b, i: (b, i, 0)),
        pl.BlockSpec((1, 1, mod.shape[-1]), mod_map),
        pl.BlockSpec((1, d), lambda b, i: (0, 0)),
        pl.BlockSpec((d, P_TOTAL), lambda b, i: (0, 0)),
    ]
    args = [x, mod, nw, w]
    if rope:
        in_specs += [pl.BlockSpec((tm, LANES), lambda b, i: (i, 0))] * 2
        args += list(tables)
    widths = (1024, 256, 768, 512, 1536)
    dtypes = (bf16, f32, bf16, f32, bf16)
    out_specs = [pl.BlockSpec((1, tm, wd), lambda b, i: (b, i, 0)) for wd in widths]
    out_shape = [jax.ShapeDtypeStruct((bsz, l, wd), dt) for wd, dt in zip(widths, dtypes)]
    return pl.pallas_call(
        functools.partial(_inproj_kernel, rope=rope),
        grid=(bsz, nt),
        in_specs=in_specs,
        out_specs=out_specs,
        out_shape=out_shape,
        compiler_params=_cparams(("parallel", "parallel")),
        name="inproj_x" if rope else "inproj_c",
    )(*args)


def _rope_tables(l):
    rows = l // GRID_W
    row = jnp.repeat(jnp.arange(rows), GRID_W).astype(f32)
    col = jnp.tile(jnp.arange(GRID_W), rows).astype(f32)
    n_freq = 16
    inv = ROPE_BASE ** (-jnp.arange(n_freq, dtype=f32) / n_freq)
    lane = jnp.arange(LANES)
    freq = inv[lane % n_freq]
    pos = jnp.where(((lane % 64) // 32 == 0)[None, :], row[:, None], col[:, None])
    ang = pos * freq[None, :]
    sign = jnp.where((lane % 32) < 16, -1.0, 1.0)[None, :]
    return jnp.cos(ang), jnp.sin(ang) * sign


ML_D = 64


def _mlstm_kernel(q_ref, k_ref, v_ref, g_ref, gb_ref, tri_ref, sel_ref, cn0_ref, m0_ref,
                  o_ref, cnf_ref, mf_ref, cn_scr, m_scr):
    d = pl.program_id(0)
    j = pl.program_id(2)
    nj = pl.num_programs(2)
    lc = q_ref.shape[1]
    nrep = lc // LANES

    @pl.when(j == 0)
    def _():
        cn_scr[...] = cn0_ref[0, 0]
        m_scr[...] = m0_ref[0, 0]

    g = g_ref[0] + gb_ref[0]
    lf = _log_sigmoid(g)
    hi, lo = _split_hi_lo(lf)
    bc = _dot(tri_ref[0], hi) + _dot(tri_ref[0], lo)
    z = pltpu.roll(g, 4, 1) - bc
    zt = z.T
    sel = sel_ref[...]
    yrep = _dot2(bc, sel)
    zrep = _dot2(z, sel)
    fwd = d == 0
    r = lax.broadcasted_iota(jnp.int32, (lc, lc), 0)
    c = lax.broadcasted_iota(jnp.int32, (lc, lc), 1)
    mask = (r - c) * (1 - 2 * d) >= 0
    lane = lax.broadcasted_iota(jnp.int32, (lc, LANES), 1)
    ones_b = jnp.ones((lc, LANES), bf16)

    def rep(x, n):
        return jnp.concatenate([x] * n, axis=1)

    for p in range(ML_H // 2):
        qp = q_ref[0, :, p * LANES:(p + 1) * LANES]
        kp = k_ref[0, :, p * LANES:(p + 1) * LANES] * jnp.asarray(0.125, bf16)
        vp = v_ref[0, :, p * LANES:(p + 1) * LANES]
        outs = []
        for hh in range(2):
            h = 2 * p + hh
            half = (lane < ML_D) if hh == 0 else (lane >= ML_D)
            b_rep = yrep[:, h * LANES:(h + 1) * LANES]
            ab_rep = zrep[:, h * LANES:(h + 1) * LANES]
            blast = jnp.where(fwd, b_rep[lc - 1:lc, :], b_rep[0:1, :])
            m0 = m_scr[h, 0:1, :]
            z_row = zt[4 + h:5 + h, :]

            dm = jnp.where(mask, rep(b_rep, nrep) + z_row, MASK_NEG)
            inter = b_rep + m0
            m = jnp.maximum(inter, jnp.max(dm, axis=-1, keepdims=True))
            w_intra = jnp.exp(dm - rep(m, nrep))
            w_inter = jnp.exp(inter - m)

            qm = jnp.where(half, qp, jnp.zeros_like(qp))
            vo = jnp.concatenate([jnp.where(half, vp, jnp.zeros_like(vp)), ones_b], axis=1)
            s = _dot_nt(qm, kp) * w_intra
            cn = cn_scr[h]
            nd = _dot(s.astype(bf16), vo) + rep(w_inter, 2) * _dot(qm, cn.astype(bf16))
            den = nd[:, LANES:2 * LANES]
            outs.append(nd[:, 0:LANES] / jnp.maximum(jnp.abs(den), jnp.exp(-m)))

            a_rep = blast + ab_rep
            m_loc = jnp.max(a_rep, axis=0, keepdims=True)
            w = jnp.exp(a_rep - m_loc)
            kw = jnp.where(half, kp.astype(f32) * w, 0.0).astype(bf16)
            dcn = _dot_tn(kw, vo)
            m_new = jnp.maximum(blast + m0, m_loc)
            s_old = jnp.exp(blast + m0 - m_new)
            s_new = jnp.exp(m_loc - m_new)
            cn_scr[h] = rep(s_old, 2) * cn + rep(s_new, 2) * dcn
            m_scr[h] = jnp.broadcast_to(m_new, (8, LANES))
        o_ref[0, 0, :, p * LANES:(p + 1) * LANES] = outs[0] + outs[1]

    @pl.when(j == nj - 1)
    def _():
        cnf_ref[0, 0] = cn_scr[...]
        mf_ref[0, 0] = m_scr[...]


def _mlstm_sel():
    sel = np.zeros((LANES, ML_H * LANES), np.float32)
    for h in range(ML_H):
        sel[4 + h, h * LANES:(h + 1) * LANES] = 1.0
    return jnp.asarray(sel, dtype=bf16)


def _mlstm(ml, mlg, gate_b2, tri, sel, state):
    bsz, l, _ = ml.shape
    lc = min(ML_CHUNK, l)
    nj = l // lc
    cn0, m0 = state

    def tmap(col):
        return lambda d, b, j: (b, jnp.where(d == 0, j, nj - 1 - j), col)

    cn_spec = pl.BlockSpec((1, 1, ML_H, LANES, 2 * LANES), lambda d, b, j: (d, b, 0, 0, 0))
    m_spec = pl.BlockSpec((1, 1, ML_H, 8, LANES), lambda d, b, j: (d, b, 0, 0, 0))
    out, cnf, mf = pl.pallas_call(
        _mlstm_kernel,
        grid=(2, bsz, nj),
        in_specs=[
            pl.BlockSpec((1, lc, 256), tmap(0)),
            pl.BlockSpec((1, lc, 256), tmap(1)),
            pl.BlockSpec((1, lc, 256), tmap(2)),
            pl.BlockSpec((1, lc, LANES), lambda d, b, j: (b, jnp.where(d == 0, j, nj - 1 - j), d)),
            pl.BlockSpec((1, 1, LANES), lambda d, b, j: (d, 0, 0)),
            pl.BlockSpec((1, lc, lc), lambda d, b, j: (d, 0, 0)),
            pl.BlockSpec((LANES, ML_H * LANES), lambda d, b, j: (0, 0)),
            cn_spec,
            m_spec,
        ],
        out_specs=[
            pl.BlockSpec((1, 1, lc, 256), lambda d, b, j: (d, b, jnp.where(d == 0, j, nj - 1 - j), 0)),
            cn_spec,
            m_spec,
        ],
        out_shape=[
            jax.ShapeDtypeStruct((2, bsz, l, 256), f32),
            jax.ShapeDtypeStruct((2, bsz, ML_H, LANES, 2 * LANES), f32),
            jax.ShapeDtypeStruct((2, bsz, ML_H, 8, LANES), f32),
        ],
        scratch_shapes=[pltpu.VMEM((ML_H, LANES, 2 * LANES), f32), pltpu.VMEM((ML_H, 8, LANES), f32)],
        compiler_params=_cparams(("parallel", "parallel", "arbitrary")),
        name="mlstm",
    )(ml, ml, ml, mlg, gate_b2, tri, sel, cn0, m0)
    return out, (cnf, mf)


def _tri_pair(n, block):
    r = np.arange(n)[:, None]
    c = np.arange(n)[None, :]
    same = (r // block) == (c // block)
    lower = (same & (r >= c)).astype(np.float32)
    upper = (same & (r <= c)).astype(np.float32)
    return jnp.asarray(np.stack([lower, upper]), dtype=bf16)


HG_W = 256


def _hgrn2_kernel(q_ref, v_ref, f_ref, lbc_ref, tri_ref, bias_ref, s0_ref, o_ref, sf_ref, st_ref, p_scr):
    d = pl.program_id(0)
    j = pl.program_id(2)
    nj = pl.num_programs(2)
    lg = q_ref.shape[1]
    nck = lg // HG_CHUNK
    hw = HG_W // 2

    @pl.when(j == 0)
    def _():
        st_ref[...] = s0_ref[0, 0]

    fr = f_ref[0]
    log_lb = lbc_ref[0:1, :]
    log_1m_lb = lbc_ref[1:2, :]
    one_m_lb = lbc_ref[2:3, :]
    t = jnp.exp(-jnp.abs(fr))
    b = log_1m_lb + (jnp.minimum(fr, 0.0) - jnp.log(1.0 + t))
    kk = one_m_lb * (jnp.where(fr >= 0.0, t, 1.0) / (1.0 + t))
    log_f = jnp.maximum(log_lb, b) + jnp.log(1.0 + jnp.exp(-jnp.abs(log_lb - b)))
    hi, lo = _split_hi_lo(log_f)
    g = _dot(tri_ref[0], hi) + _dot(tri_ref[0], lo)
    q = q_ref[0].astype(f32)
    qe = (q * jnp.exp(g)).astype(bf16)

    rr = lax.broadcasted_iota(jnp.int32, (hw, hw), 0) // 64
    cc = lax.broadcasted_iota(jnp.int32, (hw, hw), 1) // 64
    same_head = rr == cc
    bd_b = jnp.where(same_head, 1.0, 0.0).astype(bf16)

    sub = 8
    nsub = HG_CHUNK // sub

    def run(order, last_row, fwd):
        def block_kind(s, tb):
            lo, hi = tb * sub, tb * sub + sub - 1
            if fwd:
                return "all" if lo >= s else ("none" if hi < s else "some")
            return "all" if hi <= s else ("none" if lo > s else "some")

        def build_p(c):
            r0 = c * HG_CHUNK
            gc = g[r0:r0 + HG_CHUNK]
            kc = kk[r0:r0 + HG_CHUNK]
            qc = q[r0:r0 + HG_CHUNK]
            for s in range(HG_CHUNK):
                blocks = []
                for tb in range(nsub):
                    kind = block_kind(s, tb)
                    if kind == "none":
                        blocks.append(jnp.zeros((sub, HG_W), f32))
                        continue
                    diff = gc[tb * sub:(tb + 1) * sub] - gc[s:s + 1, :]
                    if kind == "some":
                        diff = diff + bias_ref[0, s * HG_CHUNK + tb * sub:s * HG_CHUNK + (tb + 1) * sub, :]
                    blocks.append(qc[tb * sub:(tb + 1) * sub] * kc[s:s + 1, :] * jnp.exp(diff))
                p_scr[c, s * HG_CHUNK:(s + 1) * HG_CHUNK, :] = jnp.concatenate(blocks, axis=0).astype(bf16)

        def finish(c):
            r0 = c * HG_CHUNK
            gc = g[r0:r0 + HG_CHUNK]
            kc = kk[r0:r0 + HG_CHUNK]
            vcb = v_ref[0, r0:r0 + HG_CHUNK, :]
            vc = vcb.astype(f32)
            abc = jnp.concatenate([_dot(p_scr[c, :, 0:hw], bd_b), _dot(p_scr[c, :, hw:HG_W], bd_b)], axis=1)
            o_blocks = []
            for tb in range(nsub):
                acc = None
                for s in range(HG_CHUNK):
                    if block_kind(s, tb) == "none":
                        continue
                    term = abc[s * HG_CHUNK + tb * sub:s * HG_CHUNK + (tb + 1) * sub, :] * vc[s:s + 1, :]
                    acc = term if acc is None else acc + term
                o_blocks.append(acc)
            o = jnp.concatenate(o_blocks, axis=0)
            st = st_ref[...]
            stb = st.astype(bf16)
            qec = qe[r0:r0 + HG_CHUNK]
            o = o + jnp.concatenate([_dot_nt(qec[:, 0:hw], stb[0:hw]), _dot_nt(qec[:, hw:HG_W], stb[hw:HG_W])], axis=1)
            o_ref[0, 0, r0:r0 + HG_CHUNK, :] = o
            g_last = gc[last_row:last_row + 1, :]
            kt = (kc * jnp.exp(g_last - gc)).astype(bf16)
            dec = jnp.exp(g_last)
            d_lo = _dot_tn(vcb[:, 0:hw], kt[:, 0:hw])
            d_hi = _dot_tn(vcb[:, hw:HG_W], kt[:, hw:HG_W])
            st_ref[0:hw, :] = st[0:hw] * dec[:, 0:hw] + jnp.where(same_head, d_lo, 0.0)
            st_ref[hw:HG_W, :] = st[hw:HG_W] * dec[:, hw:HG_W] + jnp.where(same_head, d_hi, 0.0)

        order = list(order)
        build_p(order[0])
        for idx, c in enumerate(order):
            if idx + 1 < len(order):
                build_p(order[idx + 1])
            finish(c)

    @pl.when(d == 0)
    def _():
        run(range(nck), HG_CHUNK - 1, True)

    @pl.when(d == 1)
    def _():
        run(range(nck - 1, -1, -1), 0, False)

    @pl.when(j == nj - 1)
    def _():
        sf_ref[0, 0] = st_ref[...]


def _hgrn2(hg, hgf, lbc, tri, bias, s0):
    bsz, l, _ = hg.shape
    lg = min(HG_GROUP, l)
    nj = l // lg
    hw = HG_W // 2

    def tmap(col):
        return lambda d, b, j: (b, jnp.where(d == 0, j, nj - 1 - j), col)

    out, sf = pl.pallas_call(
        _hgrn2_kernel,
        grid=(2, bsz, nj),
        in_specs=[
            pl.BlockSpec((1, lg, HG_W), tmap(0)),
            pl.BlockSpec((1, lg, HG_W), tmap(1)),
            pl.BlockSpec((1, lg, HG_W), lambda d, b, j: (b, jnp.where(d == 0, j, nj - 1 - j), d)),
            pl.BlockSpec((8, HG_W), lambda d, b, j: (0, 0)),
            pl.BlockSpec((1, lg, lg), lambda d, b, j: (d, 0, 0)),
            pl.BlockSpec((1, HG_CHUNK * HG_CHUNK, HG_W), lambda d, b, j: (d, 0, 0)),
            pl.BlockSpec((1, 1, HG_W, hw), lambda d, b, j: (d, b, 0, 0)),
        ],
        out_specs=[
            pl.BlockSpec((1, 1, lg, HG_W), lambda d, b, j: (d, b, jnp.where(d == 0, j, nj - 1 - j), 0)),
            pl.BlockSpec((1, 1, HG_W, hw), lambda d, b, j: (d, b, 0, 0)),
        ],
        out_shape=[
            jax.ShapeDtypeStruct((2, bsz, l, HG_W), f32),
            jax.ShapeDtypeStruct((2, bsz, HG_W, hw), f32),
        ],
        scratch_shapes=[
            pltpu.VMEM((HG_W, hw), f32),
            pltpu.VMEM((lg // HG_CHUNK, HG_CHUNK * HG_CHUNK, HG_W), bf16),
        ],
        compiler_params=_cparams(("parallel", "parallel", "arbitrary")),
        name="hgrn2",
    )(hg, hg, hgf, lbc, tri, bias, s0)
    return out, sf


def _hgrn2_bias():
    idx = np.arange(HG_CHUNK * HG_CHUNK)
    s, t = idx // HG_CHUNK, idx % HG_CHUNK
    fwd = np.where(t >= s, 0.0, MASK_NEG).astype(np.float32)
    bwd = np.where(t <= s, 0.0, MASK_NEG).astype(np.float32)
    return jnp.asarray(np.broadcast_to(np.stack([fwd, bwd])[:, :, None], (2, HG_CHUNK * HG_CHUNK, HG_W)))


DA_V = 128


def _attn_kernel(*refs, has_x, lam_init, kg, nq):
    if has_x:
        (q_ref, kx_ref, kc_ref, vx_ref, vc_ref, lp_ref, nw_ref, o_ref,
         s0_scr, s1_scr, m0_scr, m1_scr, a0_scr, a1_scr, va_scr) = refs
        lx = kx_ref.shape[1]
    else:
        q_ref, kc_ref, vc_ref, lp_ref, nw_ref, o_ref, s0_scr, s1_scr, m0_scr, m1_scr, a0_scr, a1_scr, va_scr = refs
        lx = 0
    tq = q_ref.shape[1]
    lctx = kc_ref.shape[1]
    t = pl.program_id(0)

    @pl.when(t == 0)
    def _():
        s1_scr[...] = jnp.zeros(s1_scr.shape, f32)
        m1_scr[...] = jnp.zeros(m1_scr.shape, f32)
        a1_scr[...] = jnp.ones(a1_scr.shape, f32)

    @pl.when((t == 0) | ((t - 1) % nq == 0))
    def _():
        def ones_col(n):
            lane = lax.broadcasted_iota(jnp.int32, (n, DA_V), 1)
            return jnp.where(lane == 0, 1.0, 0.0).astype(bf16)

        if has_x:
            va_scr[0:lx, 0:DA_V] = vx_ref[0]
            va_scr[0:lx, DA_V:2 * DA_V] = ones_col(lx)
        va_scr[lx:lx + lctx, 0:DA_V] = vc_ref[0]
        va_scr[lx:lx + lctx, DA_V:2 * DA_V] = ones_col(lctx)

    groups = ([(kx_ref, g * kg, kg, g * kg) for g in range(lx // kg)] if has_x else []) + [(kc_ref, 0, lctx, lx)]

    def step(sa_scr, ma_scr, aa_scr, sb_scr, mb_scr, ab_scr):
        q = q_ref[0]
        lane = lax.broadcasted_iota(jnp.int32, (tq, DA_V), 1)
        zero = jnp.zeros_like(q)
        qq = jnp.concatenate([jnp.where(lane < 64, q, zero), jnp.where(lane >= 64, q, zero)], axis=0)
        m128 = None
        for ref, r0, n, c0 in groups:
            s = _dot_nt(qq, ref[0, r0:r0 + n, :])
            sa_scr[:, c0:c0 + n] = s
            for c in range(n // LANES):
                blk = s[:, c * LANES:(c + 1) * LANES]
                m128 = blk if m128 is None else jnp.maximum(m128, blk)
        ma_scr[...] = jnp.broadcast_to(jnp.max(m128, axis=-1, keepdims=True), m128.shape)
        m_prev = mb_scr[...]
        acc = None
        for ref, r0, n, c0 in groups:
            m_rep = jnp.concatenate([m_prev] * (n // LANES), axis=1)
            p = jnp.exp(sb_scr[:, c0:c0 + n] - m_rep).astype(bf16)
            part = _dot(p, va_scr[c0:c0 + n, :])
            acc = part if acc is None else acc + part
        aa_scr[...] = acc
        acc = ab_scr[:, 0:DA_V]
        l = ab_scr[:, DA_V:DA_V + 1]
        lp = lp_ref[...]
        lam = (jnp.exp(jnp.sum(lp[0:1] * lp[1:2], axis=-1, keepdims=True))
               - jnp.exp(jnp.sum(lp[2:3] * lp[3:4], axis=-1, keepdims=True)) + lam_init)
        o = acc[0:tq] / l[0:tq] - lam * (acc[tq:2 * tq] / l[tq:2 * tq])
        ms = jnp.mean(o * o, axis=-1, keepdims=True)
        o_ref[0] = (o * lax.rsqrt(ms + EPS) * nw_ref[...] * (1.0 - lam_init)).astype(bf16)

    @pl.when(t % 2 == 0)
    def _():
        step(s0_scr, m0_scr, a0_scr, s1_scr, m1_scr, a1_scr)

    @pl.when(t % 2 == 1)
    def _():
        step(s1_scr, m1_scr, a1_scr, s0_scr, m0_scr, a0_scr)


def _attn(da_q, da_c, lam_p, nw, lam_init, tq):
    has_x = da_q is not da_c
    bsz, l, _ = da_q.shape
    lctx = da_c.shape[1]
    nq = l // tq
    nt = bsz * DA_H * nq
    lx = l if has_x else 0
    kg = _pick_tile(l, (1024, 512, 256, 128))

    def split(t):
        return t // (DA_H * nq), (t // nq) % DA_H, t % nq

    def cur(col0, whole):
        def index(t):
            b, h, i = split(jnp.minimum(t, nt - 1))
            return (b, 0 if whole else i, col0 + h)
        return index

    def prev(col0, whole, lag=1):
        def index(t):
            b, h, i = split(jnp.clip(t - lag, 0, nt - 1))
            return (b, 0 if whole else i, col0 + h)
        return index

    in_specs = [pl.BlockSpec((1, tq, DA_V), cur(0, False))]
    args = [da_q]
    if has_x:
        in_specs += [pl.BlockSpec((1, l, DA_V), cur(4, True))]
        args += [da_q]
    in_specs += [pl.BlockSpec((1, lctx, DA_V), cur(4, True))]
    args += [da_c]
    if has_x:
        in_specs += [pl.BlockSpec((1, l, DA_V), prev(8, True))]
        args += [da_q]
    in_specs += [
        pl.BlockSpec((1, lctx, DA_V), prev(8, True)),
        pl.BlockSpec((4, 64), lambda t: (0, 0)),
        pl.BlockSpec((1, DA_V), lambda t: (0, 0)),
    ]
    args += [da_c, lam_p, nw]
    return pl.pallas_call(
        functools.partial(_attn_kernel, has_x=has_x, lam_init=lam_init, kg=kg, nq=nq),
        grid=(nt + 2,),
        in_specs=in_specs,
        out_specs=pl.BlockSpec((1, tq, DA_V), prev(0, False, lag=2)),
        out_shape=jax.ShapeDtypeStruct((bsz, l, DA_H * DA_V), bf16),
        scratch_shapes=[
            pltpu.VMEM((2 * tq, lx + lctx), f32),
            pltpu.VMEM((2 * tq, lx + lctx), f32),
            pltpu.VMEM((2 * tq, LANES), f32),
            pltpu.VMEM((2 * tq, LANES), f32),
            pltpu.VMEM((2 * tq, 2 * DA_V), f32),
            pltpu.VMEM((2 * tq, 2 * DA_V), f32),
            pltpu.VMEM((lx + lctx, 2 * DA_V), bf16),
        ],
        compiler_params=_cparams(("arbitrary",)),
        name="attn_x" if has_x else "attn_c",
    )(*args)


def _head_ms(y, bd_b):
    return _dot2(y * y, bd_b) * (1.0 / 64.0)


def _outproj_kernel(x_ref, mod_ref, mlf_ref, mlb_ref, mlo_ref, hgf_ref, hgb_ref, hgg_ref, da_ref,
                    mlw_ref, hgw_ref, w_ref, nw_ref, o_ref):
    d = x_ref.shape[-1]
    rr = lax.broadcasted_iota(jnp.int32, (256, 256), 0) // 64
    cc = lax.broadcasted_iota(jnp.int32, (256, 256), 1) // 64
    bd_b = jnp.where(rr == cc, 1.0, 0.0).astype(bf16)
    ml = mlf_ref[0, 0] + mlb_ref[0, 0]
    ml = ml * lax.rsqrt(_head_ms(ml, bd_b) + EPS) * mlw_ref[...]
    ml = _sigmoid(mlo_ref[0].astype(f32)) * ml
    hg = hgf_ref[0, 0] + hgb_ref[0, 0]
    hg = hg * lax.rsqrt(_head_ms(hg, bd_b) + EPS) * hgw_ref[...]
    gg = hgg_ref[0].astype(f32)
    hg = hg * (gg * _sigmoid(gg))
    mix = (_dot(ml.astype(bf16), w_ref[0:256, :]) + _dot(hg.astype(bf16), w_ref[256:512, :])
           + _dot(da_ref[0], w_ref[512:1024, :]))
    ms = jnp.mean(mix * mix, axis=-1, keepdims=True)
    y = mix * lax.rsqrt(ms + EPS) * nw_ref[...]
    o_ref[0] = x_ref[0] + mod_ref[0][:, 2 * d:3 * d] * y


def _outproj(x, mod, mod_row0, per_batch, ml_out, ml, hg_out, hg, da_out, mlw, hgw, w, nw, tm, name):
    bsz, l, d = x.shape
    nt = l // tm
    mod_map = (lambda b, i: (mod_row0 + b, 0, 0)) if per_batch else (lambda b, i: (mod_row0, 0, 0))
    return pl.pallas_call(
        _outproj_kernel,
        grid=(bsz, nt),
        in_specs=[
            pl.BlockSpec((1, tm, d), lambda b, i: (b, i, 0)),
            pl.BlockSpec((1, 1, mod.shape[-1]), mod_map),
            pl.BlockSpec((1, 1, tm, 256), lambda b, i: (0, b, i, 0)),
            pl.BlockSpec((1, 1, tm, 256), lambda b, i: (1, b, i, 0)),
            pl.BlockSpec((1, tm, 256), lambda b, i: (b, i, 3)),
            pl.BlockSpec((1, 1, tm, 256), lambda b, i: (0, b, i, 0)),
            pl.BlockSpec((1, 1, tm, 256), lambda b, i: (1, b, i, 0)),
            pl.BlockSpec((1, tm, 256), lambda b, i: (b, i, 2)),
            pl.BlockSpec((1, tm, 512), lambda b, i: (b, i, 0)),
            pl.BlockSpec((1, 256), lambda b, i: (0, 0)),
            pl.BlockSpec((1, 256), lambda b, i: (0, 0)),
            pl.BlockSpec((d, d), lambda b, i: (0, 0)),
            pl.BlockSpec((1, d), lambda b, i: (0, 0)),
        ],
        out_specs=pl.BlockSpec((1, tm, d), lambda b, i: (b, i, 0)),
        out_shape=jax.ShapeDtypeStruct((bsz, l, d), f32),
        compiler_params=_cparams(("parallel", "parallel")),
        name=name,
    )(x, mod, ml_out, ml_out, ml, hg_out, hg_out, hg, da_out, mlw, hgw, w, nw)


def _ffn_kernel(*refs, moe):
    if moe:
        x_ref, mod_ref, nw1_ref, rw_ref, rb_ref, wg_ref, wu_ref, wd_ref, nw2_ref, o_ref, h_scr, acc_scr, gate_scr = refs
    else:
        x_ref, mod_ref, nw1_ref, wg_ref, wu_ref, wd_ref, nw2_ref, o_ref, h_scr, acc_scr = refs
    e = pl.program_id(2)
    j = pl.program_id(3)
    ne = pl.num_programs(2)
    nj = pl.num_programs(3)
    d = x_ref.shape[-1]
    tm = x_ref.shape[1]

    @pl.when((e == 0) & (j == 0))
    def _():
        x = x_ref[0]
        ms = jnp.mean(x * x, axis=-1, keepdims=True)
        h = x * lax.rsqrt(ms + EPS) * nw1_ref[...]
        mod = mod_ref[0]
        h = h * (1.0 + mod[:, 4 * d:5 * d]) + mod[:, 3 * d:4 * d]
        h_scr[...] = h.astype(bf16)
        acc_scr[...] = jnp.zeros_like(acc_scr)
        if moe:
            hh, hl = _split_hi_lo(h)
            rw = rw_ref[...]
            wh, wl = _split_hi_lo(rw)
            logits = _dot(hh, wh) + _dot(hl, wh) + _dot(hh, wl) + rb_ref[...]
            lane = lax.broadcasted_iota(jnp.int32, (tm, LANES), 1)
            real = lane < N_EXPERTS
            lg = jnp.where(real, logits, MASK_NEG)
            m1 = jnp.max(lg, axis=-1, keepdims=True)
            i1 = jnp.min(jnp.where(lg == m1, lane, LANES), axis=-1, keepdims=True)
            lg2 = jnp.where(lane == i1, MASK_NEG, lg)
            m2 = jnp.max(lg2, axis=-1, keepdims=True)
            i2 = jnp.min(jnp.where(lg2 == m2, lane, LANES), axis=-1, keepdims=True)
            e2 = jnp.exp(m2 - m1)
            g1 = 1.0 / (1.0 + e2)
            g2 = e2 / (1.0 + e2)
            gate_scr[...] = jnp.where(lane == i1, g1, 0.0) + jnp.where(lane == i2, g2, 0.0)

    hb = h_scr[...]
    a = _dot(hb, wg_ref[0])
    u = _dot(hb, wu_ref[0])
    act = (a * _sigmoid(a)) * u
    if moe:
        lane = lax.broadcasted_iota(jnp.int32, (tm, LANES), 1)
        ge = jnp.sum(jnp.where(lane == e, gate_scr[...], 0.0), axis=-1, keepdims=True)
        act = act * ge
    acc_scr[...] += _dot(act.astype(bf16), wd_ref[0])

    @pl.when((e == ne - 1) & (j == nj - 1))
    def _():
        y = acc_scr[...]
        ms = jnp.mean(y * y, axis=-1, keepdims=True)
        y = y * lax.rsqrt(ms + EPS) * nw2_ref[...]
        o_ref[0] = x_ref[0] + mod_ref[0][:, 5 * d:6 * d] * y


def _ffn(x, mod, mod_row0, per_batch, nw1, wg, wu, wd, nw2, tm, tf, router=None, name="ffn"):
    bsz, l, d = x.shape
    ne, _, ff = wg.shape
    nt = l // tm
    nf = ff // tf
    moe = router is not None
    mod_map = (lambda b, i, e, j: (mod_row0 + b, 0, 0)) if per_batch else (lambda b, i, e, j: (mod_row0, 0, 0))
    in_specs = [
        pl.BlockSpec((1, tm, d), lambda b, i, e, j: (b, i, 0)),
        pl.BlockSpec((1, 1, mod.shape[-1]), mod_map),
        pl.BlockSpec((1, d), lambda b, i, e, j: (0, 0)),
    ]
    args = [x, mod, nw1]
    scratch = [pltpu.VMEM((tm, d), bf16), pltpu.VMEM((tm, d), f32)]
    if moe:
        in_specs += [
            pl.BlockSpec((d, LANES), lambda b, i, e, j: (0, 0)),
            pl.BlockSpec((1, LANES), lambda b, i, e, j: (0, 0)),
        ]
        args += list(router)
        scratch += [pltpu.VMEM((tm, LANES), f32)]
    in_specs += [
        pl.BlockSpec((1, d, tf), lambda b, i, e, j: (e, 0, j)),
        pl.BlockSpec((1, d, tf), lambda b, i, e, j: (e, 0, j)),
        pl.BlockSpec((1, tf, d), lambda b, i, e, j: (e, j, 0)),
        pl.BlockSpec((1, d), lambda b, i, e, j: (0, 0)),
    ]
    args += [wg, wu, wd, nw2]
    return pl.pallas_call(
        functools.partial(_ffn_kernel, moe=moe),
        grid=(bsz, nt, ne, nf),
        in_specs=in_specs,
        out_specs=pl.BlockSpec((1, tm, d), lambda b, i, e, j: (b, i, 0)),
        out_shape=jax.ShapeDtypeStruct((bsz, l, d), f32),
        scratch_shapes=scratch,
        compiler_params=_cparams(("parallel", "parallel", "arbitrary", "arbitrary")),
        name=name,
    )(*args)


MOE_TM = 512
MOE_TS = 512
META_I1, META_I2, META_R1, META_R2, META_G1, META_G2 = range(6)


def _router_kernel(x_ref, mod_ref, nw_ref, rw_ref, rb_ref, tri_ref, h_ref, meta_ref, cnt_ref, base_scr):
    d = x_ref.shape[-1]
    tm = x_ref.shape[1]

    @pl.when((pl.program_id(0) == 0) & (pl.program_id(1) == 0))
    def _():
        base_scr[...] = jnp.zeros_like(base_scr)

    x = x_ref[0]
    ms = jnp.mean(x * x, axis=-1, keepdims=True)
    h = x * lax.rsqrt(ms + EPS) * nw_ref[...]
    mod = mod_ref[0]
    h = h * (1.0 + mod[:, 4 * d:5 * d]) + mod[:, 3 * d:4 * d]
    h_ref[0] = h
    hh, hl = _split_hi_lo(h)
    wh, wl = _split_hi_lo(rw_ref[...])
    logits = _dot(hh, wh) + _dot(hl, wh) + _dot(hh, wl) + rb_ref[...]
    lane = lax.broadcasted_iota(jnp.int32, (tm, LANES), 1)
    lg = jnp.where(lane < N_EXPERTS, logits, MASK_NEG)
    m1 = jnp.max(lg, axis=-1, keepdims=True)
    i1 = jnp.min(jnp.where(lg == m1, lane, LANES), axis=-1, keepdims=True)
    lg2 = jnp.where(lane == i1, MASK_NEG, lg)
    m2 = jnp.max(lg2, axis=-1, keepdims=True)
    i2 = jnp.min(jnp.where(lg2 == m2, lane, LANES), axis=-1, keepdims=True)
    e2 = jnp.exp(m2 - m1)
    g1 = 1.0 / (1.0 + e2)
    g2 = e2 / (1.0 + e2)
    oh = jnp.where((lane == i1) | (lane == i2), 1.0, 0.0)
    pos = base_scr[...] + _dot(tri_ref[...], oh.astype(bf16))
    r1 = jnp.sum(jnp.where(lane == i1, pos, 0.0), axis=-1, keepdims=True)
    r2 = jnp.sum(jnp.where(lane == i2, pos, 0.0), axis=-1, keepdims=True)
    meta = jnp.zeros((tm, LANES), f32)
    for k, val in ((META_I1, i1.astype(f32)), (META_I2, i2.astype(f32)), (META_R1, r1), (META_R2, r2),
                   (META_G1, g1), (META_G2, g2)):
        meta = jnp.where(lane == k, val, meta)
    meta_ref[0] = meta
    new_base = pos[tm - 1:tm, :] + oh[tm - 1:tm, :]
    base_scr[...] = new_base
    cnt_ref[...] = jnp.broadcast_to(new_base, cnt_ref.shape)


def _moe_route(x, mod, mod_row0, nw1, rw, rb, tm):
    bsz, l, d = x.shape
    nt = l // tm
    r = jnp.arange(tm)
    tri = (r[:, None] > r[None, :]).astype(bf16)
    return pl.pallas_call(
        _router_kernel,
        grid=(bsz, nt),
        in_specs=[
            pl.BlockSpec((1, tm, d), lambda b, i: (b, i, 0)),
            pl.BlockSpec((1, 1, mod.shape[-1]), lambda b, i: (mod_row0 + b, 0, 0)),
            pl.BlockSpec((1, d), lambda b, i: (0, 0)),
            pl.BlockSpec((d, LANES), lambda b, i: (0, 0)),
            pl.BlockSpec((1, LANES), lambda b, i: (0, 0)),
            pl.BlockSpec((tm, tm), lambda b, i: (0, 0)),
        ],
        out_specs=[
            pl.BlockSpec((1, tm, d), lambda b, i: (b, i, 0)),
            pl.BlockSpec((1, tm, LANES), lambda b, i: (b, i, 0)),
            pl.BlockSpec((8, LANES), lambda b, i: (0, 0)),
        ],
        out_shape=[
            jax.ShapeDtypeStruct((bsz, l, d), f32),
            jax.ShapeDtypeStruct((bsz, l, LANES), f32),
            jax.ShapeDtypeStruct((8, LANES), f32),
        ],
        scratch_shapes=[pltpu.VMEM((1, LANES), f32)],
        compiler_params=_cparams(("arbitrary", "arbitrary")),
        name="moe_route",
    )(x, mod, nw1, rw, rb, tri)


def _row_copy(src_ref, src_row, dst_ref, dst_row, sem):
    return pltpu.make_async_copy(src_ref.at[pl.ds(src_row, 1)], dst_ref.at[pl.ds(dst_row, 1)], sem)


def _scatter_kernel(dest_ref, h_ref, xs_in_ref, xs_ref, sem):
    del xs_in_ref
    i = pl.program_id(0)
    ts = h_ref.shape[0]
    base = i * (2 * ts)

    def issue(r, carry):
        _row_copy(h_ref, r, xs_ref, dest_ref[base + r], sem).start(priority=0)
        _row_copy(h_ref, r, xs_ref, dest_ref[base + ts + r], sem).start(priority=1)
        return carry

    lax.fori_loop(0, ts, issue, 0, unroll=8)

    def drain(r, carry):
        _row_copy(h_ref, 0, xs_ref, 0, sem).wait()
        _row_copy(h_ref, 0, xs_ref, 0, sem).wait()
        return carry

    lax.fori_loop(0, ts, drain, 0, unroll=8)


def _moe_scatter(h2, dest, n_rows, ts):
    t, d = h2.shape
    xs0 = jnp.zeros((n_rows, d), f32)
    return pl.pallas_call(
        _scatter_kernel,
        grid_spec=pltpu.PrefetchScalarGridSpec(
            num_scalar_prefetch=1,
            grid=(t // ts,),
            in_specs=[
                pl.BlockSpec((ts, d), lambda i, dest: (i, 0)),
                pl.BlockSpec(memory_space=pl.ANY),
            ],
            out_specs=pl.BlockSpec(memory_space=pl.ANY),
            scratch_shapes=[pltpu.SemaphoreType.DMA],
        ),
        out_shape=jax.ShapeDtypeStruct((n_rows, d), f32),
        input_output_aliases={2: 0},
        compiler_params=_cparams(("arbitrary",)),
        name="moe_scatter",
    )(dest, h2, xs0)


def _expert_kernel(te_ref, tv_ref, x_ref, wg_ref, wu_ref, wd_ref, o_ref, acc_scr):
    del te_ref
    j = pl.program_id(0)
    f = pl.program_id(1)
    nf = pl.num_programs(1)
    valid = tv_ref[j] == 1

    @pl.when(valid)
    def _():
        xb = x_ref[...].astype(bf16)
        a = _dot(xb, wg_ref[0])
        u = _dot(xb, wu_ref[0])
        part = _dot(((a * _sigmoid(a)) * u).astype(bf16), wd_ref[0])

        @pl.when(f == 0)
        def _():
            acc_scr[...] = part

        @pl.when(f > 0)
        def _():
            acc_scr[...] += part

        @pl.when(f == nf - 1)
        def _():
            o_ref[...] = acc_scr[...]

    @pl.when(jnp.logical_not(valid))
    def _():
        o_ref[...] = jnp.zeros_like(o_ref)


def _moe_experts(xs, te, tv, wg, wu, wd, tm, tf):
    n_rows, d = xs.shape
    ff = wg.shape[-1]
    nf = ff // tf
    n_tiles = te.shape[0]

    def fsel(j, f, tv):
        return jnp.where(tv[j] == 1, f, nf - 1)

    return pl.pallas_call(
        _expert_kernel,
        grid_spec=pltpu.PrefetchScalarGridSpec(
            num_scalar_prefetch=2,
            grid=(n_tiles, nf),
            in_specs=[
                pl.BlockSpec((tm, d), lambda j, f, te, tv: (j, 0)),
                pl.BlockSpec((1, d, tf), lambda j, f, te, tv: (te[j], 0, fsel(j, f, tv))),
                pl.BlockSpec((1, d, tf), lambda j, f, te, tv: (te[j], 0, fsel(j, f, tv))),
                pl.BlockSpec((1, tf, d), lambda j, f, te, tv: (te[j], fsel(j, f, tv), 0)),
            ],
            out_specs=pl.BlockSpec((tm, d), lambda j, f, te, tv: (j, 0)),
            scratch_shapes=[pltpu.VMEM((tm, d), f32)],
        ),
        out_shape=jax.ShapeDtypeStruct((n_rows, d), f32),
        compiler_params=_cparams(("arbitrary", "arbitrary")),
        name="moe_experts",
    )(te, tv, xs, wg, wu, wd)


def _combine_kernel(dest_ref, x_ref, mod_ref, meta_ref, nw_ref, ys_ref, o_ref, y1_scr, y2_scr, sem):
    i = pl.program_id(0)
    ts = x_ref.shape[0]
    d = x_ref.shape[1]
    base = i * (2 * ts)

    def issue(r, carry):
        _row_copy(ys_ref, dest_ref[base + r], y1_scr, r, sem).start(priority=0)
        _row_copy(ys_ref, dest_ref[base + ts + r], y2_scr, r, sem).start(priority=1)
        return carry

    lax.fori_loop(0, ts, issue, 0, unroll=8)

    def drain(r, carry):
        _row_copy(ys_ref, 0, y1_scr, 0, sem).wait()
        _row_copy(ys_ref, 0, y2_scr, 0, sem).wait()
        return carry

    lax.fori_loop(0, ts, drain, 0, unroll=8)

    meta = meta_ref[...]
    y = meta[:, META_G1:META_G1 + 1] * y1_scr[...] + meta[:, META_G2:META_G2 + 1] * y2_scr[...]
    ms = jnp.mean(y * y, axis=-1, keepdims=True)
    y = y * lax.rsqrt(ms + EPS) * nw_ref[...]
    o_ref[...] = x_ref[...] + mod_ref[0][:, 5 * d:6 * d] * y


def _moe_combine(x2, mod, mod_row0, tokens_per_batch, meta2, nw2, ys, dest, ts):
    t, d = x2.shape
    per_b = tokens_per_batch // ts
    return pl.pallas_call(
        _combine_kernel,
        grid_spec=pltpu.PrefetchScalarGridSpec(
            num_scalar_prefetch=1,
            grid=(t // ts,),
            in_specs=[
                pl.BlockSpec((ts, d), lambda i, dest: (i, 0)),
                pl.BlockSpec((1, 1, mod.shape[-1]), lambda i, dest: (mod_row0 + i // per_b, 0, 0)),
                pl.BlockSpec((ts, LANES), lambda i, dest: (i, 0)),
                pl.BlockSpec((1, d), lambda i, dest: (0, 0)),
                pl.BlockSpec(memory_space=pl.ANY),
            ],
            out_specs=pl.BlockSpec((ts, d), lambda i, dest: (i, 0)),
            scratch_shapes=[pltpu.VMEM((ts, d), f32), pltpu.VMEM((ts, d), f32), pltpu.SemaphoreType.DMA],
        ),
        out_shape=jax.ShapeDtypeStruct((t, d), f32),
        compiler_params=_cparams(("arbitrary",)),
        name="moe_combine",
    )(dest, x2, mod, meta2, nw2, ys)


def _moe_sparse(x, mod, mod_row0, nw1, rw, rb, wg, wu, wd, nw2):
    bsz, l, d = x.shape
    t = bsz * l
    ne = wg.shape[0]
    tm_r = _pick_tile(l, (1024, 512, 256, 128))
    tm = min(MOE_TM, t)
    ts = _pick_tile(l, (MOE_TS, 256, 128))
    tf = _pick_tile(wg.shape[-1], (1792, 896, 512, 256, 128))

    h, meta, cnt = _moe_route(x, mod, mod_row0, nw1, rw, rb, tm_r)
    meta2 = meta.reshape(t, LANES)
    counts = cnt[0, :ne].astype(jnp.int32)
    tiles_e = (counts + tm - 1) // tm
    ends = jnp.cumsum(tiles_e)
    offset = (ends - tiles_e) * tm
    n_tiles = (2 * t) // tm + ne
    i1 = meta2[:, META_I1].astype(jnp.int32)
    i2 = meta2[:, META_I2].astype(jnp.int32)
    dest1 = jnp.take(offset, i1) + meta2[:, META_R1].astype(jnp.int32)
    dest2 = jnp.take(offset, i2) + meta2[:, META_R2].astype(jnp.int32)
    dest = jnp.stack([dest1.reshape(t // ts, ts), dest2.reshape(t // ts, ts)], axis=1).reshape(-1)
    jt = jnp.arange(n_tiles, dtype=jnp.int32)
    total = ends[ne - 1]
    tv = (jt < total).astype(jnp.int32)
    te_all = jnp.minimum(jnp.sum((jt[:, None] >= ends[None, :]).astype(jnp.int32), axis=1), ne - 1)
    te_last = jnp.take(te_all, jnp.maximum(total - 1, 0))
    te = jnp.where(tv == 1, te_all, te_last)
    n_rows = n_tiles * tm

    xs = _moe_scatter(h.reshape(t, d), dest, n_rows, ts)
    ys = _moe_experts(xs, te, tv, wg, wu, wd, tm, tf)
    out = _moe_combine(x.reshape(t, d), mod, mod_row0, l, meta2, nw2, ys, dest, ts)
    return out.reshape(bsz, l, d)


def _pick_tile(n, prefs):
    for t in prefs:
        if n % t == 0:
            return t
    return n


def kernel(x, c, ctx, c_ctx, w_mod, b_mod, norm_pre_mix, norm_post_mix, norm_pre_ffn, norm_post_ffn, w_in, w_out, ml_gate_b, ml_norm, hg_lb_logits, hg_norm, da_lambda, da_norm, ffn_w_gate, ffn_w_up, ffn_w_down, router_w, router_b, moe_w_gate, moe_w_up, moe_w_down):
    depth = w_mod.shape[0]
    bsz, l, d = x.shape
    lctx = ctx.shape[1]

    mod = _mod_vectors(c, c_ctx, w_mod, b_mod)
    rows = mod.shape[1]
    mod = mod.reshape(depth * rows, 1, 6 * d)

    lb_soft = jax.nn.softmax(hg_lb_logits.astype(f32), axis=0)
    lower_bounds = jnp.cumsum(lb_soft, axis=0) - lb_soft[0]

    tables = _rope_tables(l)
    ml_lc = min(ML_CHUNK, lctx)
    tri_ml_x = _tri_pair(min(ML_CHUNK, l), min(ML_CHUNK, l))
    tri_ml_c = _tri_pair(ml_lc, ml_lc)
    tri_hg_x = _tri_pair(min(HG_GROUP, l), HG_CHUNK)
    tri_hg_c = _tri_pair(min(HG_GROUP, lctx), HG_CHUNK)
    hg_bias = _hgrn2_bias()
    ml_sel = _mlstm_sel()

    tm_x = _pick_tile(l, (512, 256, 128))
    tm_c = _pick_tile(lctx, (256, 128))
    tq = _pick_tile(l, (512, 256, 128))

    for layer in range(depth):
        need_ctx = layer < depth - 1
        lam_init = 0.8 - 0.6 * math.exp(-0.3 * layer)
        row0 = layer * rows
        w_in_l = _prep_w_in(w_in[layer])
        w_out_l = w_out[layer].astype(bf16)
        nw_pre = norm_pre_mix[layer].reshape(1, d)

        pc = _inproj(ctx, mod, row0 + bsz, False, nw_pre, w_in_l, None, tm_c)
        px = _inproj(x, mod, row0, True, nw_pre, w_in_l, tables, tm_x)
        ml_c, mlg_c, hg_c, hgf_c, da_c = pc
        ml_x, mlg_x, hg_x, hgf_x, da_x = px

        gb = ml_gate_b[layer].astype(f32)
        z = jnp.zeros((LANES - 8,), f32)
        gate_b2 = jnp.stack([jnp.concatenate([gb[0:8], z]), jnp.concatenate([gb[8:16], z])]).reshape(2, 1, LANES)
        s0 = (jnp.zeros((2, bsz, ML_H, LANES, 2 * LANES), f32), jnp.zeros((2, bsz, ML_H, 8, LANES), f32))
        mlo_c, s_ml = _mlstm(ml_c, mlg_c, gate_b2, tri_ml_c, ml_sel, s0)
        mlo_x, _ = _mlstm(ml_x, mlg_x, gate_b2, tri_ml_x, ml_sel, s_ml)

        lb = lower_bounds[layer]
        lbc = jnp.zeros((8, HG_W), f32)
        lbc = lbc.at[0].set(jnp.log(jnp.maximum(lb, LB_FLOOR))).at[1].set(jnp.log1p(-lb)).at[2].set(1.0 - lb)
        h0 = jnp.zeros((2, bsz, HG_W, HG_W // 2), f32)
        hgo_c, s_hg = _hgrn2(hg_c, hgf_c, lbc, tri_hg_c, hg_bias, h0)
        hgo_x, _ = _hgrn2(hg_x, hgf_x, lbc, tri_hg_x, hg_bias, s_hg)

        lam_p = da_lambda[layer].astype(f32)
        da_nw = da_norm[layer].reshape(1, DA_V)
        dao_x = _attn(da_x, da_c, lam_p, da_nw, lam_init, tq)

        mlw = ml_norm[layer].reshape(1, 256)
        hgw = hg_norm[layer].reshape(1, 256)
        nw_post = norm_post_mix[layer].reshape(1, d)
        x = _outproj(x, mod, row0, True, mlo_x, ml_x, hgo_x, hg_x, dao_x, mlw, hgw, w_out_l, nw_post, tm_x, "outproj_x")
        if need_ctx:
            dao_c = _attn(da_c, da_c, lam_p, da_nw, lam_init, lctx)
            ctx = _outproj(ctx, mod, row0 + bsz, False, mlo_c, ml_c, hgo_c, hg_c, dao_c, mlw, hgw, w_out_l, nw_post, tm_c, "outproj_c")

        i = layer // 2
        nw1 = norm_pre_ffn[layer].reshape(1, d)
        nw2 = norm_post_ffn[layer].reshape(1, d)
        if layer % 2 == 0:
            wg = ffn_w_gate[i].astype(bf16)[None]
            wu = ffn_w_up[i].astype(bf16)[None]
            wd = ffn_w_down[i].astype(bf16)[None]
            router = None
        else:
            wg = moe_w_gate[i].astype(bf16)
            wu = moe_w_up[i].astype(bf16)
            wd = moe_w_down[i].astype(bf16)
            rw = jnp.zeros((d, LANES), f32).at[:, :N_EXPERTS].set(router_w[i])
            rb = jnp.zeros((1, LANES), f32).at[0, :N_EXPERTS].set(router_b[i])
            router = (rw, rb)
        ff = wg.shape[-1]
        tf = _pick_tile(ff, (896, 1408, 512, 256))
        tm_f = _pick_tile(l, (1024, 512, 256, 128) if tf <= 896 else (512, 256, 128))
        if router is None:
            x = _ffn(x, mod, row0, True, nw1, wg, wu, wd, nw2, tm_f, tf, None, name="ffn_x%d" % layer)
        else:
            x = _moe_sparse(x, mod, row0, nw1, rw, rb, wg, wu, wd, nw2)
        if need_ctx:
            ctx = _ffn(ctx, mod, row0 + bsz, False, nw1, wg, wu, wd, nw2, tm_c, tf, router, name="ffn_c%d" % layer)
    return x
```

```python
import functools
import math

import numpy as np
import jax
import jax.numpy as jnp
from jax import lax
from jax.experimental import pallas as pl
from jax.experimental.pallas import tpu as pltpu

f32 = jnp.float32
bf16 = jnp.bfloat16

ML_H = 4
HG_H = 4
DA_H = 4
GRID_W = 64
ROPE_BASE = 10000.0
N_EXPERTS = 8
EPS = 1e-6
MASK_NEG = -1e30
LB_FLOOR = 1e-30

LANES = 128
ML_CHUNK = 256
HG_CHUNK = 16
HG_GROUP = 512
VMEM_LIMIT = 56 * 1024 * 1024


def _cparams(sem):
    return pltpu.CompilerParams(dimension_semantics=sem, vmem_limit_bytes=VMEM_LIMIT)


def _sigmoid(x):
    return 1.0 / (1.0 + jnp.exp(-x))


def _log_sigmoid(x):
    return jnp.minimum(x, 0.0) - jnp.log(1.0 + jnp.exp(-jnp.abs(x)))


def _split_hi_lo(a):
    hi = a.astype(bf16)
    lo = (a - hi.astype(f32)).astype(bf16)
    return hi, lo


def _dot(a, b):
    return jnp.dot(a, b, preferred_element_type=f32)


def _dot_nt(a, b):
    return lax.dot_general(a, b, (((1,), (1,)), ((), ())), preferred_element_type=f32)


def _dot_tn(a, b):
    return lax.dot_general(a, b, (((0,), (0,)), ((), ())), preferred_element_type=f32)


def _dot2(a_f32, b_bf16):
    hi, lo = _split_hi_lo(a_f32)
    return _dot(hi, b_bf16) + _dot(lo, b_bf16)


def _mod_kernel(s_ref, w_ref, b_ref, o_ref):
    s = s_ref[...]
    s = s * _sigmoid(s)
    o_ref[0] = _dot(s.astype(bf16), w_ref[0].astype(bf16)) + b_ref[0]


def _mod_vectors(c, c_ctx, w_mod, b_mod):
    depth, d, n = w_mod.shape
    bsz = c.shape[0]
    rows = 8
    s = jnp.zeros((rows, d), f32).at[:bsz].set(c).at[bsz].set(c_ctx)
    tn = n // 4
    out = pl.pallas_call(
        _mod_kernel,
        grid=(depth, n // tn),
        in_specs=[
            pl.BlockSpec((rows, d), lambda l, j: (0, 0)),
            pl.BlockSpec((1, d, tn), lambda l, j: (l, 0, j)),
            pl.BlockSpec((1, 1, tn), lambda l, j: (l, 0, j)),
        ],
        out_specs=pl.BlockSpec((1, rows, tn), lambda l, j: (l, 0, j)),
        out_shape=jax.ShapeDtypeStruct((depth, rows, n), f32),
        compiler_params=_cparams(("parallel", "parallel")),
        name="mod_vectors",
    )(s, w_mod, b_mod.reshape(depth, 1, n))
    return out


P_ML = (0, 1024)
P_MLG = (1024, 1280)
P_HG = (1280, 2048)
P_HGF = (2048, 2560)
P_DA = (2560, 4096)
P_TOTAL = 4096


def _prep_w_in(w):
    d = w.shape[0]
    ml = w[:, 0:1024]
    g = w[:, 1024:1040]
    z = jnp.zeros((d, LANES - 8), w.dtype)
    gates = jnp.concatenate([g[:, 0:8], z, g[:, 8:16], z], axis=1)
    hg_q = w[:, 1040:1296]
    hg_f = w[:, 1296:1808]
    hg_i = w[:, 1808:2064]
    hg_g = w[:, 2064:2320]
    da = w[:, 2320:3856]
    return jnp.concatenate([ml, gates, hg_q, hg_i, hg_g, hg_f, da], axis=1).astype(bf16)


def _inproj_kernel(*refs, rope):
    if rope:
        x_ref, mod_ref, nw_ref, w_ref, cos_ref, sin_ref = refs[:6]
        outs = refs[6:]
    else:
        x_ref, mod_ref, nw_ref, w_ref = refs[:4]
        outs = refs[4:]
    ml_ref, mlg_ref, hg_ref, hgf_ref, da_ref = outs
    d = x_ref.shape[-1]
    x = x_ref[0]
    ms = jnp.mean(x * x, axis=-1, keepdims=True)
    h = x * lax.rsqrt(ms + EPS) * nw_ref[...]
    mod = mod_ref[0]
    h = h * (1.0 + mod[:, d:2 * d]) + mod[:, 0:d]
    hb = h.astype(bf16)

    def proj(lo, hi):
        return _dot(hb, w_ref[:, lo:hi])

    ml_ref[0] = proj(*P_ML).astype(bf16)
    mlg_ref[0] = proj(*P_MLG)
    pq = proj(P_HG[0], P_HG[0] + 256)
    hg_ref[0, :, 0:256] = (pq * _sigmoid(pq)).astype(bf16)
    hg_ref[0, :, 256:768] = proj(P_HG[0] + 256, P_HG[1]).astype(bf16)
    hgf_ref[0] = proj(*P_HGF)
    q = proj(P_DA[0], P_DA[0] + 512)
    k = proj(P_DA[0] + 512, P_DA[0] + 1024)
    if rope:
        cos = jnp.concatenate([cos_ref[...]] * 4, axis=1)
        sin = jnp.concatenate([sin_ref[...]] * 4, axis=1)
        lane = lax.broadcasted_iota(jnp.int32, (1, 512), 1)
        first = (lane % 32) < 16

        def rot(t):
            sw = jnp.where(first, pltpu.roll(t, 512 - 16, 1), pltpu.roll(t, 16, 1))
            return t * cos + sw * sin

        q = rot(q)
        k = rot(k)
    da_ref[0, :, 0:512] = (q * 0.125).astype(bf16)
    da_ref[0, :, 512:1024] = k.astype(bf16)
    da_ref[0, :, 1024:1536] = proj(P_DA[0] + 1024, P_DA[1]).astype(bf16)


def _inproj(x, mod, mod_row0, per_batch, nw, w, tables, tm):
    bsz, l, d = x.shape
    rope = tables is not None
    nt = l // tm
    mod_map = (lambda b, i: (mod_row0 + b, 0, 0)) if per_batch else (lambda b, i: (mod_row0, 0, 0))
    in_specs = [
        pl.BlockSpec((1, tm, d), lambda b, i: (b, i, 0)),
        pl.BlockSpec((1, 1, mod.shape[-1]), mod_map),
        pl.BlockSpec((1, d), lambda b, i: (0, 0)),
        pl.BlockSpec((d, P_TOTAL), lambda b, i: (0, 0)),
    ]
    args = [x, mod, nw, w]
    if rope:
        in_specs += [pl.BlockSpec((tm, LANES), lambda b, i: (i, 0))] * 2
        args += list(tables)
    widths = (1024, 256, 768, 512, 1536)
    dtypes = (bf16, f32, bf16, f32, bf16)
    out_specs = [pl.BlockSpec((1, tm, wd), lambda b, i: (b, i, 0)) for wd in widths]
    out_shape = [jax.ShapeDtypeStruct((bsz, l, wd), dt) for wd, dt in zip(widths, dtypes)]
    return pl.pallas_call(
        functools.partial(_inproj_kernel, rope=rope),
        grid=(bsz, nt),
        in_specs=in_specs,
        out_specs=out_specs,
        out_shape=out_shape,
        compiler_params=_cparams(("parallel", "parallel")),
        name="inproj_x" if rope else "inproj_c",
    )(*args)


def _rope_tables(l):
    rows = l // GRID_W
    row = jnp.repeat(jnp.arange(rows), GRID_W).astype(f32)
    col = jnp.tile(jnp.arange(GRID_W), rows).astype(f32)
    n_freq = 16
    inv = ROPE_BASE ** (-jnp.arange(n_freq, dtype=f32) / n_freq)
    lane = jnp.arange(LANES)
    freq = inv[lane % n_freq]
    pos = jnp.where(((lane % 64) // 32 == 0)[None, :], row[:, None], col[:, None])
    ang = pos * freq[None, :]
    sign = jnp.where((lane % 32) < 16, -1.0, 1.0)[None, :]
    return jnp.cos(ang), jnp.sin(ang) * sign


ML_D = 64


def _mlstm_kernel(q_ref, k_ref, v_ref, g_ref, gb_ref, tri_ref, sel_ref, cn0_ref, m0_ref,
                  o_ref, cnf_ref, mf_ref, cn_scr, m_scr):
    d = pl.program_id(0)
    j = pl.program_id(2)
    nj = pl.num_programs(2)
    lc = q_ref.shape[1]
    nrep = lc // LANES

    @pl.when(j == 0)
    def _():
        cn_scr[...] = cn0_ref[0, 0]
        m_scr[...] = m0_ref[0, 0]

    g = g_ref[0] + gb_ref[0]
    lf = _log_sigmoid(g)
    hi, lo = _split_hi_lo(lf)
    bc = _dot(tri_ref[0], hi) + _dot(tri_ref[0], lo)
    z = pltpu.roll(g, 4, 1) - bc
    zt = z.T
    sel = sel_ref[...]
    yrep = _dot2(bc, sel)
    zrep = _dot2(z, sel)
    fwd = d == 0
    r = lax.broadcasted_iota(jnp.int32, (lc, lc), 0)
    c = lax.broadcasted_iota(jnp.int32, (lc, lc), 1)
    mask = (r - c) * (1 - 2 * d) >= 0
    lane = lax.broadcasted_iota(jnp.int32, (lc, LANES), 1)
    ones_b = jnp.ones((lc, LANES), bf16)

    def rep(x, n):
        return jnp.concatenate([x] * n, axis=1)

    heads = []
    for h in range(ML_H):
        p, hh = divmod(h, 2)
        half = (lane < ML_D) if hh == 0 else (lane >= ML_D)
        qp = q_ref[0, :, p * LANES:(p + 1) * LANES]
        kp = k_ref[0, :, p * LANES:(p + 1) * LANES] * jnp.asarray(0.125, bf16)
        vp = v_ref[0, :, p * LANES:(p + 1) * LANES]
        b_rep = yrep[:, h * LANES:(h + 1) * LANES]
        ab_rep = zrep[:, h * LANES:(h + 1) * LANES]
        blast = jnp.where(fwd, b_rep[lc - 1:lc, :], b_rep[0:1, :])
        m0 = m_scr[h, 0:1, :]
        qm = jnp.where(half, qp, jnp.zeros_like(qp))
        vo = jnp.concatenate([jnp.where(half, vp, jnp.zeros_like(vp)), ones_b], axis=1)
        cn = cn_scr[h]
        s_raw = _dot_nt(qm, kp)
        nd_inter = _dot(qm, cn.astype(bf16))
        a_rep = blast + ab_rep
        m_loc = jnp.max(a_rep, axis=0, keepdims=True)
        w = jnp.exp(a_rep - m_loc)
        kw = jnp.where(half, kp.astype(f32) * w, 0.0).astype(bf16)
        dcn = _dot_tn(kw, vo)
        heads.append((b_rep, blast, m0, vo, cn, s_raw, nd_inter, m_loc, dcn))

    outs = []
    for h in range(ML_H):
        b_rep, blast, m0, vo, cn, s_raw, nd_inter, m_loc, dcn = heads[h]
        z_row = zt[4 + h:5 + h, :]
        dm = jnp.where(mask, rep(b_rep, nrep) + z_row, MASK_NEG)
        inter = b_rep + m0
        m = jnp.maximum(inter, jnp.max(dm, axis=-1, keepdims=True))
        w_intra = jnp.exp(dm - rep(m, nrep))
        w_inter = jnp.exp(inter - m)
        nd = _dot((s_raw * w_intra).astype(bf16), vo) + rep(w_inter, 2) * nd_inter
        den = nd[:, LANES:2 * LANES]
        outs.append(nd[:, 0:LANES] / jnp.maximum(jnp.abs(den), jnp.exp(-m)))

        m_new = jnp.maximum(blast + m0, m_loc)
        s_old = jnp.exp(blast + m0 - m_new)
        s_new = jnp.exp(m_loc - m_new)
        cn_scr[h] = rep(s_old, 2) * cn + rep(s_new, 2) * dcn
        m_scr[h] = jnp.broadcast_to(m_new, (8, LANES))
    for p in range(ML_H // 2):
        o_ref[0, 0, :, p * LANES:(p + 1) * LANES] = outs[2 * p] + outs[2 * p + 1]

    @pl.when(j == nj - 1)
    def _():
        cnf_ref[0, 0] = cn_scr[...]
        mf_ref[0, 0] = m_scr[...]


def _mlstm_sel():
    sel = np.zeros((LANES, ML_H * LANES), np.float32)
    for h in range(ML_H):
        sel[4 + h, h * LANES:(h + 1) * LANES] = 1.0
    return jnp.asarray(sel, dtype=bf16)


def _mlstm(ml, mlg, gate_b2, tri, sel, state):
    bsz, l, _ = ml.shape
    lc = min(ML_CHUNK, l)
    nj = l // lc
    cn0, m0 = state

    def tmap(col):
        return lambda d, b, j: (b, jnp.where(d == 0, j, nj - 1 - j), col)

    cn_spec = pl.BlockSpec((1, 1, ML_H, LANES, 2 * LANES), lambda d, b, j: (d, b, 0, 0, 0))
    m_spec = pl.BlockSpec((1, 1, ML_H, 8, LANES), lambda d, b, j: (d, b, 0, 0, 0))
    out, cnf, mf = pl.pallas_call(
        _mlstm_kernel,
        grid=(2, bsz, nj),
        in_specs=[
            pl.BlockSpec((1, lc, 256), tmap(0)),
            pl.BlockSpec((1, lc, 256), tmap(1)),
            pl.BlockSpec((1, lc, 256), tmap(2)),
            pl.BlockSpec((1, lc, LANES), lambda d, b, j: (b, jnp.where(d == 0, j, nj - 1 - j), d)),
            pl.BlockSpec((1, 1, LANES), lambda d, b, j: (d, 0, 0)),
            pl.BlockSpec((1, lc, lc), lambda d, b, j: (d, 0, 0)),
            pl.BlockSpec((LANES, ML_H * LANES), lambda d, b, j: (0, 0)),
            cn_spec,
            m_spec,
        ],
        out_specs=[
            pl.BlockSpec((1, 1, lc, 256), lambda d, b, j: (d, b, jnp.where(d == 0, j, nj - 1 - j), 0)),
            cn_spec,
            m_spec,
        ],
        out_shape=[
            jax.ShapeDtypeStruct((2, bsz, l, 256), f32),
            jax.ShapeDtypeStruct((2, bsz, ML_H, LANES, 2 * LANES), f32),
            jax.ShapeDtypeStruct((2, bsz, ML_H, 8, LANES), f32),
        ],
        scratch_shapes=[pltpu.VMEM((ML_H, LANES, 2 * LANES), f32), pltpu.VMEM((ML_H, 8, LANES), f32)],
        compiler_params=_cparams(("parallel", "parallel", "arbitrary")),
        name="mlstm",
    )(ml, ml, ml, mlg, gate_b2, tri, sel, cn0, m0)
    return out, (cnf, mf)


def _tri_pair(n, block):
    r = np.arange(n)[:, None]
    c = np.arange(n)[None, :]
    same = (r // block) == (c // block)
    lower = (same & (r >= c)).astype(np.float32)
    upper = (same & (r <= c)).astype(np.float32)
    return jnp.asarray(np.stack([lower, upper]), dtype=bf16)


HG_W = 256
HG_FACTOR_MAX = 80.0


def _hgrn2_kernel(q_ref, v_ref, f_ref, lbc_ref, tri_ref, bias_ref, s0_ref, o_ref, sf_ref, st_ref, p_scr):
    d = pl.program_id(0)
    j = pl.program_id(2)
    nj = pl.num_programs(2)
    lg = q_ref.shape[1]
    nck = lg // HG_CHUNK
    hw = HG_W // 2

    @pl.when(j == 0)
    def _():
        st_ref[...] = s0_ref[0, 0]

    fr = f_ref[0]
    log_lb = lbc_ref[0:1, :]
    log_1m_lb = lbc_ref[1:2, :]
    one_m_lb = lbc_ref[2:3, :]
    t = jnp.exp(-jnp.abs(fr))
    b = log_1m_lb + (jnp.minimum(fr, 0.0) - jnp.log(1.0 + t))
    kk = one_m_lb * (jnp.where(fr >= 0.0, t, 1.0) / (1.0 + t))
    log_f = jnp.maximum(log_lb, b) + jnp.log(1.0 + jnp.exp(-jnp.abs(log_lb - b)))
    hi, lo = _split_hi_lo(log_f)
    g = _dot(tri_ref[0], hi) + _dot(tri_ref[0], lo)
    q = q_ref[0].astype(f32)
    qe = (q * jnp.exp(g)).astype(bf16)

    rr = lax.broadcasted_iota(jnp.int32, (hw, hw), 0) // 64
    cc = lax.broadcasted_iota(jnp.int32, (hw, hw), 1) // 64
    same_head = rr == cc
    bd_b = jnp.where(same_head, 1.0, 0.0).astype(bf16)

    sub = 8
    nsub = HG_CHUNK // sub

    lane_head = lax.broadcasted_iota(jnp.int32, (HG_CHUNK, HG_W), 1) // 64
    row_t = lax.broadcasted_iota(jnp.int32, (HG_H * HG_CHUNK, HG_CHUNK), 0) % HG_CHUNK
    col_s = lax.broadcasted_iota(jnp.int32, (HG_H * HG_CHUNK, HG_CHUNK), 1)

    def run(order, last_row, fwd, factored):
        causal = (row_t >= col_s) if fwd else (row_t <= col_s)

        def block_kind(s, tb):
            lo, hi = tb * sub, tb * sub + sub - 1
            if fwd:
                return "all" if lo >= s else ("none" if hi < s else "some")
            return "all" if hi <= s else ("none" if lo > s else "some")

        def build_p(c):
            r0 = c * HG_CHUNK
            gc = g[r0:r0 + HG_CHUNK]
            kc = kk[r0:r0 + HG_CHUNK]
            qc = q[r0:r0 + HG_CHUNK]
            for s in range(HG_CHUNK):
                blocks = []
                for tb in range(nsub):
                    kind = block_kind(s, tb)
                    if kind == "none":
                        blocks.append(jnp.zeros((sub, HG_W), f32))
                        continue
                    diff = gc[tb * sub:(tb + 1) * sub] - gc[s:s + 1, :]
                    if kind == "some":
                        diff = diff + bias_ref[0, s * HG_CHUNK + tb * sub:s * HG_CHUNK + (tb + 1) * sub, :]
                    blocks.append(qc[tb * sub:(tb + 1) * sub] * kc[s:s + 1, :] * jnp.exp(diff))
                p_scr[c, s * HG_CHUNK:(s + 1) * HG_CHUNK, :] = jnp.concatenate(blocks, axis=0).astype(bf16)

        def intra_direct(c):
            r0 = c * HG_CHUNK
            vc = v_ref[0, r0:r0 + HG_CHUNK, :].astype(f32)
            abc = jnp.concatenate([_dot(p_scr[c, :, 0:hw], bd_b), _dot(p_scr[c, :, hw:HG_W], bd_b)], axis=1)
            o_blocks = []
            for tb in range(nsub):
                acc = None
                for s in range(HG_CHUNK):
                    if block_kind(s, tb) == "none":
                        continue
                    term = abc[s * HG_CHUNK + tb * sub:s * HG_CHUNK + (tb + 1) * sub, :] * vc[s:s + 1, :]
                    acc = term if acc is None else acc + term
                o_blocks.append(acc)
            return jnp.concatenate(o_blocks, axis=0)

        def scores_factored(c):
            r0 = c * HG_CHUNK
            qec = qe[r0:r0 + HG_CHUNK]
            kx = (kk[r0:r0 + HG_CHUNK] * jnp.exp(-g[r0:r0 + HG_CHUNK])).astype(bf16)
            qs = jnp.concatenate([jnp.where(lane_head == h, qec, jnp.zeros_like(qec)) for h in range(HG_H)], axis=0)
            return jnp.where(causal, _dot_nt(qs, kx), 0.0).astype(bf16)

        def intra_factored(c, a):
            o_all = _dot(a, v_ref[0, c * HG_CHUNK:(c + 1) * HG_CHUNK, :])
            o = None
            for h in range(HG_H):
                term = jnp.where(lane_head == h, o_all[h * HG_CHUNK:(h + 1) * HG_CHUNK], 0.0)
                o = term if o is None else o + term
            return o

        def state_delta(c):
            r0 = c * HG_CHUNK
            gc = g[r0:r0 + HG_CHUNK]
            vcb = v_ref[0, r0:r0 + HG_CHUNK, :]
            g_last = gc[last_row:last_row + 1, :]
            kt = (kk[r0:r0 + HG_CHUNK] * jnp.exp(g_last - gc)).astype(bf16)
            return (jnp.exp(g_last), _dot_tn(vcb[:, 0:hw], kt[:, 0:hw]), _dot_tn(vcb[:, hw:HG_W], kt[:, hw:HG_W]))

        def finish(c, o, delta):
            r0 = c * HG_CHUNK
            dec, d_lo, d_hi = delta
            st = st_ref[...]
            stb = st.astype(bf16)
            qec = qe[r0:r0 + HG_CHUNK]
            o = o + jnp.concatenate([_dot_nt(qec[:, 0:hw], stb[0:hw]), _dot_nt(qec[:, hw:HG_W], stb[hw:HG_W])], axis=1)
            o_ref[0, 0, r0:r0 + HG_CHUNK, :] = o
            st_ref[0:hw, :] = st[0:hw] * dec[:, 0:hw] + jnp.where(same_head, d_lo, 0.0)
            st_ref[hw:HG_W, :] = st[hw:HG_W] * dec[:, hw:HG_W] + jnp.where(same_head, d_hi, 0.0)

        order = list(order)
        if factored:
            n = len(order)
            stage_a, stage_b = {}, {}
            for i in range(n + 2):
                if i < n:
                    stage_a[order[i]] = (scores_factored(order[i]), state_delta(order[i]))
                if 0 <= i - 1 < n:
                    c = order[i - 1]
                    stage_b[c] = intra_factored(c, stage_a[c][0])
                if 0 <= i - 2 < n:
                    c = order[i - 2]
                    finish(c, stage_b.pop(c), stage_a.pop(c)[1])
        else:
            for c in order:
                build_p(c)
                finish(c, intra_direct(c), state_delta(c))

    can_factor = jnp.max(-g) < HG_FACTOR_MAX
    for direction, (order, last_row) in enumerate(((range(nck), HG_CHUNK - 1), (range(nck - 1, -1, -1), 0))):
        for factored in (True, False):
            @pl.when((d == direction) & (can_factor if factored else jnp.logical_not(can_factor)))
            def _(order=order, last_row=last_row, direction=direction, factored=factored):
                run(order, last_row, direction == 0, factored)

    @pl.when(j == nj - 1)
    def _():
        sf_ref[0, 0] = st_ref[...]


def _hgrn2(hg, hgf, lbc, tri, bias, s0):
    bsz, l, _ = hg.shape
    lg = min(HG_GROUP, l)
    nj = l // lg
    hw = HG_W // 2

    def tmap(col):
        return lambda d, b, j: (b, jnp.where(d == 0, j, nj - 1 - j), col)

    out, sf = pl.pallas_call(
        _hgrn2_kernel,
        grid=(2, bsz, nj),
        in_specs=[
            pl.BlockSpec((1, lg, HG_W), tmap(0)),
            pl.BlockSpec((1, lg, HG_W), tmap(1)),
            pl.BlockSpec((1, lg, HG_W), lambda d, b, j: (b, jnp.where(d == 0, j, nj - 1 - j), d)),
            pl.BlockSpec((8, HG_W), lambda d, b, j: (0, 0)),
            pl.BlockSpec((1, lg, lg), lambda d, b, j: (d, 0, 0)),
            pl.BlockSpec((1, HG_CHUNK * HG_CHUNK, HG_W), lambda d, b, j: (d, 0, 0)),
            pl.BlockSpec((1, 1, HG_W, hw), lambda d, b, j: (d, b, 0, 0)),
        ],
        out_specs=[
            pl.BlockSpec((1, 1, lg, HG_W), lambda d, b, j: (d, b, jnp.where(d == 0, j, nj - 1 - j), 0)),
            pl.BlockSpec((1, 1, HG_W, hw), lambda d, b, j: (d, b, 0, 0)),
        ],
        out_shape=[
            jax.ShapeDtypeStruct((2, bsz, l, HG_W), f32),
            jax.ShapeDtypeStruct((2, bsz, HG_W, hw), f32),
        ],
        scratch_shapes=[
            pltpu.VMEM((HG_W, hw), f32),
            pltpu.VMEM((lg // HG_CHUNK, HG_CHUNK * HG_CHUNK, HG_W), bf16),
        ],
        compiler_params=_cparams(("parallel", "parallel", "arbitrary")),
        name="hgrn2",
    )(hg, hg, hgf, lbc, tri, bias, s0)
    return out, sf


def _hgrn2_bias():
    idx = np.arange(HG_CHUNK * HG_CHUNK)
    s, t = idx // HG_CHUNK, idx % HG_CHUNK
    fwd = np.where(t >= s, 0.0, MASK_NEG).astype(np.float32)
    bwd = np.where(t <= s, 0.0, MASK_NEG).astype(np.float32)
    return jnp.asarray(np.broadcast_to(np.stack([fwd, bwd])[:, :, None], (2, HG_CHUNK * HG_CHUNK, HG_W)))


DA_V = 128


def _attn_kernel(*refs, has_x, lam_init, kg, nq):
    if has_x:
        (q_ref, kx_ref, kc_ref, vx_ref, vc_ref, lp_ref, nw_ref, o_ref,
         s0_scr, s1_scr, m0_scr, m1_scr, a0_scr, a1_scr, va_scr) = refs
        lx = kx_ref.shape[1]
    else:
        q_ref, kc_ref, vc_ref, lp_ref, nw_ref, o_ref, s0_scr, s1_scr, m0_scr, m1_scr, a0_scr, a1_scr, va_scr = refs
        lx = 0
    tq = q_ref.shape[1]
    lctx = kc_ref.shape[1]
    t = pl.program_id(0)

    @pl.when(t == 0)
    def _():
        s1_scr[...] = jnp.zeros(s1_scr.shape, f32)
        m1_scr[...] = jnp.zeros(m1_scr.shape, f32)
        a1_scr[...] = jnp.ones(a1_scr.shape, f32)

    @pl.when((t == 0) | ((t - 1) % nq == 0))
    def _():
        def ones_col(n):
            lane = lax.broadcasted_iota(jnp.int32, (n, DA_V), 1)
            return jnp.where(lane == 0, 1.0, 0.0).astype(bf16)

        if has_x:
            va_scr[0:lx, 0:DA_V] = vx_ref[0]
            va_scr[0:lx, DA_V:2 * DA_V] = ones_col(lx)
        va_scr[lx:lx + lctx, 0:DA_V] = vc_ref[0]
        va_scr[lx:lx + lctx, DA_V:2 * DA_V] = ones_col(lctx)

    groups = ([(kx_ref, g * kg, kg, g * kg) for g in range(lx // kg)] if has_x else []) + [(kc_ref, 0, lctx, lx)]

    def step(sa_scr, ma_scr, aa_scr, sb_scr, mb_scr, ab_scr):
        q = q_ref[0]
        lane = lax.broadcasted_iota(jnp.int32, (tq, DA_V), 1)
        zero = jnp.zeros_like(q)
        qq = jnp.concatenate([jnp.where(lane < 64, q, zero), jnp.where(lane >= 64, q, zero)], axis=0)
        m128 = None
        for ref, r0, n, c0 in groups:
            s = _dot_nt(qq, ref[0, r0:r0 + n, :])
            sa_scr[:, c0:c0 + n] = s
            for c in range(n // LANES):
                blk = s[:, c * LANES:(c + 1) * LANES]
                m128 = blk if m128 is None else jnp.maximum(m128, blk)
        ma_scr[...] = jnp.broadcast_to(jnp.max(m128, axis=-1, keepdims=True), m128.shape)
        m_prev = mb_scr[...]
        acc = None
        for ref, r0, n, c0 in groups:
            m_rep = jnp.concatenate([m_prev] * (n // LANES), axis=1)
            p = jnp.exp(sb_scr[:, c0:c0 + n] - m_rep).astype(bf16)
            part = _dot(p, va_scr[c0:c0 + n, :])
            acc = part if acc is None else acc + part
        aa_scr[...] = acc
        acc = ab_scr[:, 0:DA_V]
        l = ab_scr[:, DA_V:DA_V + 1]
        lp = lp_ref[...]
        lam = (jnp.exp(jnp.sum(lp[0:1] * lp[1:2], axis=-1, keepdims=True))
               - jnp.exp(jnp.sum(lp[2:3] * lp[3:4], axis=-1, keepdims=True)) + lam_init)
        o = acc[0:tq] / l[0:tq] - lam * (acc[tq:2 * tq] / l[tq:2 * tq])
        ms = jnp.mean(o * o, axis=-1, keepdims=True)
        o_ref[0] = (o * lax.rsqrt(ms + EPS) * nw_ref[...] * (1.0 - lam_init)).astype(bf16)

    @pl.when(t % 2 == 0)
    def _():
        step(s0_scr, m0_scr, a0_scr, s1_scr, m1_scr, a1_scr)

    @pl.when(t % 2 == 1)
    def _():
        step(s1_scr, m1_scr, a1_scr, s0_scr, m0_scr, a0_scr)


def _attn(da_q, da_c, lam_p, nw, lam_init, tq):
    has_x = da_q is not da_c
    bsz, l, _ = da_q.shape
    lctx = da_c.shape[1]
    nq = l // tq
    nt = bsz * DA_H * nq
    lx = l if has_x else 0
    kg = _pick_tile(l, (1024, 512, 256, 128))

    def split(t):
        return t // (DA_H * nq), (t // nq) % DA_H, t % nq

    def cur(col0, whole):
        def index(t):
            b, h, i = split(jnp.minimum(t, nt - 1))
            return (b, 0 if whole else i, col0 + h)
        return index

    def prev(col0, whole, lag=1):
        def index(t):
            b, h, i = split(jnp.clip(t - lag, 0, nt - 1))
            return (b, 0 if whole else i, col0 + h)
        return index

    in_specs = [pl.BlockSpec((1, tq, DA_V), cur(0, False))]
    args = [da_q]
    if has_x:
        in_specs += [pl.BlockSpec((1, l, DA_V), cur(4, True))]
        args += [da_q]
    in_specs += [pl.BlockSpec((1, lctx, DA_V), cur(4, True))]
    args += [da_c]
    if has_x:
        in_specs += [pl.BlockSpec((1, l, DA_V), prev(8, True))]
        args += [da_q]
    in_specs += [
        pl.BlockSpec((1, lctx, DA_V), prev(8, True)),
        pl.BlockSpec((4, 64), lambda t: (0, 0)),
        pl.BlockSpec((1, DA_V), lambda t: (0, 0)),
    ]
    args += [da_c, lam_p, nw]
    return pl.pallas_call(
        functools.partial(_attn_kernel, has_x=has_x, lam_init=lam_init, kg=kg, nq=nq),
        grid=(nt + 2,),
        in_specs=in_specs,
        out_specs=pl.BlockSpec((1, tq, DA_V), prev(0, False, lag=2)),
        out_shape=jax.ShapeDtypeStruct((bsz, l, DA_H * DA_V), bf16),
        scratch_shapes=[
            pltpu.VMEM((2 * tq, lx + lctx), f32),
            pltpu.VMEM((2 * tq, lx + lctx), f32),
            pltpu.VMEM((2 * tq, LANES), f32),
            pltpu.VMEM((2 * tq, LANES), f32),
            pltpu.VMEM((2 * tq, 2 * DA_V), f32),
            pltpu.VMEM((2 * tq, 2 * DA_V), f32),
            pltpu.VMEM((lx + lctx, 2 * DA_V), bf16),
        ],
        compiler_params=_cparams(("arbitrary",)),
        name="attn_x" if has_x else "attn_c",
    )(*args)


def _head_ms(y, bd_b):
    return _dot2(y * y, bd_b) * (1.0 / 64.0)


def _outproj_kernel(x_ref, mod_ref, mlf_ref, mlb_ref, mlo_ref, hgf_ref, hgb_ref, hgg_ref, da_ref,
                    mlw_ref, hgw_ref, w_ref, nw_ref, o_ref):
    d = x_ref.shape[-1]
    rr = lax.broadcasted_iota(jnp.int32, (256, 256), 0) // 64
    cc = lax.broadcasted_iota(jnp.int32, (256, 256), 1) // 64
    bd_b = jnp.where(rr == cc, 1.0, 0.0).astype(bf16)
    ml = mlf_ref[0, 0] + mlb_ref[0, 0]
    ml = ml * lax.rsqrt(_head_ms(ml, bd_b) + EPS) * mlw_ref[...]
    ml = _sigmoid(mlo_ref[0].astype(f32)) * ml
    hg = hgf_ref[0, 0] + hgb_ref[0, 0]
    hg = hg * lax.rsqrt(_head_ms(hg, bd_b) + EPS) * hgw_ref[...]
    gg = hgg_ref[0].astype(f32)
    hg = hg * (gg * _sigmoid(gg))
    mix = (_dot(ml.astype(bf16), w_ref[0:256, :]) + _dot(hg.astype(bf16), w_ref[256:512, :])
           + _dot(da_ref[0], w_ref[512:1024, :]))
    ms = jnp.mean(mix * mix, axis=-1, keepdims=True)
    y = mix * lax.rsqrt(ms + EPS) * nw_ref[...]
    o_ref[0] = x_ref[0] + mod_ref[0][:, 2 * d:3 * d] * y


def _outproj(x, mod, mod_row0, per_batch, ml_out, ml, hg_out, hg, da_out, mlw, hgw, w, nw, tm, name):
    bsz, l, d = x.shape
    nt = l // tm
    mod_map = (lambda b, i: (mod_row0 + b, 0, 0)) if per_batch else (lambda b, i: (mod_row0, 0, 0))
    return pl.pallas_call(
        _outproj_kernel,
        grid=(bsz, nt),
        in_specs=[
            pl.BlockSpec((1, tm, d), lambda b, i: (b, i, 0)),
            pl.BlockSpec((1, 1, mod.shape[-1]), mod_map),
            pl.BlockSpec((1, 1, tm, 256), lambda b, i: (0, b, i, 0)),
            pl.BlockSpec((1, 1, tm, 256), lambda b, i: (1, b, i, 0)),
            pl.BlockSpec((1, tm, 256), lambda b, i: (b, i, 3)),
            pl.BlockSpec((1, 1, tm, 256), lambda b, i: (0, b, i, 0)),
            pl.BlockSpec((1, 1, tm, 256), lambda b, i: (1, b, i, 0)),
            pl.BlockSpec((1, tm, 256), lambda b, i: (b, i, 2)),
            pl.BlockSpec((1, tm, 512), lambda b, i: (b, i, 0)),
            pl.BlockSpec((1, 256), lambda b, i: (0, 0)),
            pl.BlockSpec((1, 256), lambda b, i: (0, 0)),
            pl.BlockSpec((d, d), lambda b, i: (0, 0)),
            pl.BlockSpec((1, d), lambda b, i: (0, 0)),
        ],
        out_specs=pl.BlockSpec((1, tm, d), lambda b, i: (b, i, 0)),
        out_shape=jax.ShapeDtypeStruct((bsz, l, d), f32),
        compiler_params=_cparams(("parallel", "parallel")),
        name=name,
    )(x, mod, ml_out, ml_out, ml, hg_out, hg_out, hg, da_out, mlw, hgw, w, nw)


def _ffn_kernel(*refs, moe):
    if moe:
        x_ref, mod_ref, nw1_ref, rw_ref, rb_ref, wg_ref, wu_ref, wd_ref, nw2_ref, o_ref, h_scr, acc_scr, gate_scr = refs
    else:
        x_ref, mod_ref, nw1_ref, wg_ref, wu_ref, wd_ref, nw2_ref, o_ref, h_scr, acc_scr = refs
    e = pl.program_id(2)
    j = pl.program_id(3)
    ne = pl.num_programs(2)
    nj = pl.num_programs(3)
    d = x_ref.shape[-1]
    tm = x_ref.shape[1]

    @pl.when((e == 0) & (j == 0))
    def _():
        x = x_ref[0]
        ms = jnp.mean(x * x, axis=-1, keepdims=True)
        h = x * lax.rsqrt(ms + EPS) * nw1_ref[...]
        mod = mod_ref[0]
        h = h * (1.0 + mod[:, 4 * d:5 * d]) + mod[:, 3 * d:4 * d]
        h_scr[...] = h.astype(bf16)
        acc_scr[...] = jnp.zeros_like(acc_scr)
        if moe:
            hh, hl = _split_hi_lo(h)
            rw = rw_ref[...]
            wh, wl = _split_hi_lo(rw)
            logits = _dot(hh, wh) + _dot(hl, wh) + _dot(hh, wl) + rb_ref[...]
            lane = lax.broadcasted_iota(jnp.int32, (tm, LANES), 1)
            real = lane < N_EXPERTS
            lg = jnp.where(real, logits, MASK_NEG)
            m1 = jnp.max(lg, axis=-1, keepdims=True)
            i1 = jnp.min(jnp.where(lg == m1, lane, LANES), axis=-1, keepdims=True)
            lg2 = jnp.where(lane == i1, MASK_NEG, lg)
            m2 = jnp.max(lg2, axis=-1, keepdims=True)
            i2 = jnp.min(jnp.where(lg2 == m2, lane, LANES), axis=-1, keepdims=True)
            e2 = jnp.exp(m2 - m1)
            g1 = 1.0 / (1.0 + e2)
            g2 = e2 / (1.0 + e2)
            gate_scr[...] = jnp.where(lane == i1, g1, 0.0) + jnp.where(lane == i2, g2, 0.0)

    hb = h_scr[...]
    a = _dot(hb, wg_ref[0])
    u = _dot(hb, wu_ref[0])
    act = (a * _sigmoid(a)) * u
    if moe:
        lane = lax.broadcasted_iota(jnp.int32, (tm, LANES), 1)
        ge = jnp.sum(jnp.where(lane == e, gate_scr[...], 0.0), axis=-1, keepdims=True)
        act = act * ge
    acc_scr[...] += _dot(act.astype(bf16), wd_ref[0])

    @pl.when((e == ne - 1) & (j == nj - 1))
    def _():
        y = acc_scr[...]
        ms = jnp.mean(y * y, axis=-1, keepdims=True)
        y = y * lax.rsqrt(ms + EPS) * nw2_ref[...]
        o_ref[0] = x_ref[0] + mod_ref[0][:, 5 * d:6 * d] * y


def _ffn(x, mod, mod_row0, per_batch, nw1, wg, wu, wd, nw2, tm, tf, router=None, name="ffn"):
    bsz, l, d = x.shape
    ne, _, ff = wg.shape
    nt = l // tm
    nf = ff // tf
    moe = router is not None
    mod_map = (lambda b, i, e, j: (mod_row0 + b, 0, 0)) if per_batch else (lambda b, i, e, j: (mod_row0, 0, 0))
    in_specs = [
        pl.BlockSpec((1, tm, d), lambda b, i, e, j: (b, i, 0)),
        pl.BlockSpec((1, 1, mod.shape[-1]), mod_map),
        pl.BlockSpec((1, d), lambda b, i, e, j: (0, 0)),
    ]
    args = [x, mod, nw1]
    scratch = [pltpu.VMEM((tm, d), bf16), pltpu.VMEM((tm, d), f32)]
    if moe:
        in_specs += [
            pl.BlockSpec((d, LANES), lambda b, i, e, j: (0, 0)),
            pl.BlockSpec((1, LANES), lambda b, i, e, j: (0, 0)),
        ]
        args += list(router)
        scratch += [pltpu.VMEM((tm, LANES), f32)]
    in_specs += [
        pl.BlockSpec((1, d, tf), lambda b, i, e, j: (e, 0, j)),
        pl.BlockSpec((1, d, tf), lambda b, i, e, j: (e, 0, j)),
        pl.BlockSpec((1, tf, d), lambda b, i, e, j: (e, j, 0)),
        pl.BlockSpec((1, d), lambda b, i, e, j: (0, 0)),
    ]
    args += [wg, wu, wd, nw2]
    return pl.pallas_call(
        functools.partial(_ffn_kernel, moe=moe),
        grid=(bsz, nt, ne, nf),
        in_specs=in_specs,
        out_specs=pl.BlockSpec((1, tm, d), lambda b, i, e, j: (b, i, 0)),
        out_shape=jax.ShapeDtypeStruct((bsz, l, d), f32),
        scratch_shapes=scratch,
        compiler_params=_cparams(("parallel", "parallel", "arbitrary", "arbitrary")),
        name=name,
    )(*args)


MOE_TM = 512
MOE_TS = 512
META_I1, META_I2, META_R1, META_R2, META_G1, META_G2 = range(6)


def _router_kernel(x_ref, mod_ref, nw_ref, rw_ref, rb_ref, tri_ref, h_ref, meta_ref, cnt_ref, base_scr):
    d = x_ref.shape[-1]
    tm = x_ref.shape[1]

    @pl.when((pl.program_id(0) == 0) & (pl.program_id(1) == 0))
    def _():
        base_scr[...] = jnp.zeros_like(base_scr)

    x = x_ref[0]
    ms = jnp.mean(x * x, axis=-1, keepdims=True)
    h = x * lax.rsqrt(ms + EPS) * nw_ref[...]
    mod = mod_ref[0]
    h = h * (1.0 + mod[:, 4 * d:5 * d]) + mod[:, 3 * d:4 * d]
    h_ref[0] = h
    hh, hl = _split_hi_lo(h)
    wh, wl = _split_hi_lo(rw_ref[...])
    logits = _dot(hh, wh) + _dot(hl, wh) + _dot(hh, wl) + rb_ref[...]
    lane = lax.broadcasted_iota(jnp.int32, (tm, LANES), 1)
    lg = jnp.where(lane < N_EXPERTS, logits, MASK_NEG)
    m1 = jnp.max(lg, axis=-1, keepdims=True)
    i1 = jnp.min(jnp.where(lg == m1, lane, LANES), axis=-1, keepdims=True)
    lg2 = jnp.where(lane == i1, MASK_NEG, lg)
    m2 = jnp.max(lg2, axis=-1, keepdims=True)
    i2 = jnp.min(jnp.where(lg2 == m2, lane, LANES), axis=-1, keepdims=True)
    e2 = jnp.exp(m2 - m1)
    g1 = 1.0 / (1.0 + e2)
    g2 = e2 / (1.0 + e2)
    oh = jnp.where((lane == i1) | (lane == i2), 1.0, 0.0)
    pos = base_scr[...] + _dot(tri_ref[...], oh.astype(bf16))
    r1 = jnp.sum(jnp.where(lane == i1, pos, 0.0), axis=-1, keepdims=True)
    r2 = jnp.sum(jnp.where(lane == i2, pos, 0.0), axis=-1, keepdims=True)
    meta = jnp.zeros((tm, LANES), f32)
    for k, val in ((META_I1, i1.astype(f32)), (META_I2, i2.astype(f32)), (META_R1, r1), (META_R2, r2),
                   (META_G1, g1), (META_G2, g2)):
        meta = jnp.where(lane == k, val, meta)
    meta_ref[0] = meta
    new_base = pos[tm - 1:tm, :] + oh[tm - 1:tm, :]
    base_scr[...] = new_base
    cnt_ref[...] = jnp.broadcast_to(new_base, cnt_ref.shape)


def _moe_route(x, mod, mod_row0, nw1, rw, rb, tm):
    bsz, l, d = x.shape
    nt = l // tm
    r = jnp.arange(tm)
    tri = (r[:, None] > r[None, :]).astype(bf16)
    return pl.pallas_call(
        _router_kernel,
        grid=(bsz, nt),
        in_specs=[
            pl.BlockSpec((1, tm, d), lambda b, i: (b, i, 0)),
            pl.BlockSpec((1, 1, mod.shape[-1]), lambda b, i: (mod_row0 + b, 0, 0)),
            pl.BlockSpec((1, d), lambda b, i: (0, 0)),
            pl.BlockSpec((d, LANES), lambda b, i: (0, 0)),
            pl.BlockSpec((1, LANES), lambda b, i: (0, 0)),
            pl.BlockSpec((tm, tm), lambda b, i: (0, 0)),
        ],
        out_specs=[
            pl.BlockSpec((1, tm, d), lambda b, i: (b, i, 0)),
            pl.BlockSpec((1, tm, LANES), lambda b, i: (b, i, 0)),
            pl.BlockSpec((8, LANES), lambda b, i: (0, 0)),
        ],
        out_shape=[
            jax.ShapeDtypeStruct((bsz, l, d), f32),
            jax.ShapeDtypeStruct((bsz, l, LANES), f32),
            jax.ShapeDtypeStruct((8, LANES), f32),
        ],
        scratch_shapes=[pltpu.VMEM((1, LANES), f32)],
        compiler_params=_cparams(("arbitrary", "arbitrary")),
        name="moe_route",
    )(x, mod, nw1, rw, rb, tri)


def _row_copy(src_ref, src_row, dst_ref, dst_row, sem):
    return pltpu.make_async_copy(src_ref.at[pl.ds(src_row, 1)], dst_ref.at[pl.ds(dst_row, 1)], sem)


def _scatter_kernel(dest_ref, h_ref, xs_in_ref, xs_ref, sem):
    del xs_in_ref
    i = pl.program_id(0)
    ts = h_ref.shape[0]
    base = i * (2 * ts)

    def issue(r, carry):
        _row_copy(h_ref, r, xs_ref, dest_ref[base + r], sem).start(priority=0)
        _row_copy(h_ref, r, xs_ref, dest_ref[base + ts + r], sem).start(priority=1)
        return carry

    lax.fori_loop(0, ts, issue, 0, unroll=8)

    def drain(r, carry):
        _row_copy(h_ref, 0, xs_ref, 0, sem).wait()
        _row_copy(h_ref, 0, xs_ref, 0, sem).wait()
        return carry

    lax.fori_loop(0, ts, drain, 0, unroll=8)


def _moe_scatter(h2, dest, n_rows, ts):
    t, d = h2.shape
    xs0 = jnp.zeros((n_rows, d), f32)
    return pl.pallas_call(
        _scatter_kernel,
        grid_spec=pltpu.PrefetchScalarGridSpec(
            num_scalar_prefetch=1,
            grid=(t // ts,),
            in_specs=[
                pl.BlockSpec((ts, d), lambda i, dest: (i, 0)),
                pl.BlockSpec(memory_space=pl.ANY),
            ],
            out_specs=pl.BlockSpec(memory_space=pl.ANY),
            scratch_shapes=[pltpu.SemaphoreType.DMA],
        ),
        out_shape=jax.ShapeDtypeStruct((n_rows, d), f32),
        input_output_aliases={2: 0},
        compiler_params=_cparams(("arbitrary",)),
        name="moe_scatter",
    )(dest, h2, xs0)


def _expert_kernel(te_ref, tv_ref, x_ref, wg_ref, wu_ref, wd_ref, o_ref, acc_scr):
    del te_ref
    j = pl.program_id(0)
    f = pl.program_id(1)
    nf = pl.num_programs(1)
    valid = tv_ref[j] == 1

    @pl.when(valid)
    def _():
        xb = x_ref[...].astype(bf16)
        a = _dot(xb, wg_ref[0])
        u = _dot(xb, wu_ref[0])
        part = _dot(((a * _sigmoid(a)) * u).astype(bf16), wd_ref[0])

        @pl.when(f == 0)
        def _():
            acc_scr[...] = part

        @pl.when(f > 0)
        def _():
            acc_scr[...] += part

        @pl.when(f == nf - 1)
        def _():
            o_ref[...] = acc_scr[...]

    @pl.when(jnp.logical_not(valid))
    def _():
        o_ref[...] = jnp.zeros_like(o_ref)


def _moe_experts(xs, te, tv, wg, wu, wd, tm, tf):
    n_rows, d = xs.shape
    ff = wg.shape[-1]
    nf = ff // tf
    n_tiles = te.shape[0]

    def fsel(j, f, tv):
        return jnp.where(tv[j] == 1, f, nf - 1)

    return pl.pallas_call(
        _expert_kernel,
        grid_spec=pltpu.PrefetchScalarGridSpec(
            num_scalar_prefetch=2,
            grid=(n_tiles, nf),
            in_specs=[
                pl.BlockSpec((tm, d), lambda j, f, te, tv: (j, 0)),
                pl.BlockSpec((1, d, tf), lambda j, f, te, tv: (te[j], 0, fsel(j, f, tv))),
                pl.BlockSpec((1, d, tf), lambda j, f, te, tv: (te[j], 0, fsel(j, f, tv))),
                pl.BlockSpec((1, tf, d), lambda j, f, te, tv: (te[j], fsel(j, f, tv), 0)),
            ],
            out_specs=pl.BlockSpec((tm, d), lambda j, f, te, tv: (j, 0)),
            scratch_shapes=[pltpu.VMEM((tm, d), f32)],
        ),
        out_shape=jax.ShapeDtypeStruct((n_rows, d), f32),
        compiler_params=_cparams(("arbitrary", "arbitrary")),
        name="moe_experts",
    )(te, tv, xs, wg, wu, wd)


def _combine_kernel(dest_ref, x_ref, mod_ref, meta_ref, nw_ref, ys_ref, o_ref, y1_scr, y2_scr, sem):
    i = pl.program_id(0)
    ts = x_ref.shape[0]
    d = x_ref.shape[1]
    base = i * (2 * ts)

    def issue(r, carry):
        _row_copy(ys_ref, dest_ref[base + r], y1_scr, r, sem).start(priority=0)
        _row_copy(ys_ref, dest_ref[base + ts + r], y2_scr, r, sem).start(priority=1)
        return carry

    lax.fori_loop(0, ts, issue, 0, unroll=8)

    def drain(r, carry):
        _row_copy(ys_ref, 0, y1_scr, 0, sem).wait()
        _row_copy(ys_ref, 0, y2_scr, 0, sem).wait()
        return carry

    lax.fori_loop(0, ts, drain, 0, unroll=8)

    meta = meta_ref[...]
    y = meta[:, META_G1:META_G1 + 1] * y1_scr[...] + meta[:, META_G2:META_G2 + 1] * y2_scr[...]
    ms = jnp.mean(y * y, axis=-1, keepdims=True)
    y = y * lax.rsqrt(ms + EPS) * nw_ref[...]
    o_ref[...] = x_ref[...] + mod_ref[0][:, 5 * d:6 * d] * y


def _moe_combine(x2, mod, mod_row0, tokens_per_batch, meta2, nw2, ys, dest, ts):
    t, d = x2.shape
    per_b = tokens_per_batch // ts
    return pl.pallas_call(
        _combine_kernel,
        grid_spec=pltpu.PrefetchScalarGridSpec(
            num_scalar_prefetch=1,
            grid=(t // ts,),
            in_specs=[
                pl.BlockSpec((ts, d), lambda i, dest: (i, 0)),
                pl.BlockSpec((1, 1, mod.shape[-1]), lambda i, dest: (mod_row0 + i // per_b, 0, 0)),
                pl.BlockSpec((ts, LANES), lambda i, dest: (i, 0)),
                pl.BlockSpec((1, d), lambda i, dest: (0, 0)),
                pl.BlockSpec(memory_space=pl.ANY),
            ],
            out_specs=pl.BlockSpec((ts, d), lambda i, dest: (i, 0)),
            scratch_shapes=[pltpu.VMEM((ts, d), f32), pltpu.VMEM((ts, d), f32), pltpu.SemaphoreType.DMA],
        ),
        out_shape=jax.ShapeDtypeStruct((t, d), f32),
        compiler_params=_cparams(("arbitrary",)),
        name="moe_combine",
    )(dest, x2, mod, meta2, nw2, ys)


def _moe_sparse(x, mod, mod_row0, nw1, rw, rb, wg, wu, wd, nw2):
    bsz, l, d = x.shape
    t = bsz * l
    ne = wg.shape[0]
    tm_r = _pick_tile(l, (1024, 512, 256, 128))
    tm = min(MOE_TM, t)
    ts = _pick_tile(l, (MOE_TS, 256, 128))
    tf = _pick_tile(wg.shape[-1], (1792, 896, 512, 256, 128))

    h, meta, cnt = _moe_route(x, mod, mod_row0, nw1, rw, rb, tm_r)
    meta2 = meta.reshape(t, LANES)
    counts = cnt[0, :ne].astype(jnp.int32)
    tiles_e = (counts + tm - 1) // tm
    ends = jnp.cumsum(tiles_e)
    offset = (ends - tiles_e) * tm
    n_tiles = (2 * t) // tm + ne
    i1 = meta2[:, META_I1].astype(jnp.int32)
    i2 = meta2[:, META_I2].astype(jnp.int32)
    dest1 = jnp.take(offset, i1) + meta2[:, META_R1].astype(jnp.int32)
    dest2 = jnp.take(offset, i2) + meta2[:, META_R2].astype(jnp.int32)
    dest = jnp.stack([dest1.reshape(t // ts, ts), dest2.reshape(t // ts, ts)], axis=1).reshape(-1)
    jt = jnp.arange(n_tiles, dtype=jnp.int32)
    total = ends[ne - 1]
    tv = (jt < total).astype(jnp.int32)
    te_all = jnp.minimum(jnp.sum((jt[:, None] >= ends[None, :]).astype(jnp.int32), axis=1), ne - 1)
    te_last = jnp.take(te_all, jnp.maximum(total - 1, 0))
    te = jnp.where(tv == 1, te_all, te_last)
    n_rows = n_tiles * tm

    xs = _moe_scatter(h.reshape(t, d), dest, n_rows, ts)
    ys = _moe_experts(xs, te, tv, wg, wu, wd, tm, tf)
    out = _moe_combine(x.reshape(t, d), mod, mod_row0, l, meta2, nw2, ys, dest, ts)
    return out.reshape(bsz, l, d)


def _pick_tile(n, prefs):
    for t in prefs:
        if n % t == 0:
            return t
    return n


def kernel(x, c, ctx, c_ctx, w_mod, b_mod, norm_pre_mix, norm_post_mix, norm_pre_ffn, norm_post_ffn, w_in, w_out, ml_gate_b, ml_norm, hg_lb_logits, hg_norm, da_lambda, da_norm, ffn_w_gate, ffn_w_up, ffn_w_down, router_w, router_b, moe_w_gate, moe_w_up, moe_w_down):
    depth = w_mod.shape[0]
    bsz, l, d = x.shape
    lctx = ctx.shape[1]

    mod = _mod_vectors(c, c_ctx, w_mod, b_mod)
    rows = mod.shape[1]
    mod = mod.reshape(depth * rows, 1, 6 * d)

    lb_soft = jax.nn.softmax(hg_lb_logits.astype(f32), axis=0)
    lower_bounds = jnp.cumsum(lb_soft, axis=0) - lb_soft[0]

    tables = _rope_tables(l)
    ml_lc = min(ML_CHUNK, lctx)
    tri_ml_x = _tri_pair(min(ML_CHUNK, l), min(ML_CHUNK, l))
    tri_ml_c = _tri_pair(ml_lc, ml_lc)
    tri_hg_x = _tri_pair(min(HG_GROUP, l), HG_CHUNK)
    tri_hg_c = _tri_pair(min(HG_GROUP, lctx), HG_CHUNK)
    hg_bias = _hgrn2_bias()
    ml_sel = _mlstm_sel()

    tm_x = _pick_tile(l, (512, 256, 128))
    tm_c = _pick_tile(lctx, (256, 128))
    tq = _pick_tile(l, (512, 256, 128))

    for layer in range(depth):
        need_ctx = layer < depth - 1
        lam_init = 0.8 - 0.6 * math.exp(-0.3 * layer)
        row0 = layer * rows
        w_in_l = _prep_w_in(w_in[layer])
        w_out_l = w_out[layer].astype(bf16)
        nw_pre = norm_pre_mix[layer].reshape(1, d)

        pc = _inproj(ctx, mod, row0 + bsz, False, nw_pre, w_in_l, None, tm_c)
        px = _inproj(x, mod, row0, True, nw_pre, w_in_l, tables, tm_x)
        ml_c, mlg_c, hg_c, hgf_c, da_c = pc
        ml_x, mlg_x, hg_x, hgf_x, da_x = px

        gb = ml_gate_b[layer].astype(f32)
        z = jnp.zeros((LANES - 8,), f32)
        gate_b2 = jnp.stack([jnp.concatenate([gb[0:8], z]), jnp.concatenate([gb[8:16], z])]).reshape(2, 1, LANES)
        s0 = (jnp.zeros((2, bsz, ML_H, LANES, 2 * LANES), f32), jnp.zeros((2, bsz, ML_H, 8, LANES), f32))
        mlo_c, s_ml = _mlstm(ml_c, mlg_c, gate_b2, tri_ml_c, ml_sel, s0)
        mlo_x, _ = _mlstm(ml_x, mlg_x, gate_b2, tri_ml_x, ml_sel, s_ml)

        lb = lower_bounds[layer]
        lbc = jnp.zeros((8, HG_W), f32)
        lbc = lbc.at[0].set(jnp.log(jnp.maximum(lb, LB_FLOOR))).at[1].set(jnp.log1p(-lb)).at[2].set(1.0 - lb)
        h0 = jnp.zeros((2, bsz, HG_W, HG_W // 2), f32)
        hgo_c, s_hg = _hgrn2(hg_c, hgf_c, lbc, tri_hg_c, hg_bias, h0)
        hgo_x, _ = _hgrn2(hg_x, hgf_x, lbc, tri_hg_x, hg_bias, s_hg)

        lam_p = da_lambda[layer].astype(f32)
        da_nw = da_norm[layer].reshape(1, DA_V)
        dao_x = _attn(da_x, da_c, lam_p, da_nw, lam_init, tq)

        mlw = ml_norm[layer].reshape(1, 256)
        hgw = hg_norm[layer].reshape(1, 256)
        nw_post = norm_post_mix[layer].reshape(1, d)
        x = _outproj(x, mod, row0, True, mlo_x, ml_x, hgo_x, hg_x, dao_x, mlw, hgw, w_out_l, nw_post, tm_x, "outproj_x")
        if need_ctx:
            dao_c = _attn(da_c, da_c, lam_p, da_nw, lam_init, lctx)
            ctx = _outproj(ctx, mod, row0 + bsz, False, mlo_c, ml_c, hgo_c, hg_c, dao_c, mlw, hgw, w_out_l, nw_post, tm_c, "outproj_c")

        i = layer // 2
        nw1 = norm_pre_ffn[layer].reshape(1, d)
        nw2 = norm_post_ffn[layer].reshape(1, d)
        if layer % 2 == 0:
            wg = ffn_w_gate[i].astype(bf16)[None]
            wu = ffn_w_up[i].astype(bf16)[None]
            wd = ffn_w_down[i].astype(bf16)[None]
            router = None
        else:
            wg = moe_w_gate[i].astype(bf16)
            wu = moe_w_up[i].astype(bf16)
            wd = moe_w_down[i].astype(bf16)
            rw = jnp.zeros((d, LANES), f32).at[:, :N_EXPERTS].set(router_w[i])
            rb = jnp.zeros((1, LANES), f32).at[0, :N_EXPERTS].set(router_b[i])
            router = (rw, rb)
        ff = wg.shape[-1]
        tf = _pick_tile(ff, (896, 1408, 512, 256))
        tm_f = _pick_tile(l, (1024, 512, 256, 128) if tf <= 896 else (512, 256, 128))
        if router is None:
            x = _ffn(x, mod, row0, True, nw1, wg, wu, wd, nw2, tm_f, tf, None, name="ffn_x%d" % layer)
        else:
            x = _moe_sparse(x, mod, row0, nw1, rw, rb, wg, wu, wd, nw2)
        if need_ctx:
            ctx = _ffn(ctx, mod, row0 + bsz, False, nw1, wg, wu, wd, nw2, tm_c, tf, router, name="ffn_c%d" % layer)
    return x
```

```python
import functools
import math

import numpy as np
import jax
import jax.numpy as jnp
from jax import lax
from jax.experimental import pallas as pl
from jax.experimental.pallas import tpu as pltpu

f32 = jnp.float32
bf16 = jnp.bfloat16

ML_H = 4
HG_H = 4
DA_H = 4
GRID_W = 64
ROPE_BASE = 10000.0
N_EXPERTS = 8
EPS = 1e-6
MASK_NEG = -1e30
LB_FLOOR = 1e-30

LANES = 128
ML_CHUNK = 512
ML_SUB = 128
HG_CHUNK = 16
HG_GROUP = 512
VMEM_LIMIT = 56 * 1024 * 1024


def _cparams(sem):
    return pltpu.CompilerParams(dimension_semantics=sem, vmem_limit_bytes=VMEM_LIMIT)


def _sigmoid(x):
    return 1.0 / (1.0 + jnp.exp(-x))


def _log_sigmoid(x):
    return jnp.minimum(x, 0.0) - jnp.log(1.0 + jnp.exp(-jnp.abs(x)))


def _split_hi_lo(a):
    hi = a.astype(bf16)
    lo = (a - hi.astype(f32)).astype(bf16)
    return hi, lo


def _dot(a, b):
    return jnp.dot(a, b, preferred_element_type=f32)


def _dot_nt(a, b):
    return lax.dot_general(a, b, (((1,), (1,)), ((), ())), preferred_element_type=f32)


def _dot_tn(a, b):
    return lax.dot_general(a, b, (((0,), (0,)), ((), ())), preferred_element_type=f32)


def _dot2(a_f32, b_bf16):
    hi, lo = _split_hi_lo(a_f32)
    return _dot(hi, b_bf16) + _dot(lo, b_bf16)


def _mod_kernel(s_ref, w_ref, b_ref, o_ref):
    s = s_ref[...]
    s = s * _sigmoid(s)
    o_ref[0] = _dot(s.astype(bf16), w_ref[0].astype(bf16)) + b_ref[0]


def _mod_vectors(c, c_ctx, w_mod, b_mod):
    depth, d, n = w_mod.shape
    bsz = c.shape[0]
    rows = 8
    s = jnp.zeros((rows, d), f32).at[:bsz].set(c).at[bsz].set(c_ctx)
    tn = n // 4
    out = pl.pallas_call(
        _mod_kernel,
        grid=(depth, n // tn),
        in_specs=[
            pl.BlockSpec((rows, d), lambda l, j: (0, 0)),
            pl.BlockSpec((1, d, tn), lambda l, j: (l, 0, j)),
            pl.BlockSpec((1, 1, tn), lambda l, j: (l, 0, j)),
        ],
        out_specs=pl.BlockSpec((1, rows, tn), lambda l, j: (l, 0, j)),
        out_shape=jax.ShapeDtypeStruct((depth, rows, n), f32),
        compiler_params=_cparams(("parallel", "parallel")),
        name="mod_vectors",
    )(s, w_mod, b_mod.reshape(depth, 1, n))
    return out


P_ML = (0, 1024)
P_MLG = (1024, 1280)
P_HG = (1280, 2048)
P_HGF = (2048, 2560)
P_DA = (2560, 4096)
P_TOTAL = 4096


def _prep_w_in(w):
    d = w.shape[0]
    ml = w[:, 0:1024]
    g = w[:, 1024:1040]
    z = jnp.zeros((d, LANES - 8), w.dtype)
    gates = jnp.concatenate([g[:, 0:8], z, g[:, 8:16], z], axis=1)
    hg_q = w[:, 1040:1296]
    hg_f = w[:, 1296:1808]
    hg_i = w[:, 1808:2064]
    hg_g = w[:, 2064:2320]
    da = w[:, 2320:3856]
    return jnp.concatenate([ml, gates, hg_q, hg_i, hg_g, hg_f, da], axis=1).astype(bf16)


def _inproj_kernel(*refs, rope):
    if rope:
        x_ref, mod_ref, nw_ref, w_ref, cos_ref, sin_ref = refs[:6]
        outs = refs[6:]
    else:
        x_ref, mod_ref, nw_ref, w_ref = refs[:4]
        outs = refs[4:]
    ml_ref, mlg_ref, hg_ref, hgf_ref, da_ref = outs
    d = x_ref.shape[-1]
    x = x_ref[0]
    ms = jnp.mean(x * x, axis=-1, keepdims=True)
    h = x * lax.rsqrt(ms + EPS) * nw_ref[...]
    mod = mod_ref[0]
    h = h * (1.0 + mod[:, d:2 * d]) + mod[:, 0:d]
    hb = h.astype(bf16)

    def proj(lo, hi):
        return _dot(hb, w_ref[:, lo:hi])

    ml_ref[0] = proj(*P_ML).astype(bf16)
    mlg_ref[0] = proj(*P_MLG)
    pq = proj(P_HG[0], P_HG[0] + 256)
    hg_ref[0, :, 0:256] = (pq * _sigmoid(pq)).astype(bf16)
    hg_ref[0, :, 256:768] = proj(P_HG[0] + 256, P_HG[1]).astype(bf16)
    hgf_ref[0] = proj(*P_HGF)
    q = proj(P_DA[0], P_DA[0] + 512)
    k = proj(P_DA[0] + 512, P_DA[0] + 1024)
    if rope:
        cos = jnp.concatenate([cos_ref[...]] * 4, axis=1)
        sin = jnp.concatenate([sin_ref[...]] * 4, axis=1)
        lane = lax.broadcasted_iota(jnp.int32, (1, 512), 1)
        first = (lane % 32) < 16

        def rot(t):
            sw = jnp.where(first, pltpu.roll(t, 512 - 16, 1), pltpu.roll(t, 16, 1))
            return t * cos + sw * sin

        q = rot(q)
        k = rot(k)
    da_ref[0, :, 0:512] = (q * 0.125).astype(bf16)
    da_ref[0, :, 512:1024] = k.astype(bf16)
    da_ref[0, :, 1024:1536] = proj(P_DA[0] + 1024, P_DA[1]).astype(bf16)


def _inproj(x, mod, mod_row0, per_batch, nw, w, tables, tm):
    bsz, l, d = x.shape
    rope = tables is not None
    nt = l // tm
    mod_map = (lambda b, i: (mod_row0 + b, 0, 0)) if per_batch else (lambda b, i: (mod_row0, 0, 0))
    in_specs = [
        pl.BlockSpec((1, tm, d), lambda b, i: (b, i, 0)),
        pl.BlockSpec((1, 1, mod.shape[-1]), mod_map),
        pl.BlockSpec((1, d), lambda b, i: (0, 0)),
        pl.BlockSpec((d, P_TOTAL), lambda b, i: (0, 0)),
    ]
    args = [x, mod, nw, w]
    if rope:
        in_specs += [pl.BlockSpec((tm, LANES), lambda b, i: (i, 0))] * 2
        args += list(tables)
    widths = (1024, 256, 768, 512, 1536)
    dtypes = (bf16, f32, bf16, f32, bf16)
    out_specs = [pl.BlockSpec((1, tm, wd), lambda b, i: (b, i, 0)) for wd in widths]
    out_shape = [jax.ShapeDtypeStruct((bsz, l, wd), dt) for wd, dt in zip(widths, dtypes)]
    return pl.pallas_call(
        functools.partial(_inproj_kernel, rope=rope),
        grid=(bsz, nt),
        in_specs=in_specs,
        out_specs=out_specs,
        out_shape=out_shape,
        compiler_params=_cparams(("parallel", "parallel")),
        name="inproj_x" if rope else "inproj_c",
    )(*args)


def _rope_tables(l):
    rows = l // GRID_W
    row = jnp.repeat(jnp.arange(rows), GRID_W).astype(f32)
    col = jnp.tile(jnp.arange(GRID_W), rows).astype(f32)
    n_freq = 16
    inv = ROPE_BASE ** (-jnp.arange(n_freq, dtype=f32) / n_freq)
    lane = jnp.arange(LANES)
    freq = inv[lane % n_freq]
    pos = jnp.where(((lane % 64) // 32 == 0)[None, :], row[:, None], col[:, None])
    ang = pos * freq[None, :]
    sign = jnp.where((lane % 32) < 16, -1.0, 1.0)[None, :]
    return jnp.cos(ang), jnp.sin(ang) * sign


ML_D = 64


def _mlstm_kernel(q_ref, k_ref, v_ref, g_ref, gb_ref, tri_ref, sel_ref, cn0_ref, m0_ref,
                  o_ref, cnf_ref, mf_ref, cn_scr, m_scr):
    d = pl.program_id(0)
    j = pl.program_id(2)
    nj = pl.num_programs(2)
    lc = q_ref.shape[1]

    @pl.when(j == 0)
    def _():
        cn_scr[...] = cn0_ref[0, 0]
        m_scr[...] = m0_ref[0, 0]

    sub = min(ML_SUB, lc)
    nsub = lc // sub

    def run(order, fwd):
        g = g_ref[0] + gb_ref[0]
        lf = _log_sigmoid(g)
        hi, lo = _split_hi_lo(lf)
        bc = _dot(tri_ref[0], hi) + _dot(tri_ref[0], lo)
        z = pltpu.roll(g, 4, 1) - bc
        zt = z.T
        sel = sel_ref[...]
        yrep = _dot2(bc, sel)
        zrep = _dot2(z, sel)
        r = lax.broadcasted_iota(jnp.int32, (sub, sub), 0)
        c = lax.broadcasted_iota(jnp.int32, (sub, sub), 1)
        lane = lax.broadcasted_iota(jnp.int32, (sub, LANES), 1)
        ones_b = jnp.ones((sub, LANES), bf16)
        mask = (r >= c) if fwd else (r <= c)
        last = sub - 1 if fwd else 0

        def rep(x, n):
            return jnp.concatenate([x] * n, axis=1)

        pre = {}
        for cs in order:
            rows = slice(cs * sub, (cs + 1) * sub)
            for h in range(ML_H):
                p, hh = divmod(h, 2)
                half = (lane < ML_D) if hh == 0 else (lane >= ML_D)
                qp = q_ref[0, rows, p * LANES:(p + 1) * LANES]
                kp = k_ref[0, rows, p * LANES:(p + 1) * LANES] * jnp.asarray(0.125, bf16)
                vp = v_ref[0, rows, p * LANES:(p + 1) * LANES]
                b_rep = yrep[rows, h * LANES:(h + 1) * LANES]
                blast = b_rep[last:last + 1, :]
                qm = jnp.where(half, qp, jnp.zeros_like(qp))
                vo = jnp.concatenate([jnp.where(half, vp, jnp.zeros_like(vp)), ones_b], axis=1)
                s_raw = _dot_nt(qm, kp)
                a_rep = blast + zrep[rows, h * LANES:(h + 1) * LANES]
                m_loc = jnp.max(a_rep, axis=0, keepdims=True)
                w = jnp.exp(a_rep - m_loc)
                kw = jnp.where(half, kp.astype(f32) * w, 0.0).astype(bf16)
                dcn = _dot_tn(kw, vo)
                pre[cs, h] = (b_rep, blast, qm, vo, s_raw, m_loc, dcn)

        for cs in order:
            rows = slice(cs * sub, (cs + 1) * sub)
            outs = []
            for h in range(ML_H):
                b_rep, blast, qm, vo, s_raw, m_loc, dcn = pre.pop((cs, h))
                m0 = m_scr[h, 0:1, :]
                cn = cn_scr[h]
                nd_inter = _dot(qm, cn.astype(bf16))
                z_row = zt[4 + h:5 + h, rows]
                dm = jnp.where(mask, rep(b_rep, sub // LANES) + z_row, MASK_NEG)
                inter = b_rep + m0
                m = jnp.maximum(inter, jnp.max(dm, axis=-1, keepdims=True))
                w_intra = jnp.exp(dm - rep(m, sub // LANES))
                w_inter = jnp.exp(inter - m)
                nd = _dot((s_raw * w_intra).astype(bf16), vo) + rep(w_inter, 2) * nd_inter
                den = nd[:, LANES:2 * LANES]
                outs.append(nd[:, 0:LANES] / jnp.maximum(jnp.abs(den), jnp.exp(-m)))

                m_new = jnp.maximum(blast + m0, m_loc)
                s_old = jnp.exp(blast + m0 - m_new)
                s_new = jnp.exp(m_loc - m_new)
                cn_scr[h] = rep(s_old, 2) * cn + rep(s_new, 2) * dcn
                m_scr[h] = jnp.broadcast_to(m_new, (8, LANES))
            for p in range(ML_H // 2):
                o_ref[0, 0, rows, p * LANES:(p + 1) * LANES] = outs[2 * p] + outs[2 * p + 1]

    @pl.when(d == 0)
    def _():
        run(range(nsub), True)

    @pl.when(d == 1)
    def _():
        run(range(nsub - 1, -1, -1), False)

    @pl.when(j == nj - 1)
    def _():
        cnf_ref[0, 0] = cn_scr[...]
        mf_ref[0, 0] = m_scr[...]


def _mlstm_sel():
    sel = np.zeros((LANES, ML_H * LANES), np.float32)
    for h in range(ML_H):
        sel[4 + h, h * LANES:(h + 1) * LANES] = 1.0
    return jnp.asarray(sel, dtype=bf16)


def _mlstm(ml, mlg, gate_b2, tri, sel, state):
    bsz, l, _ = ml.shape
    lc = min(ML_CHUNK, l)
    nj = l // lc
    cn0, m0 = state

    def tmap(col):
        return lambda d, b, j: (b, jnp.where(d == 0, j, nj - 1 - j), col)

    cn_spec = pl.BlockSpec((1, 1, ML_H, LANES, 2 * LANES), lambda d, b, j: (d, b, 0, 0, 0))
    m_spec = pl.BlockSpec((1, 1, ML_H, 8, LANES), lambda d, b, j: (d, b, 0, 0, 0))
    out, cnf, mf = pl.pallas_call(
        _mlstm_kernel,
        grid=(2, bsz, nj),
        in_specs=[
            pl.BlockSpec((1, lc, 256), tmap(0)),
            pl.BlockSpec((1, lc, 256), tmap(1)),
            pl.BlockSpec((1, lc, 256), tmap(2)),
            pl.BlockSpec((1, lc, LANES), lambda d, b, j: (b, jnp.where(d == 0, j, nj - 1 - j), d)),
            pl.BlockSpec((1, 1, LANES), lambda d, b, j: (d, 0, 0)),
            pl.BlockSpec((1, lc, lc), lambda d, b, j: (d, 0, 0)),
            pl.BlockSpec((LANES, ML_H * LANES), lambda d, b, j: (0, 0)),
            cn_spec,
            m_spec,
        ],
        out_specs=[
            pl.BlockSpec((1, 1, lc, 256), lambda d, b, j: (d, b, jnp.where(d == 0, j, nj - 1 - j), 0)),
            cn_spec,
            m_spec,
        ],
        out_shape=[
            jax.ShapeDtypeStruct((2, bsz, l, 256), f32),
            jax.ShapeDtypeStruct((2, bsz, ML_H, LANES, 2 * LANES), f32),
            jax.ShapeDtypeStruct((2, bsz, ML_H, 8, LANES), f32),
        ],
        scratch_shapes=[pltpu.VMEM((ML_H, LANES, 2 * LANES), f32), pltpu.VMEM((ML_H, 8, LANES), f32)],
        compiler_params=_cparams(("parallel", "parallel", "arbitrary")),
        name="mlstm",
    )(ml, ml, ml, mlg, gate_b2, tri, sel, cn0, m0)
    return out, (cnf, mf)


def _tri_pair(n, block):
    r = np.arange(n)[:, None]
    c = np.arange(n)[None, :]
    same = (r // block) == (c // block)
    lower = (same & (r >= c)).astype(np.float32)
    upper = (same & (r <= c)).astype(np.float32)
    return jnp.asarray(np.stack([lower, upper]), dtype=bf16)


HG_W = 256
HG_FACTOR_MAX = 80.0


def _hgrn2_kernel(q_ref, v_ref, f_ref, lbc_ref, tri_ref, bias_ref, s0_ref, o_ref, sf_ref, st_ref, p_scr):
    d = pl.program_id(0)
    j = pl.program_id(2)
    nj = pl.num_programs(2)
    lg = q_ref.shape[1]
    nck = lg // HG_CHUNK
    hw = HG_W // 2

    @pl.when(j == 0)
    def _():
        st_ref[...] = s0_ref[0, 0]

    fr = f_ref[0]
    log_lb = lbc_ref[0:1, :]
    log_1m_lb = lbc_ref[1:2, :]
    one_m_lb = lbc_ref[2:3, :]
    t = jnp.exp(-jnp.abs(fr))
    b = log_1m_lb + (jnp.minimum(fr, 0.0) - jnp.log(1.0 + t))
    kk = one_m_lb * (jnp.where(fr >= 0.0, t, 1.0) / (1.0 + t))
    log_f = jnp.maximum(log_lb, b) + jnp.log(1.0 + jnp.exp(-jnp.abs(log_lb - b)))
    hi, lo = _split_hi_lo(log_f)
    g = _dot(tri_ref[0], hi) + _dot(tri_ref[0], lo)
    q = q_ref[0].astype(f32)
    qe = (q * jnp.exp(g)).astype(bf16)

    rr = lax.broadcasted_iota(jnp.int32, (hw, hw), 0) // 64
    cc = lax.broadcasted_iota(jnp.int32, (hw, hw), 1) // 64
    same_head = rr == cc
    bd_b = jnp.where(same_head, 1.0, 0.0).astype(bf16)

    sub = 8
    nsub = HG_CHUNK // sub

    lane_head = lax.broadcasted_iota(jnp.int32, (HG_CHUNK, HG_W), 1) // 64
    row_t = lax.broadcasted_iota(jnp.int32, (HG_H * HG_CHUNK, HG_CHUNK), 0) % HG_CHUNK
    col_s = lax.broadcasted_iota(jnp.int32, (HG_H * HG_CHUNK, HG_CHUNK), 1)

    def run(order, last_row, fwd, factored):
        causal = (row_t >= col_s) if fwd else (row_t <= col_s)

        def block_kind(s, tb):
            lo, hi = tb * sub, tb * sub + sub - 1
            if fwd:
                return "all" if lo >= s else ("none" if hi < s else "some")
            return "all" if hi <= s else ("none" if lo > s else "some")

        def build_p(c):
            r0 = c * HG_CHUNK
            gc = g[r0:r0 + HG_CHUNK]
            kc = kk[r0:r0 + HG_CHUNK]
            qc = q[r0:r0 + HG_CHUNK]
            for s in range(HG_CHUNK):
                blocks = []
                for tb in range(nsub):
                    kind = block_kind(s, tb)
                    if kind == "none":
                        blocks.append(jnp.zeros((sub, HG_W), f32))
                        continue
                    diff = gc[tb * sub:(tb + 1) * sub] - gc[s:s + 1, :]
                    if kind == "some":
                        diff = diff + bias_ref[0, s * HG_CHUNK + tb * sub:s * HG_CHUNK + (tb + 1) * sub, :]
                    blocks.append(qc[tb * sub:(tb + 1) * sub] * kc[s:s + 1, :] * jnp.exp(diff))
                p_scr[c, s * HG_CHUNK:(s + 1) * HG_CHUNK, :] = jnp.concatenate(blocks, axis=0).astype(bf16)

        def intra_direct(c):
            r0 = c * HG_CHUNK
            vc = v_ref[0, r0:r0 + HG_CHUNK, :].astype(f32)
            abc = jnp.concatenate([_dot(p_scr[c, :, 0:hw], bd_b), _dot(p_scr[c, :, hw:HG_W], bd_b)], axis=1)
            o_blocks = []
            for tb in range(nsub):
                acc = None
                for s in range(HG_CHUNK):
                    if block_kind(s, tb) == "none":
                        continue
                    term = abc[s * HG_CHUNK + tb * sub:s * HG_CHUNK + (tb + 1) * sub, :] * vc[s:s + 1, :]
                    acc = term if acc is None else acc + term
                o_blocks.append(acc)
            return jnp.concatenate(o_blocks, axis=0)

        def scores_factored(c):
            r0 = c * HG_CHUNK
            qec = qe[r0:r0 + HG_CHUNK]
            kx = (kk[r0:r0 + HG_CHUNK] * jnp.exp(-g[r0:r0 + HG_CHUNK])).astype(bf16)
            qs = jnp.concatenate([jnp.where(lane_head == h, qec, jnp.zeros_like(qec)) for h in range(HG_H)], axis=0)
            return jnp.where(causal, _dot_nt(qs, kx), 0.0).astype(bf16)

        def intra_factored(c, a):
            o_all = _dot(a, v_ref[0, c * HG_CHUNK:(c + 1) * HG_CHUNK, :])
            o = None
            for h in range(HG_H):
                term = jnp.where(lane_head == h, o_all[h * HG_CHUNK:(h + 1) * HG_CHUNK], 0.0)
                o = term if o is None else o + term
            return o

        def state_delta(c):
            r0 = c * HG_CHUNK
            gc = g[r0:r0 + HG_CHUNK]
            vcb = v_ref[0, r0:r0 + HG_CHUNK, :]
            g_last = gc[last_row:last_row + 1, :]
            kt = (kk[r0:r0 + HG_CHUNK] * jnp.exp(g_last - gc)).astype(bf16)
            return (jnp.exp(g_last), _dot_tn(vcb[:, 0:hw], kt[:, 0:hw]), _dot_tn(vcb[:, hw:HG_W], kt[:, hw:HG_W]))

        def finish(c, o, delta):
            r0 = c * HG_CHUNK
            dec, d_lo, d_hi = delta
            st = st_ref[...]
            stb = st.astype(bf16)
            qec = qe[r0:r0 + HG_CHUNK]
            o = o + jnp.concatenate([_dot_nt(qec[:, 0:hw], stb[0:hw]), _dot_nt(qec[:, hw:HG_W], stb[hw:HG_W])], axis=1)
            o_ref[0, 0, r0:r0 + HG_CHUNK, :] = o
            st_ref[0:hw, :] = st[0:hw] * dec[:, 0:hw] + jnp.where(same_head, d_lo, 0.0)
            st_ref[hw:HG_W, :] = st[hw:HG_W] * dec[:, hw:HG_W] + jnp.where(same_head, d_hi, 0.0)

        order = list(order)
        if factored:
            n = len(order)
            stage_a, stage_b = {}, {}
            for i in range(n + 2):
                if i < n:
                    stage_a[order[i]] = (scores_factored(order[i]), state_delta(order[i]))
                if 0 <= i - 1 < n:
                    c = order[i - 1]
                    stage_b[c] = intra_factored(c, stage_a[c][0])
                if 0 <= i - 2 < n:
                    c = order[i - 2]
                    finish(c, stage_b.pop(c), stage_a.pop(c)[1])
        else:
            for c in order:
                build_p(c)
                finish(c, intra_direct(c), state_delta(c))

    can_factor = jnp.max(-g) < HG_FACTOR_MAX
    for direction, (order, last_row) in enumerate(((range(nck), HG_CHUNK - 1), (range(nck - 1, -1, -1), 0))):
        for factored in (True, False):
            @pl.when((d == direction) & (can_factor if factored else jnp.logical_not(can_factor)))
            def _(order=order, last_row=last_row, direction=direction, factored=factored):
                run(order, last_row, direction == 0, factored)

    @pl.when(j == nj - 1)
    def _():
        sf_ref[0, 0] = st_ref[...]


def _hgrn2(hg, hgf, lbc, tri, bias, s0):
    bsz, l, _ = hg.shape
    lg = min(HG_GROUP, l)
    nj = l // lg
    hw = HG_W // 2

    def tmap(col):
        return lambda d, b, j: (b, jnp.where(d == 0, j, nj - 1 - j), col)

    out, sf = pl.pallas_call(
        _hgrn2_kernel,
        grid=(2, bsz, nj),
        in_specs=[
            pl.BlockSpec((1, lg, HG_W), tmap(0)),
            pl.BlockSpec((1, lg, HG_W), tmap(1)),
            pl.BlockSpec((1, lg, HG_W), lambda d, b, j: (b, jnp.where(d == 0, j, nj - 1 - j), d)),
            pl.BlockSpec((8, HG_W), lambda d, b, j: (0, 0)),
            pl.BlockSpec((1, lg, lg), lambda d, b, j: (d, 0, 0)),
            pl.BlockSpec((1, HG_CHUNK * HG_CHUNK, HG_W), lambda d, b, j: (d, 0, 0)),
            pl.BlockSpec((1, 1, HG_W, hw), lambda d, b, j: (d, b, 0, 0)),
        ],
        out_specs=[
            pl.BlockSpec((1, 1, lg, HG_W), lambda d, b, j: (d, b, jnp.where(d == 0, j, nj - 1 - j), 0)),
            pl.BlockSpec((1, 1, HG_W, hw), lambda d, b, j: (d, b, 0, 0)),
        ],
        out_shape=[
            jax.ShapeDtypeStruct((2, bsz, l, HG_W), f32),
            jax.ShapeDtypeStruct((2, bsz, HG_W, hw), f32),
        ],
        scratch_shapes=[
            pltpu.VMEM((HG_W, hw), f32),
            pltpu.VMEM((lg // HG_CHUNK, HG_CHUNK * HG_CHUNK, HG_W), bf16),
        ],
        compiler_params=_cparams(("parallel", "parallel", "arbitrary")),
        name="hgrn2",
    )(hg, hg, hgf, lbc, tri, bias, s0)
    return out, sf


def _hgrn2_bias():
    idx = np.arange(HG_CHUNK * HG_CHUNK)
    s, t = idx // HG_CHUNK, idx % HG_CHUNK
    fwd = np.where(t >= s, 0.0, MASK_NEG).astype(np.float32)
    bwd = np.where(t <= s, 0.0, MASK_NEG).astype(np.float32)
    return jnp.asarray(np.broadcast_to(np.stack([fwd, bwd])[:, :, None], (2, HG_CHUNK * HG_CHUNK, HG_W)))


DA_V = 128


def _attn_kernel(*refs, has_x, lam_init, kg, nq):
    if has_x:
        (q_ref, kx_ref, kc_ref, vx_ref, vc_ref, lp_ref, nw_ref, o_ref,
         s0_scr, s1_scr, m0_scr, m1_scr, a0_scr, a1_scr, va_scr) = refs
        lx = kx_ref.shape[1]
    else:
        q_ref, kc_ref, vc_ref, lp_ref, nw_ref, o_ref, s0_scr, s1_scr, m0_scr, m1_scr, a0_scr, a1_scr, va_scr = refs
        lx = 0
    tq = q_ref.shape[1]
    lctx = kc_ref.shape[1]
    t = pl.program_id(0)

    @pl.when(t == 0)
    def _():
        s1_scr[...] = jnp.zeros(s1_scr.shape, f32)
        m1_scr[...] = jnp.zeros(m1_scr.shape, f32)
        a1_scr[...] = jnp.ones(a1_scr.shape, f32)

    @pl.when((t == 0) | ((t - 1) % nq == 0))
    def _():
        def ones_col(n):
            lane = lax.broadcasted_iota(jnp.int32, (n, DA_V), 1)
            return jnp.where(lane == 0, 1.0, 0.0).astype(bf16)

        if has_x:
            va_scr[0:lx, 0:DA_V] = vx_ref[0]
            va_scr[0:lx, DA_V:2 * DA_V] = ones_col(lx)
        va_scr[lx:lx + lctx, 0:DA_V] = vc_ref[0]
        va_scr[lx:lx + lctx, DA_V:2 * DA_V] = ones_col(lctx)

    groups = ([(kx_ref, g * kg, kg, g * kg) for g in range(lx // kg)] if has_x else []) + [(kc_ref, 0, lctx, lx)]

    def step(sa_scr, ma_scr, aa_scr, sb_scr, mb_scr, ab_scr):
        q = q_ref[0]
        lane = lax.broadcasted_iota(jnp.int32, (tq, DA_V), 1)
        zero = jnp.zeros_like(q)
        qq = jnp.concatenate([jnp.where(lane < 64, q, zero), jnp.where(lane >= 64, q, zero)], axis=0)
        m128 = None
        for ref, r0, n, c0 in groups:
            s = _dot_nt(qq, ref[0, r0:r0 + n, :])
            sa_scr[:, c0:c0 + n] = s
            for c in range(n // LANES):
                blk = s[:, c * LANES:(c + 1) * LANES]
                m128 = blk if m128 is None else jnp.maximum(m128, blk)
        ma_scr[...] = jnp.broadcast_to(jnp.max(m128, axis=-1, keepdims=True), m128.shape)
        m_prev = mb_scr[...]
        acc = None
        for ref, r0, n, c0 in groups:
            m_rep = jnp.concatenate([m_prev] * (n // LANES), axis=1)
            p = jnp.exp(sb_scr[:, c0:c0 + n] - m_rep).astype(bf16)
            part = _dot(p, va_scr[c0:c0 + n, :])
            acc = part if acc is None else acc + part
        aa_scr[...] = acc
        acc = ab_scr[:, 0:DA_V]
        l = ab_scr[:, DA_V:DA_V + 1]
        lp = lp_ref[...]
        lam = (jnp.exp(jnp.sum(lp[0:1] * lp[1:2], axis=-1, keepdims=True))
               - jnp.exp(jnp.sum(lp[2:3] * lp[3:4], axis=-1, keepdims=True)) + lam_init)
        o = acc[0:tq] / l[0:tq] - lam * (acc[tq:2 * tq] / l[tq:2 * tq])
        ms = jnp.mean(o * o, axis=-1, keepdims=True)
        o_ref[0] = (o * lax.rsqrt(ms + EPS) * nw_ref[...] * (1.0 - lam_init)).astype(bf16)

    @pl.when(t % 2 == 0)
    def _():
        step(s0_scr, m0_scr, a0_scr, s1_scr, m1_scr, a1_scr)

    @pl.when(t % 2 == 1)
    def _():
        step(s1_scr, m1_scr, a1_scr, s0_scr, m0_scr, a0_scr)


def _attn(da_q, da_c, lam_p, nw, lam_init, tq):
    has_x = da_q is not da_c
    bsz, l, _ = da_q.shape
    lctx = da_c.shape[1]
    nq = l // tq
    nt = bsz * DA_H * nq
    lx = l if has_x else 0
    kg = _pick_tile(l, (1024, 512, 256, 128))

    def split(t):
        return t // (DA_H * nq), (t // nq) % DA_H, t % nq

    def cur(col0, whole):
        def index(t):
            b, h, i = split(jnp.minimum(t, nt - 1))
            return (b, 0 if whole else i, col0 + h)
        return index

    def prev(col0, whole, lag=1):
        def index(t):
            b, h, i = split(jnp.clip(t - lag, 0, nt - 1))
            return (b, 0 if whole else i, col0 + h)
        return index

    in_specs = [pl.BlockSpec((1, tq, DA_V), cur(0, False))]
    args = [da_q]
    if has_x:
        in_specs += [pl.BlockSpec((1, l, DA_V), cur(4, True))]
        args += [da_q]
    in_specs += [pl.BlockSpec((1, lctx, DA_V), cur(4, True))]
    args += [da_c]
    if has_x:
        in_specs += [pl.BlockSpec((1, l, DA_V), prev(8, True))]
        args += [da_q]
    in_specs += [
        pl.BlockSpec((1, lctx, DA_V), prev(8, True)),
        pl.BlockSpec((4, 64), lambda t: (0, 0)),
        pl.BlockSpec((1, DA_V), lambda t: (0, 0)),
    ]
    args += [da_c, lam_p, nw]
    return pl.pallas_call(
        functools.partial(_attn_kernel, has_x=has_x, lam_init=lam_init, kg=kg, nq=nq),
        grid=(nt + 2,),
        in_specs=in_specs,
        out_specs=pl.BlockSpec((1, tq, DA_V), prev(0, False, lag=2)),
        out_shape=jax.ShapeDtypeStruct((bsz, l, DA_H * DA_V), bf16),
        scratch_shapes=[
            pltpu.VMEM((2 * tq, lx + lctx), f32),
            pltpu.VMEM((2 * tq, lx + lctx), f32),
            pltpu.VMEM((2 * tq, LANES), f32),
            pltpu.VMEM((2 * tq, LANES), f32),
            pltpu.VMEM((2 * tq, 2 * DA_V), f32),
            pltpu.VMEM((2 * tq, 2 * DA_V), f32),
            pltpu.VMEM((lx + lctx, 2 * DA_V), bf16),
        ],
        compiler_params=_cparams(("arbitrary",)),
        name="attn_x" if has_x else "attn_c",
    )(*args)


def _head_ms(y, bd_b):
    return _dot2(y * y, bd_b) * (1.0 / 64.0)


def _outproj_kernel(x_ref, mod_ref, mlf_ref, mlb_ref, mlo_ref, hgf_ref, hgb_ref, hgg_ref, da_ref,
                    mlw_ref, hgw_ref, w_ref, nw_ref, o_ref):
    d = x_ref.shape[-1]
    rr = lax.broadcasted_iota(jnp.int32, (256, 256), 0) // 64
    cc = lax.broadcasted_iota(jnp.int32, (256, 256), 1) // 64
    bd_b = jnp.where(rr == cc, 1.0, 0.0).astype(bf16)
    ml = mlf_ref[0, 0] + mlb_ref[0, 0]
    ml = ml * lax.rsqrt(_head_ms(ml, bd_b) + EPS) * mlw_ref[...]
    ml = _sigmoid(mlo_ref[0].astype(f32)) * ml
    hg = hgf_ref[0, 0] + hgb_ref[0, 0]
    hg = hg * lax.rsqrt(_head_ms(hg, bd_b) + EPS) * hgw_ref[...]
    gg = hgg_ref[0].astype(f32)
    hg = hg * (gg * _sigmoid(gg))
    mix = (_dot(ml.astype(bf16), w_ref[0:256, :]) + _dot(hg.astype(bf16), w_ref[256:512, :])
           + _dot(da_ref[0], w_ref[512:1024, :]))
    ms = jnp.mean(mix * mix, axis=-1, keepdims=True)
    y = mix * lax.rsqrt(ms + EPS) * nw_ref[...]
    o_ref[0] = x_ref[0] + mod_ref[0][:, 2 * d:3 * d] * y


def _outproj(x, mod, mod_row0, per_batch, ml_out, ml, hg_out, hg, da_out, mlw, hgw, w, nw, tm, name):
    bsz, l, d = x.shape
    nt = l // tm
    mod_map = (lambda b, i: (mod_row0 + b, 0, 0)) if per_batch else (lambda b, i: (mod_row0, 0, 0))
    return pl.pallas_call(
        _outproj_kernel,
        grid=(bsz, nt),
        in_specs=[
            pl.BlockSpec((1, tm, d), lambda b, i: (b, i, 0)),
            pl.BlockSpec((1, 1, mod.shape[-1]), mod_map),
            pl.BlockSpec((1, 1, tm, 256), lambda b, i: (0, b, i, 0)),
            pl.BlockSpec((1, 1, tm, 256), lambda b, i: (1, b, i, 0)),
            pl.BlockSpec((1, tm, 256), lambda b, i: (b, i, 3)),
            pl.BlockSpec((1, 1, tm, 256), lambda b, i: (0, b, i, 0)),
            pl.BlockSpec((1, 1, tm, 256), lambda b, i: (1, b, i, 0)),
            pl.BlockSpec((1, tm, 256), lambda b, i: (b, i, 2)),
            pl.BlockSpec((1, tm, 512), lambda b, i: (b, i, 0)),
            pl.BlockSpec((1, 256), lambda b, i: (0, 0)),
            pl.BlockSpec((1, 256), lambda b, i: (0, 0)),
            pl.BlockSpec((d, d), lambda b, i: (0, 0)),
            pl.BlockSpec((1, d), lambda b, i: (0, 0)),
        ],
        out_specs=pl.BlockSpec((1, tm, d), lambda b, i: (b, i, 0)),
        out_shape=jax.ShapeDtypeStruct((bsz, l, d), f32),
        compiler_params=_cparams(("parallel", "parallel")),
        name=name,
    )(x, mod, ml_out, ml_out, ml, hg_out, hg_out, hg, da_out, mlw, hgw, w, nw)


def _ffn_kernel(*refs, moe):
    if moe:
        x_ref, mod_ref, nw1_ref, rw_ref, rb_ref, wg_ref, wu_ref, wd_ref, nw2_ref, o_ref, h_scr, acc_scr, gate_scr = refs
    else:
        x_ref, mod_ref, nw1_ref, wg_ref, wu_ref, wd_ref, nw2_ref, o_ref, h_scr, acc_scr = refs
    e = pl.program_id(2)
    j = pl.program_id(3)
    ne = pl.num_programs(2)
    nj = pl.num_programs(3)
    d = x_ref.shape[-1]
    tm = x_ref.shape[1]

    @pl.when((e == 0) & (j == 0))
    def _():
        x = x_ref[0]
        ms = jnp.mean(x * x, axis=-1, keepdims=True)
        h = x * lax.rsqrt(ms + EPS) * nw1_ref[...]
        mod = mod_ref[0]
        h = h * (1.0 + mod[:, 4 * d:5 * d]) + mod[:, 3 * d:4 * d]
        h_scr[...] = h.astype(bf16)
        acc_scr[...] = jnp.zeros_like(acc_scr)
        if moe:
            hh, hl = _split_hi_lo(h)
            rw = rw_ref[...]
            wh, wl = _split_hi_lo(rw)
            logits = _dot(hh, wh) + _dot(hl, wh) + _dot(hh, wl) + rb_ref[...]
            lane = lax.broadcasted_iota(jnp.int32, (tm, LANES), 1)
            real = lane < N_EXPERTS
            lg = jnp.where(real, logits, MASK_NEG)
            m1 = jnp.max(lg, axis=-1, keepdims=True)
            i1 = jnp.min(jnp.where(lg == m1, lane, LANES), axis=-1, keepdims=True)
            lg2 = jnp.where(lane == i1, MASK_NEG, lg)
            m2 = jnp.max(lg2, axis=-1, keepdims=True)
            i2 = jnp.min(jnp.where(lg2 == m2, lane, LANES), axis=-1, keepdims=True)
            e2 = jnp.exp(m2 - m1)
            g1 = 1.0 / (1.0 + e2)
            g2 = e2 / (1.0 + e2)
            gate_scr[...] = jnp.where(lane == i1, g1, 0.0) + jnp.where(lane == i2, g2, 0.0)

    hb = h_scr[...]
    a = _dot(hb, wg_ref[0])
    u = _dot(hb, wu_ref[0])
    act = (a * _sigmoid(a)) * u
    if moe:
        lane = lax.broadcasted_iota(jnp.int32, (tm, LANES), 1)
        ge = jnp.sum(jnp.where(lane == e, gate_scr[...], 0.0), axis=-1, keepdims=True)
        act = act * ge
    acc_scr[...] += _dot(act.astype(bf16), wd_ref[0])

    @pl.when((e == ne - 1) & (j == nj - 1))
    def _():
        y = acc_scr[...]
        ms = jnp.mean(y * y, axis=-1, keepdims=True)
        y = y * lax.rsqrt(ms + EPS) * nw2_ref[...]
        o_ref[0] = x_ref[0] + mod_ref[0][:, 5 * d:6 * d] * y


def _ffn(x, mod, mod_row0, per_batch, nw1, wg, wu, wd, nw2, tm, tf, router=None, name="ffn"):
    bsz, l, d = x.shape
    ne, _, ff = wg.shape
    nt = l // tm
    nf = ff // tf
    moe = router is not None
    mod_map = (lambda b, i, e, j: (mod_row0 + b, 0, 0)) if per_batch else (lambda b, i, e, j: (mod_row0, 0, 0))
    in_specs = [
        pl.BlockSpec((1, tm, d), lambda b, i, e, j: (b, i, 0)),
        pl.BlockSpec((1, 1, mod.shape[-1]), mod_map),
        pl.BlockSpec((1, d), lambda b, i, e, j: (0, 0)),
    ]
    args = [x, mod, nw1]
    scratch = [pltpu.VMEM((tm, d), bf16), pltpu.VMEM((tm, d), f32)]
    if moe:
        in_specs += [
            pl.BlockSpec((d, LANES), lambda b, i, e, j: (0, 0)),
            pl.BlockSpec((1, LANES), lambda b, i, e, j: (0, 0)),
        ]
        args += list(router)
        scratch += [pltpu.VMEM((tm, LANES), f32)]
    in_specs += [
        pl.BlockSpec((1, d, tf), lambda b, i, e, j: (e, 0, j)),
        pl.BlockSpec((1, d, tf), lambda b, i, e, j: (e, 0, j)),
        pl.BlockSpec((1, tf, d), lambda b, i, e, j: (e, j, 0)),
        pl.BlockSpec((1, d), lambda b, i, e, j: (0, 0)),
    ]
    args += [wg, wu, wd, nw2]
    return pl.pallas_call(
        functools.partial(_ffn_kernel, moe=moe),
        grid=(bsz, nt, ne, nf),
        in_specs=in_specs,
        out_specs=pl.BlockSpec((1, tm, d), lambda b, i, e, j: (b, i, 0)),
        out_shape=jax.ShapeDtypeStruct((bsz, l, d), f32),
        scratch_shapes=scratch,
        compiler_params=_cparams(("parallel", "parallel", "arbitrary", "arbitrary")),
        name=name,
    )(*args)


MOE_TM = 512
MOE_TS = 512
META_I1, META_I2, META_R1, META_R2, META_G1, META_G2 = range(6)


def _router_kernel(x_ref, mod_ref, nw_ref, rw_ref, rb_ref, tri_ref, h_ref, meta_ref, cnt_ref, base_scr):
    d = x_ref.shape[-1]
    tm = x_ref.shape[1]

    @pl.when((pl.program_id(0) == 0) & (pl.program_id(1) == 0))
    def _():
        base_scr[...] = jnp.zeros_like(base_scr)

    x = x_ref[0]
    ms = jnp.mean(x * x, axis=-1, keepdims=True)
    h = x * lax.rsqrt(ms + EPS) * nw_ref[...]
    mod = mod_ref[0]
    h = h * (1.0 + mod[:, 4 * d:5 * d]) + mod[:, 3 * d:4 * d]
    h_ref[0] = h
    hh, hl = _split_hi_lo(h)
    wh, wl = _split_hi_lo(rw_ref[...])
    logits = _dot(hh, wh) + _dot(hl, wh) + _dot(hh, wl) + rb_ref[...]
    lane = lax.broadcasted_iota(jnp.int32, (tm, LANES), 1)
    lg = jnp.where(lane < N_EXPERTS, logits, MASK_NEG)
    m1 = jnp.max(lg, axis=-1, keepdims=True)
    i1 = jnp.min(jnp.where(lg == m1, lane, LANES), axis=-1, keepdims=True)
    lg2 = jnp.where(lane == i1, MASK_NEG, lg)
    m2 = jnp.max(lg2, axis=-1, keepdims=True)
    i2 = jnp.min(jnp.where(lg2 == m2, lane, LANES), axis=-1, keepdims=True)
    e2 = jnp.exp(m2 - m1)
    g1 = 1.0 / (1.0 + e2)
    g2 = e2 / (1.0 + e2)
    oh = jnp.where((lane == i1) | (lane == i2), 1.0, 0.0)
    pos = base_scr[...] + _dot(tri_ref[...], oh.astype(bf16))
    r1 = jnp.sum(jnp.where(lane == i1, pos, 0.0), axis=-1, keepdims=True)
    r2 = jnp.sum(jnp.where(lane == i2, pos, 0.0), axis=-1, keepdims=True)
    meta = jnp.zeros((tm, LANES), f32)
    for k, val in ((META_I1, i1.astype(f32)), (META_I2, i2.astype(f32)), (META_R1, r1), (META_R2, r2),
                   (META_G1, g1), (META_G2, g2)):
        meta = jnp.where(lane == k, val, meta)
    meta_ref[0] = meta
    new_base = pos[tm - 1:tm, :] + oh[tm - 1:tm, :]
    base_scr[...] = new_base
    cnt_ref[...] = jnp.broadcast_to(new_base, cnt_ref.shape)


def _moe_route(x, mod, mod_row0, nw1, rw, rb, tm):
    bsz, l, d = x.shape
    nt = l // tm
    r = jnp.arange(tm)
    tri = (r[:, None] > r[None, :]).astype(bf16)
    return pl.pallas_call(
        _router_kernel,
        grid=(bsz, nt),
        in_specs=[
            pl.BlockSpec((1, tm, d), lambda b, i: (b, i, 0)),
            pl.BlockSpec((1, 1, mod.shape[-1]), lambda b, i: (mod_row0 + b, 0, 0)),
            pl.BlockSpec((1, d), lambda b, i: (0, 0)),
            pl.BlockSpec((d, LANES), lambda b, i: (0, 0)),
            pl.BlockSpec((1, LANES), lambda b, i: (0, 0)),
            pl.BlockSpec((tm, tm), lambda b, i: (0, 0)),
        ],
        out_specs=[
            pl.BlockSpec((1, tm, d), lambda b, i: (b, i, 0)),
            pl.BlockSpec((1, tm, LANES), lambda b, i: (b, i, 0)),
            pl.BlockSpec((8, LANES), lambda b, i: (0, 0)),
        ],
        out_shape=[
            jax.ShapeDtypeStruct((bsz, l, d), f32),
            jax.ShapeDtypeStruct((bsz, l, LANES), f32),
            jax.ShapeDtypeStruct((8, LANES), f32),
        ],
        scratch_shapes=[pltpu.VMEM((1, LANES), f32)],
        compiler_params=_cparams(("arbitrary", "arbitrary")),
        name="moe_route",
    )(x, mod, nw1, rw, rb, tri)


def _row_copy(src_ref, src_row, dst_ref, dst_row, sem):
    return pltpu.make_async_copy(src_ref.at[pl.ds(src_row, 1)], dst_ref.at[pl.ds(dst_row, 1)], sem)


MOE_NZERO = 2 * N_EXPERTS


def _scatter_kernel(dest_ref, h_ref, xs_ref, zero_scr, sem, zsem):
    i = pl.program_id(0)
    ts = h_ref.shape[0]
    tm = zero_scr.shape[0]
    base = i * (2 * ts)

    @pl.when(i == 0)
    def _():
        zero_scr[...] = jnp.zeros(zero_scr.shape, f32)
        zbase = pl.num_programs(0) * (2 * ts)

        def tile_copy(k):
            start = pl.multiple_of(dest_ref[zbase + k], tm)
            return pltpu.make_async_copy(zero_scr, xs_ref.at[pl.ds(start, tm)], zsem)

        def when_named(k, action):
            @pl.when(dest_ref[zbase + MOE_NZERO + k] == 1)
            def _():
                action(tile_copy(k))

        for k in range(MOE_NZERO):
            when_named(k, lambda cp: cp.start())
        for k in range(MOE_NZERO):
            when_named(k, lambda cp: cp.wait())

    def issue(r, carry):
        _row_copy(h_ref, r, xs_ref, dest_ref[base + r], sem).start(priority=0)
        _row_copy(h_ref, r, xs_ref, dest_ref[base + ts + r], sem).start(priority=1)
        return carry

    lax.fori_loop(0, ts, issue, 0, unroll=8)

    def drain(r, carry):
        _row_copy(h_ref, 0, xs_ref, 0, sem).wait()
        _row_copy(h_ref, 0, xs_ref, 0, sem).wait()
        return carry

    lax.fori_loop(0, ts, drain, 0, unroll=8)


def _moe_scatter(h2, dest, n_rows, ts, tm):
    t, d = h2.shape
    return pl.pallas_call(
        _scatter_kernel,
        grid_spec=pltpu.PrefetchScalarGridSpec(
            num_scalar_prefetch=1,
            grid=(t // ts,),
            in_specs=[pl.BlockSpec((ts, d), lambda i, dest: (i, 0))],
            out_specs=pl.BlockSpec(memory_space=pl.ANY),
            scratch_shapes=[pltpu.VMEM((tm, d), f32), pltpu.SemaphoreType.DMA, pltpu.SemaphoreType.DMA],
        ),
        out_shape=jax.ShapeDtypeStruct((n_rows, d), f32),
        compiler_params=_cparams(("arbitrary",)),
        name="moe_scatter",
    )(dest, h2)


def _expert_kernel(te_ref, tv_ref, x_ref, wg_ref, wu_ref, wd_ref, o_ref, acc_scr):
    del te_ref
    j = pl.program_id(0)
    f = pl.program_id(1)
    nf = pl.num_programs(1)
    valid = tv_ref[j] == 1

    @pl.when(valid)
    def _():
        xb = x_ref[...].astype(bf16)
        a = _dot(xb, wg_ref[0])
        u = _dot(xb, wu_ref[0])
        part = _dot(((a * _sigmoid(a)) * u).astype(bf16), wd_ref[0])

        @pl.when(f == 0)
        def _():
            acc_scr[...] = part

        @pl.when(f > 0)
        def _():
            acc_scr[...] += part

        @pl.when(f == nf - 1)
        def _():
            o_ref[...] = acc_scr[...]

    @pl.when(jnp.logical_not(valid))
    def _():
        o_ref[...] = jnp.zeros_like(o_ref)


def _moe_experts(xs, te, tv, wg, wu, wd, tm, tf):
    n_rows, d = xs.shape
    ff = wg.shape[-1]
    nf = ff // tf
    n_tiles = te.shape[0]

    def fsel(j, f, tv):
        return jnp.where(tv[j] == 1, f, nf - 1)

    return pl.pallas_call(
        _expert_kernel,
        grid_spec=pltpu.PrefetchScalarGridSpec(
            num_scalar_prefetch=2,
            grid=(n_tiles, nf),
            in_specs=[
                pl.BlockSpec((tm, d), lambda j, f, te, tv: (j, 0)),
                pl.BlockSpec((1, d, tf), lambda j, f, te, tv: (te[j], 0, fsel(j, f, tv))),
                pl.BlockSpec((1, d, tf), lambda j, f, te, tv: (te[j], 0, fsel(j, f, tv))),
                pl.BlockSpec((1, tf, d), lambda j, f, te, tv: (te[j], fsel(j, f, tv), 0)),
            ],
            out_specs=pl.BlockSpec((tm, d), lambda j, f, te, tv: (j, 0)),
            scratch_shapes=[pltpu.VMEM((tm, d), f32)],
        ),
        out_shape=jax.ShapeDtypeStruct((n_rows, d), f32),
        compiler_params=_cparams(("arbitrary", "arbitrary")),
        name="moe_experts",
    )(te, tv, xs, wg, wu, wd)


def _combine_kernel(dest_ref, x_ref, mod_ref, meta_ref, nw_ref, ys_ref, o_ref, y1_scr, y2_scr, sem):
    i = pl.program_id(0)
    ts = x_ref.shape[0]
    d = x_ref.shape[1]
    base = i * (2 * ts)

    def issue(r, carry):
        _row_copy(ys_ref, dest_ref[base + r], y1_scr, r, sem).start(priority=0)
        _row_copy(ys_ref, dest_ref[base + ts + r], y2_scr, r, sem).start(priority=1)
        return carry

    lax.fori_loop(0, ts, issue, 0, unroll=8)

    def drain(r, carry):
        _row_copy(ys_ref, 0, y1_scr, 0, sem).wait()
        _row_copy(ys_ref, 0, y2_scr, 0, sem).wait()
        return carry

    lax.fori_loop(0, ts, drain, 0, unroll=8)

    meta = meta_ref[...]
    y = meta[:, META_G1:META_G1 + 1] * y1_scr[...] + meta[:, META_G2:META_G2 + 1] * y2_scr[...]
    ms = jnp.mean(y * y, axis=-1, keepdims=True)
    y = y * lax.rsqrt(ms + EPS) * nw_ref[...]
    o_ref[...] = x_ref[...] + mod_ref[0][:, 5 * d:6 * d] * y


def _moe_combine(x2, mod, mod_row0, tokens_per_batch, meta2, nw2, ys, dest, ts):
    t, d = x2.shape
    per_b = tokens_per_batch // ts
    return pl.pallas_call(
        _combine_kernel,
        grid_spec=pltpu.PrefetchScalarGridSpec(
            num_scalar_prefetch=1,
            grid=(t // ts,),
            in_specs=[
                pl.BlockSpec((ts, d), lambda i, dest: (i, 0)),
                pl.BlockSpec((1, 1, mod.shape[-1]), lambda i, dest: (mod_row0 + i // per_b, 0, 0)),
                pl.BlockSpec((ts, LANES), lambda i, dest: (i, 0)),
                pl.BlockSpec((1, d), lambda i, dest: (0, 0)),
                pl.BlockSpec(memory_space=pl.ANY),
            ],
            out_specs=pl.BlockSpec((ts, d), lambda i, dest: (i, 0)),
            scratch_shapes=[pltpu.VMEM((ts, d), f32), pltpu.VMEM((ts, d), f32), pltpu.SemaphoreType.DMA],
        ),
        out_shape=jax.ShapeDtypeStruct((t, d), f32),
        compiler_params=_cparams(("arbitrary",)),
        name="moe_combine",
    )(dest, x2, mod, meta2, nw2, ys)


def _moe_sparse(x, mod, mod_row0, nw1, rw, rb, wg, wu, wd, nw2):
    bsz, l, d = x.shape
    t = bsz * l
    ne = wg.shape[0]
    tm_r = _pick_tile(l, (1024, 512, 256, 128))
    tm = min(MOE_TM, t)
    ts = _pick_tile(l, (MOE_TS, 256, 128))
    tf = _pick_tile(wg.shape[-1], (1792, 896, 512, 256, 128))

    h, meta, cnt = _moe_route(x, mod, mod_row0, nw1, rw, rb, tm_r)
    meta2 = meta.reshape(t, LANES)
    counts = cnt[0, :ne].astype(jnp.int32)
    tiles_e = (counts + tm - 1) // tm
    ends = jnp.cumsum(tiles_e)
    offset = (ends - tiles_e) * tm
    n_tiles = (2 * t) // tm + ne
    i1 = meta2[:, META_I1].astype(jnp.int32)
    i2 = meta2[:, META_I2].astype(jnp.int32)
    dest1 = jnp.take(offset, i1) + meta2[:, META_R1].astype(jnp.int32)
    dest2 = jnp.take(offset, i2) + meta2[:, META_R2].astype(jnp.int32)
    dest = jnp.stack([dest1.reshape(t // ts, ts), dest2.reshape(t // ts, ts)], axis=1).reshape(-1)
    jt = jnp.arange(n_tiles, dtype=jnp.int32)
    total = ends[ne - 1]
    tv = (jt < total).astype(jnp.int32)
    te_all = jnp.minimum(jnp.sum((jt[:, None] >= ends[None, :]).astype(jnp.int32), axis=1), ne - 1)
    te_last = jnp.take(te_all, jnp.maximum(total - 1, 0))
    te = jnp.where(tv == 1, te_all, te_last)
    n_rows = n_tiles * tm
    past = total + jnp.arange(ne, dtype=jnp.int32)
    zero_tiles = jnp.concatenate([jnp.maximum(ends - 1, 0), jnp.minimum(past, n_tiles - 1)])
    zero_flags = jnp.concatenate([tiles_e > 0, past < n_tiles]).astype(jnp.int32)
    dest = jnp.concatenate([dest, zero_tiles.astype(jnp.int32) * tm, zero_flags])

    xs = _moe_scatter(h.reshape(t, d), dest, n_rows, ts, tm)
    ys = _moe_experts(xs, te, tv, wg, wu, wd, tm, tf)
    out = _moe_combine(x.reshape(t, d), mod, mod_row0, l, meta2, nw2, ys, dest, ts)
    return out.reshape(bsz, l, d)


def _pick_tile(n, prefs):
    for t in prefs:
        if n % t == 0:
            return t
    return n


def kernel(x, c, ctx, c_ctx, w_mod, b_mod, norm_pre_mix, norm_post_mix, norm_pre_ffn, norm_post_ffn, w_in, w_out, ml_gate_b, ml_norm, hg_lb_logits, hg_norm, da_lambda, da_norm, ffn_w_gate, ffn_w_up, ffn_w_down, router_w, router_b, moe_w_gate, moe_w_up, moe_w_down):
    depth = w_mod.shape[0]
    bsz, l, d = x.shape
    lctx = ctx.shape[1]

    mod = _mod_vectors(c, c_ctx, w_mod, b_mod)
    rows = mod.shape[1]
    mod = mod.reshape(depth * rows, 1, 6 * d)

    lb_soft = jax.nn.softmax(hg_lb_logits.astype(f32), axis=0)
    lower_bounds = jnp.cumsum(lb_soft, axis=0) - lb_soft[0]

    tables = _rope_tables(l)
    ml_lc = min(ML_CHUNK, lctx)
    tri_ml_x = _tri_pair(min(ML_CHUNK, l), min(ML_SUB, l))
    tri_ml_c = _tri_pair(ml_lc, min(ML_SUB, ml_lc))
    tri_hg_x = _tri_pair(min(HG_GROUP, l), HG_CHUNK)
    tri_hg_c = _tri_pair(min(HG_GROUP, lctx), HG_CHUNK)
    hg_bias = _hgrn2_bias()
    ml_sel = _mlstm_sel()

    tm_x = _pick_tile(l, (512, 256, 128))
    tm_c = _pick_tile(lctx, (256, 128))
    tq = _pick_tile(l, (512, 256, 128))

    for layer in range(depth):
        need_ctx = layer < depth - 1
        lam_init = 0.8 - 0.6 * math.exp(-0.3 * layer)
        row0 = layer * rows
        w_in_l = _prep_w_in(w_in[layer])
        w_out_l = w_out[layer].astype(bf16)
        nw_pre = norm_pre_mix[layer].reshape(1, d)

        pc = _inproj(ctx, mod, row0 + bsz, False, nw_pre, w_in_l, None, tm_c)
        px = _inproj(x, mod, row0, True, nw_pre, w_in_l, tables, tm_x)
        ml_c, mlg_c, hg_c, hgf_c, da_c = pc
        ml_x, mlg_x, hg_x, hgf_x, da_x = px

        gb = ml_gate_b[layer].astype(f32)
        z = jnp.zeros((LANES - 8,), f32)
        gate_b2 = jnp.stack([jnp.concatenate([gb[0:8], z]), jnp.concatenate([gb[8:16], z])]).reshape(2, 1, LANES)
        s0 = (jnp.zeros((2, bsz, ML_H, LANES, 2 * LANES), f32), jnp.zeros((2, bsz, ML_H, 8, LANES), f32))
        mlo_c, s_ml = _mlstm(ml_c, mlg_c, gate_b2, tri_ml_c, ml_sel, s0)
        mlo_x, _ = _mlstm(ml_x, mlg_x, gate_b2, tri_ml_x, ml_sel, s_ml)

        lb = lower_bounds[layer]
        lbc = jnp.zeros((8, HG_W), f32)
        lbc = lbc.at[0].set(jnp.log(jnp.maximum(lb, LB_FLOOR))).at[1].set(jnp.log1p(-lb)).at[2].set(1.0 - lb)
        h0 = jnp.zeros((2, bsz, HG_W, HG_W // 2), f32)
        hgo_c, s_hg = _hgrn2(hg_c, hgf_c, lbc, tri_hg_c, hg_bias, h0)
        hgo_x, _ = _hgrn2(hg_x, hgf_x, lbc, tri_hg_x, hg_bias, s_hg)

        lam_p = da_lambda[layer].astype(f32)
        da_nw = da_norm[layer].reshape(1, DA_V)
        dao_x = _attn(da_x, da_c, lam_p, da_nw, lam_init, tq)

        mlw = ml_norm[layer].reshape(1, 256)
        hgw = hg_norm[layer].reshape(1, 256)
        nw_post = norm_post_mix[layer].reshape(1, d)
        x = _outproj(x, mod, row0, True, mlo_x, ml_x, hgo_x, hg_x, dao_x, mlw, hgw, w_out_l, nw_post,
                     _pick_tile(l, (1024, 512, 256, 128)), "outproj_x")
        if need_ctx:
            dao_c = _attn(da_c, da_c, lam_p, da_nw, lam_init, lctx)
            ctx = _outproj(ctx, mod, row0 + bsz, False, mlo_c, ml_c, hgo_c, hg_c, dao_c, mlw, hgw, w_out_l, nw_post, tm_c, "outproj_c")

        i = layer // 2
        nw1 = norm_pre_ffn[layer].reshape(1, d)
        nw2 = norm_post_ffn[layer].reshape(1, d)
        if layer % 2 == 0:
            wg = ffn_w_gate[i].astype(bf16)[None]
            wu = ffn_w_up[i].astype(bf16)[None]
            wd = ffn_w_down[i].astype(bf16)[None]
            router = None
        else:
            wg = moe_w_gate[i].astype(bf16)
            wu = moe_w_up[i].astype(bf16)
            wd = moe_w_down[i].astype(bf16)
            rw = jnp.zeros((d, LANES), f32).at[:, :N_EXPERTS].set(router_w[i])
            rb = jnp.zeros((1, LANES), f32).at[0, :N_EXPERTS].set(router_b[i])
            router = (rw, rb)
        ff = wg.shape[-1]
        tf = _pick_tile(ff, (896, 1408, 512, 256))
        tm_f = _pick_tile(l, (1024, 512, 256, 128) if tf <= 896 else (512, 256, 128))
        if router is None:
            x = _ffn(x, mod, row0, True, nw1, wg, wu, wd, nw2, tm_f, tf, None, name="ffn_x%d" % layer)
        else:
            x = _moe_sparse(x, mod, row0, nw1, rw, rb, wg, wu, wd, nw2)
        if need_ctx:
            ctx = _ffn(ctx, mod, row0 + bsz, False, nw1, wg, wu, wd, nw2, tm_c, tf, router, name="ffn_c%d" % layer)
    return x
```

```python
import functools
import math

import numpy as np
import jax
import jax.numpy as jnp
from jax import lax
from jax.experimental import pallas as pl
from jax.experimental.pallas import tpu as pltpu

f32 = jnp.float32
bf16 = jnp.bfloat16

ML_H = 4
HG_H = 4
DA_H = 4
GRID_W = 64
ROPE_BASE = 10000.0
N_EXPERTS = 8
EPS = 1e-6
MASK_NEG = -1e30
LB_FLOOR = 1e-30

LANES = 128
ML_CHUNK = 512
ML_SUB = 128
HG_CHUNK = 16
HG_GROUP = 512
VMEM_LIMIT = 56 * 1024 * 1024


def _cparams(sem):
    return pltpu.CompilerParams(dimension_semantics=sem, vmem_limit_bytes=VMEM_LIMIT)


def _sigmoid(x):
    return 1.0 / (1.0 + jnp.exp(-x))


def _log_sigmoid(x):
    return jnp.minimum(x, 0.0) - jnp.log(1.0 + jnp.exp(-jnp.abs(x)))


def _split_hi_lo(a):
    hi = a.astype(bf16)
    lo = (a - hi.astype(f32)).astype(bf16)
    return hi, lo


def _dot(a, b):
    return jnp.dot(a, b, preferred_element_type=f32)


def _dot_nt(a, b):
    return lax.dot_general(a, b, (((1,), (1,)), ((), ())), preferred_element_type=f32)


def _dot_tn(a, b):
    return lax.dot_general(a, b, (((0,), (0,)), ((), ())), preferred_element_type=f32)


def _dot2(a_f32, b_bf16):
    hi, lo = _split_hi_lo(a_f32)
    return _dot(hi, b_bf16) + _dot(lo, b_bf16)


def _mod_kernel(s_ref, w_ref, b_ref, o_ref):
    s = s_ref[...]
    s = s * _sigmoid(s)
    o_ref[0] = _dot(s.astype(bf16), w_ref[0].astype(bf16)) + b_ref[0]


def _mod_vectors(c, c_ctx, w_mod, b_mod):
    depth, d, n = w_mod.shape
    bsz = c.shape[0]
    rows = 8
    s = jnp.zeros((rows, d), f32).at[:bsz].set(c).at[bsz].set(c_ctx)
    tn = n // 4
    out = pl.pallas_call(
        _mod_kernel,
        grid=(depth, n // tn),
        in_specs=[
            pl.BlockSpec((rows, d), lambda l, j: (0, 0)),
            pl.BlockSpec((1, d, tn), lambda l, j: (l, 0, j)),
            pl.BlockSpec((1, 1, tn), lambda l, j: (l, 0, j)),
        ],
        out_specs=pl.BlockSpec((1, rows, tn), lambda l, j: (l, 0, j)),
        out_shape=jax.ShapeDtypeStruct((depth, rows, n), f32),
        compiler_params=_cparams(("parallel", "parallel")),
        name="mod_vectors",
    )(s, w_mod, b_mod.reshape(depth, 1, n))
    return out


P_ML = (0, 1024)
P_MLG = (1024, 1280)
P_HG = (1280, 2048)
P_HGF = (2048, 2560)
P_DA = (2560, 4096)
P_TOTAL = 4096


def _prep_w_in(w):
    d = w.shape[0]
    ml = w[:, 0:1024]
    g = w[:, 1024:1040]
    z = jnp.zeros((d, LANES - 8), w.dtype)
    gates = jnp.concatenate([g[:, 0:8], z, g[:, 8:16], z], axis=1)
    hg_q = w[:, 1040:1296]
    hg_f = w[:, 1296:1808]
    hg_i = w[:, 1808:2064]
    hg_g = w[:, 2064:2320]
    da = w[:, 2320:3856]
    return jnp.concatenate([ml, gates, hg_q, hg_i, hg_g, hg_f, da], axis=1).astype(bf16)


def _inproj_kernel(*refs, rope):
    if rope:
        x_ref, mod_ref, nw_ref, w_ref, cos_ref, sin_ref = refs[:6]
        outs = refs[6:]
    else:
        x_ref, mod_ref, nw_ref, w_ref = refs[:4]
        outs = refs[4:]
    ml_ref, mlg_ref, hg_ref, hgf_ref, da_ref = outs
    d = x_ref.shape[-1]
    x = x_ref[0]
    ms = jnp.mean(x * x, axis=-1, keepdims=True)
    h = x * lax.rsqrt(ms + EPS) * nw_ref[...]
    mod = mod_ref[0]
    h = h * (1.0 + mod[:, d:2 * d]) + mod[:, 0:d]
    hb = h.astype(bf16)

    def proj(lo, hi):
        return _dot(hb, w_ref[:, lo:hi])

    ml_ref[0] = proj(*P_ML).astype(bf16)
    mlg_ref[0] = proj(*P_MLG)
    pq = proj(P_HG[0], P_HG[0] + 256)
    hg_ref[0, :, 0:256] = (pq * _sigmoid(pq)).astype(bf16)
    hg_ref[0, :, 256:768] = proj(P_HG[0] + 256, P_HG[1]).astype(bf16)
    hgf_ref[0] = proj(*P_HGF)
    q = proj(P_DA[0], P_DA[0] + 512)
    k = proj(P_DA[0] + 512, P_DA[0] + 1024)
    if rope:
        cos = jnp.concatenate([cos_ref[...]] * 4, axis=1)
        sin = jnp.concatenate([sin_ref[...]] * 4, axis=1)
        lane = lax.broadcasted_iota(jnp.int32, (1, 512), 1)
        first = (lane % 32) < 16

        def rot(t):
            sw = jnp.where(first, pltpu.roll(t, 512 - 16, 1), pltpu.roll(t, 16, 1))
            return t * cos + sw * sin

        q = rot(q)
        k = rot(k)
    da_ref[0, :, 0:512] = (q * 0.125).astype(bf16)
    da_ref[0, :, 512:1024] = k.astype(bf16)
    da_ref[0, :, 1024:1536] = proj(P_DA[0] + 1024, P_DA[1]).astype(bf16)


def _inproj(x, mod, mod_row0, per_batch, nw, w, tables, tm):
    bsz, l, d = x.shape
    rope = tables is not None
    nt = l // tm
    mod_map = (lambda b, i: (mod_row0 + b, 0, 0)) if per_batch else (lambda b, i: (mod_row0, 0, 0))
    in_specs = [
        pl.BlockSpec((1, tm, d), lambda b, i: (b, i, 0)),
        pl.BlockSpec((1, 1, mod.shape[-1]), mod_map),
        pl.BlockSpec((1, d), lambda b, i: (0, 0)),
        pl.BlockSpec((d, P_TOTAL), lambda b, i: (0, 0)),
    ]
    args = [x, mod, nw, w]
    if rope:
        in_specs += [pl.BlockSpec((tm, LANES), lambda b, i: (i, 0))] * 2
        args += list(tables)
    widths = (1024, 256, 768, 512, 1536)
    dtypes = (bf16, f32, bf16, f32, bf16)
    out_specs = [pl.BlockSpec((1, tm, wd), lambda b, i: (b, i, 0)) for wd in widths]
    out_shape = [jax.ShapeDtypeStruct((bsz, l, wd), dt) for wd, dt in zip(widths, dtypes)]
    return pl.pallas_call(
        functools.partial(_inproj_kernel, rope=rope),
        grid=(bsz, nt),
        in_specs=in_specs,
        out_specs=out_specs,
        out_shape=out_shape,
        compiler_params=_cparams(("parallel", "parallel")),
        name="inproj_x" if rope else "inproj_c",
    )(*args)


def _rope_tables(l):
    rows = l // GRID_W
    row = jnp.repeat(jnp.arange(rows), GRID_W).astype(f32)
    col = jnp.tile(jnp.arange(GRID_W), rows).astype(f32)
    n_freq = 16
    inv = ROPE_BASE ** (-jnp.arange(n_freq, dtype=f32) / n_freq)
    lane = jnp.arange(LANES)
    freq = inv[lane % n_freq]
    pos = jnp.where(((lane % 64) // 32 == 0)[None, :], row[:, None], col[:, None])
    ang = pos * freq[None, :]
    sign = jnp.where((lane % 32) < 16, -1.0, 1.0)[None, :]
    return jnp.cos(ang), jnp.sin(ang) * sign


ML_D = 64


def _mlstm_kernel(q_ref, k_ref, v_ref, g_ref, gb_ref, tri_ref, sel_ref, cn0_ref, m0_ref,
                  o_ref, cnf_ref, mf_ref, cn_scr, m_scr):
    d = pl.program_id(0)
    j = pl.program_id(2)
    nj = pl.num_programs(2)
    lc = q_ref.shape[1]

    @pl.when(j == 0)
    def _():
        cn_scr[...] = cn0_ref[0, 0]
        m_scr[...] = m0_ref[0, 0]

    sub = min(ML_SUB, lc)
    nsub = lc // sub

    def run(order, fwd):
        g = g_ref[0] + gb_ref[0]
        lf = _log_sigmoid(g)
        hi, lo = _split_hi_lo(lf)
        bc = _dot(tri_ref[0], hi) + _dot(tri_ref[0], lo)
        z = pltpu.roll(g, 4, 1) - bc
        zt = z.T
        sel = sel_ref[...]
        yrep = _dot2(bc, sel)
        zrep = _dot2(z, sel)
        r = lax.broadcasted_iota(jnp.int32, (sub, sub), 0)
        c = lax.broadcasted_iota(jnp.int32, (sub, sub), 1)
        lane = lax.broadcasted_iota(jnp.int32, (sub, LANES), 1)
        ones_b = jnp.ones((sub, LANES), bf16)
        mask = (r >= c) if fwd else (r <= c)
        last = sub - 1 if fwd else 0

        def rep(x, n):
            return jnp.concatenate([x] * n, axis=1)

        pre = {}
        for cs in order:
            rows = slice(cs * sub, (cs + 1) * sub)
            for h in range(ML_H):
                p, hh = divmod(h, 2)
                half = (lane < ML_D) if hh == 0 else (lane >= ML_D)
                qp = q_ref[0, rows, p * LANES:(p + 1) * LANES]
                kp = k_ref[0, rows, p * LANES:(p + 1) * LANES] * jnp.asarray(0.125, bf16)
                vp = v_ref[0, rows, p * LANES:(p + 1) * LANES]
                b_rep = yrep[rows, h * LANES:(h + 1) * LANES]
                blast = b_rep[last:last + 1, :]
                qm = jnp.where(half, qp, jnp.zeros_like(qp))
                vo = jnp.concatenate([jnp.where(half, vp, jnp.zeros_like(vp)), ones_b], axis=1)
                s_raw = _dot_nt(qm, kp)
                a_rep = blast + zrep[rows, h * LANES:(h + 1) * LANES]
                m_loc = jnp.max(a_rep, axis=0, keepdims=True)
                w = jnp.exp(a_rep - m_loc)
                kw = jnp.where(half, kp.astype(f32) * w, 0.0).astype(bf16)
                dcn = _dot_tn(kw, vo)
                pre[cs, h] = (b_rep, blast, qm, vo, s_raw, m_loc, dcn)

        for cs in order:
            rows = slice(cs * sub, (cs + 1) * sub)
            outs = []
            for h in range(ML_H):
                b_rep, blast, qm, vo, s_raw, m_loc, dcn = pre.pop((cs, h))
                m0 = m_scr[h, 0:1, :]
                cn = cn_scr[h]
                nd_inter = _dot(qm, cn.astype(bf16))
                z_row = zt[4 + h:5 + h, rows]
                dm = jnp.where(mask, rep(b_rep, sub // LANES) + z_row, MASK_NEG)
                inter = b_rep + m0
                m = jnp.maximum(inter, jnp.max(dm, axis=-1, keepdims=True))
                w_intra = jnp.exp(dm - rep(m, sub // LANES))
                w_inter = jnp.exp(inter - m)
                nd = _dot((s_raw * w_intra).astype(bf16), vo) + rep(w_inter, 2) * nd_inter
                den = nd[:, LANES:2 * LANES]
                outs.append(nd[:, 0:LANES] / jnp.maximum(jnp.abs(den), jnp.exp(-m)))

                m_new = jnp.maximum(blast + m0, m_loc)
                s_old = jnp.exp(blast + m0 - m_new)
                s_new = jnp.exp(m_loc - m_new)
                cn_scr[h] = rep(s_old, 2) * cn + rep(s_new, 2) * dcn
                m_scr[h] = jnp.broadcast_to(m_new, (8, LANES))
            for p in range(ML_H // 2):
                o_ref[0, 0, rows, p * LANES:(p + 1) * LANES] = outs[2 * p] + outs[2 * p + 1]

    @pl.when(d == 0)
    def _():
        run(range(nsub), True)

    @pl.when(d == 1)
    def _():
        run(range(nsub - 1, -1, -1), False)

    @pl.when(j == nj - 1)
    def _():
        cnf_ref[0, 0] = cn_scr[...]
        mf_ref[0, 0] = m_scr[...]


def _mlstm_sel():
    sel = np.zeros((LANES, ML_H * LANES), np.float32)
    for h in range(ML_H):
        sel[4 + h, h * LANES:(h + 1) * LANES] = 1.0
    return jnp.asarray(sel, dtype=bf16)


def _mlstm(ml, mlg, gate_b2, tri, sel, state):
    bsz, l, _ = ml.shape
    lc = min(ML_CHUNK, l)
    nj = l // lc
    cn0, m0 = state

    def tmap(col):
        return lambda d, b, j: (b, jnp.where(d == 0, j, nj - 1 - j), col)

    cn_spec = pl.BlockSpec((1, 1, ML_H, LANES, 2 * LANES), lambda d, b, j: (d, b, 0, 0, 0))
    m_spec = pl.BlockSpec((1, 1, ML_H, 8, LANES), lambda d, b, j: (d, b, 0, 0, 0))
    out, cnf, mf = pl.pallas_call(
        _mlstm_kernel,
        grid=(2, bsz, nj),
        in_specs=[
            pl.BlockSpec((1, lc, 256), tmap(0)),
            pl.BlockSpec((1, lc, 256), tmap(1)),
            pl.BlockSpec((1, lc, 256), tmap(2)),
            pl.BlockSpec((1, lc, LANES), lambda d, b, j: (b, jnp.where(d == 0, j, nj - 1 - j), d)),
            pl.BlockSpec((1, 1, LANES), lambda d, b, j: (d, 0, 0)),
            pl.BlockSpec((1, lc, lc), lambda d, b, j: (d, 0, 0)),
            pl.BlockSpec((LANES, ML_H * LANES), lambda d, b, j: (0, 0)),
            cn_spec,
            m_spec,
        ],
        out_specs=[
            pl.BlockSpec((1, 1, lc, 256), lambda d, b, j: (d, b, jnp.where(d == 0, j, nj - 1 - j), 0)),
            cn_spec,
            m_spec,
        ],
        out_shape=[
            jax.ShapeDtypeStruct((2, bsz, l, 256), f32),
            jax.ShapeDtypeStruct((2, bsz, ML_H, LANES, 2 * LANES), f32),
            jax.ShapeDtypeStruct((2, bsz, ML_H, 8, LANES), f32),
        ],
        scratch_shapes=[pltpu.VMEM((ML_H, LANES, 2 * LANES), f32), pltpu.VMEM((ML_H, 8, LANES), f32)],
        compiler_params=_cparams(("parallel", "parallel", "arbitrary")),
        name="mlstm",
    )(ml, ml, ml, mlg, gate_b2, tri, sel, cn0, m0)
    return out, (cnf, mf)


def _tri_pair(n, block):
    r = np.arange(n)[:, None]
    c = np.arange(n)[None, :]
    same = (r // block) == (c // block)
    lower = (same & (r >= c)).astype(np.float32)
    upper = (same & (r <= c)).astype(np.float32)
    return jnp.asarray(np.stack([lower, upper]), dtype=bf16)


HG_W = 256
HG_FACTOR_MAX = 80.0


def _hgrn2_kernel(q_ref, v_ref, f_ref, lbc_ref, tri_ref, bias_ref, s0_ref, o_ref, sf_ref, st_ref, p_scr):
    d = pl.program_id(0)
    j = pl.program_id(2)
    nj = pl.num_programs(2)
    lg = q_ref.shape[1]
    nck = lg // HG_CHUNK
    hw = HG_W // 2

    @pl.when(j == 0)
    def _():
        st_ref[...] = s0_ref[0, 0]

    fr = f_ref[0]
    log_lb = lbc_ref[0:1, :]
    log_1m_lb = lbc_ref[1:2, :]
    one_m_lb = lbc_ref[2:3, :]
    t = jnp.exp(-jnp.abs(fr))
    b = log_1m_lb + (jnp.minimum(fr, 0.0) - jnp.log(1.0 + t))
    kk = one_m_lb * (jnp.where(fr >= 0.0, t, 1.0) / (1.0 + t))
    log_f = jnp.maximum(log_lb, b) + jnp.log(1.0 + jnp.exp(-jnp.abs(log_lb - b)))
    hi, lo = _split_hi_lo(log_f)
    g = _dot(tri_ref[0], hi) + _dot(tri_ref[0], lo)
    q = q_ref[0].astype(f32)
    qe = (q * jnp.exp(g)).astype(bf16)

    rr = lax.broadcasted_iota(jnp.int32, (hw, hw), 0) // 64
    cc = lax.broadcasted_iota(jnp.int32, (hw, hw), 1) // 64
    same_head = rr == cc
    bd_b = jnp.where(same_head, 1.0, 0.0).astype(bf16)

    sub = 8
    nsub = HG_CHUNK // sub

    lane_head = lax.broadcasted_iota(jnp.int32, (HG_CHUNK, HG_W), 1) // 64
    row_t = lax.broadcasted_iota(jnp.int32, (HG_H * HG_CHUNK, HG_CHUNK), 0) % HG_CHUNK
    col_s = lax.broadcasted_iota(jnp.int32, (HG_H * HG_CHUNK, HG_CHUNK), 1)

    def run(order, last_row, fwd, factored):
        causal = (row_t >= col_s) if fwd else (row_t <= col_s)

        def block_kind(s, tb):
            lo, hi = tb * sub, tb * sub + sub - 1
            if fwd:
                return "all" if lo >= s else ("none" if hi < s else "some")
            return "all" if hi <= s else ("none" if lo > s else "some")

        def build_p(c):
            r0 = c * HG_CHUNK
            gc = g[r0:r0 + HG_CHUNK]
            kc = kk[r0:r0 + HG_CHUNK]
            qc = q[r0:r0 + HG_CHUNK]
            for s in range(HG_CHUNK):
                blocks = []
                for tb in range(nsub):
                    kind = block_kind(s, tb)
                    if kind == "none":
                        blocks.append(jnp.zeros((sub, HG_W), f32))
                        continue
                    diff = gc[tb * sub:(tb + 1) * sub] - gc[s:s + 1, :]
                    if kind == "some":
                        diff = diff + bias_ref[0, s * HG_CHUNK + tb * sub:s * HG_CHUNK + (tb + 1) * sub, :]
                    blocks.append(qc[tb * sub:(tb + 1) * sub] * kc[s:s + 1, :] * jnp.exp(diff))
                p_scr[c, s * HG_CHUNK:(s + 1) * HG_CHUNK, :] = jnp.concatenate(blocks, axis=0).astype(bf16)

        def intra_direct(c):
            r0 = c * HG_CHUNK
            vc = v_ref[0, r0:r0 + HG_CHUNK, :].astype(f32)
            abc = jnp.concatenate([_dot(p_scr[c, :, 0:hw], bd_b), _dot(p_scr[c, :, hw:HG_W], bd_b)], axis=1)
            o_blocks = []
            for tb in range(nsub):
                acc = None
                for s in range(HG_CHUNK):
                    if block_kind(s, tb) == "none":
                        continue
                    term = abc[s * HG_CHUNK + tb * sub:s * HG_CHUNK + (tb + 1) * sub, :] * vc[s:s + 1, :]
                    acc = term if acc is None else acc + term
                o_blocks.append(acc)
            return jnp.concatenate(o_blocks, axis=0)

        def scores_factored(c):
            r0 = c * HG_CHUNK
            qec = qe[r0:r0 + HG_CHUNK]
            kx = (kk[r0:r0 + HG_CHUNK] * jnp.exp(-g[r0:r0 + HG_CHUNK])).astype(bf16)
            qs = jnp.concatenate([jnp.where(lane_head == h, qec, jnp.zeros_like(qec)) for h in range(HG_H)], axis=0)
            return jnp.where(causal, _dot_nt(qs, kx), 0.0).astype(bf16)

        def intra_factored(c, a):
            o_all = _dot(a, v_ref[0, c * HG_CHUNK:(c + 1) * HG_CHUNK, :])
            o = None
            for h in range(HG_H):
                term = jnp.where(lane_head == h, o_all[h * HG_CHUNK:(h + 1) * HG_CHUNK], 0.0)
                o = term if o is None else o + term
            return o

        def state_delta(c):
            r0 = c * HG_CHUNK
            gc = g[r0:r0 + HG_CHUNK]
            vcb = v_ref[0, r0:r0 + HG_CHUNK, :]
            g_last = gc[last_row:last_row + 1, :]
            kt = (kk[r0:r0 + HG_CHUNK] * jnp.exp(g_last - gc)).astype(bf16)
            return (jnp.exp(g_last), _dot_tn(vcb[:, 0:hw], kt[:, 0:hw]), _dot_tn(vcb[:, hw:HG_W], kt[:, hw:HG_W]))

        def finish(c, o, delta):
            r0 = c * HG_CHUNK
            dec, d_lo, d_hi = delta
            st = st_ref[...]
            stb = st.astype(bf16)
            qec = qe[r0:r0 + HG_CHUNK]
            o = o + jnp.concatenate([_dot_nt(qec[:, 0:hw], stb[0:hw]), _dot_nt(qec[:, hw:HG_W], stb[hw:HG_W])], axis=1)
            o_ref[0, 0, r0:r0 + HG_CHUNK, :] = o
            st_ref[0:hw, :] = st[0:hw] * dec[:, 0:hw] + jnp.where(same_head, d_lo, 0.0)
            st_ref[hw:HG_W, :] = st[hw:HG_W] * dec[:, hw:HG_W] + jnp.where(same_head, d_hi, 0.0)

        order = list(order)
        if factored:
            n = len(order)
            stage_a, stage_b = {}, {}
            for i in range(n + 2):
                if i < n:
                    stage_a[order[i]] = (scores_factored(order[i]), state_delta(order[i]))
                if 0 <= i - 1 < n:
                    c = order[i - 1]
                    stage_b[c] = intra_factored(c, stage_a[c][0])
                if 0 <= i - 2 < n:
                    c = order[i - 2]
                    finish(c, stage_b.pop(c), stage_a.pop(c)[1])
        else:
            for c in order:
                build_p(c)
                finish(c, intra_direct(c), state_delta(c))

    can_factor = jnp.max(-g) < HG_FACTOR_MAX
    for direction, (order, last_row) in enumerate(((range(nck), HG_CHUNK - 1), (range(nck - 1, -1, -1), 0))):
        for factored in (True, False):
            @pl.when((d == direction) & (can_factor if factored else jnp.logical_not(can_factor)))
            def _(order=order, last_row=last_row, direction=direction, factored=factored):
                run(order, last_row, direction == 0, factored)

    @pl.when(j == nj - 1)
    def _():
        sf_ref[0, 0] = st_ref[...]


def _hgrn2(hg, hgf, lbc, tri, bias, s0):
    bsz, l, _ = hg.shape
    lg = min(HG_GROUP, l)
    nj = l // lg
    hw = HG_W // 2

    def tmap(col):
        return lambda d, b, j: (b, jnp.where(d == 0, j, nj - 1 - j), col)

    out, sf = pl.pallas_call(
        _hgrn2_kernel,
        grid=(2, bsz, nj),
        in_specs=[
            pl.BlockSpec((1, lg, HG_W), tmap(0)),
            pl.BlockSpec((1, lg, HG_W), tmap(1)),
            pl.BlockSpec((1, lg, HG_W), lambda d, b, j: (b, jnp.where(d == 0, j, nj - 1 - j), d)),
            pl.BlockSpec((8, HG_W), lambda d, b, j: (0, 0)),
            pl.BlockSpec((1, lg, lg), lambda d, b, j: (d, 0, 0)),
            pl.BlockSpec((1, HG_CHUNK * HG_CHUNK, HG_W), lambda d, b, j: (d, 0, 0)),
            pl.BlockSpec((1, 1, HG_W, hw), lambda d, b, j: (d, b, 0, 0)),
        ],
        out_specs=[
            pl.BlockSpec((1, 1, lg, HG_W), lambda d, b, j: (d, b, jnp.where(d == 0, j, nj - 1 - j), 0)),
            pl.BlockSpec((1, 1, HG_W, hw), lambda d, b, j: (d, b, 0, 0)),
        ],
        out_shape=[
            jax.ShapeDtypeStruct((2, bsz, l, HG_W), f32),
            jax.ShapeDtypeStruct((2, bsz, HG_W, hw), f32),
        ],
        scratch_shapes=[
            pltpu.VMEM((HG_W, hw), f32),
            pltpu.VMEM((lg // HG_CHUNK, HG_CHUNK * HG_CHUNK, HG_W), bf16),
        ],
        compiler_params=_cparams(("parallel", "parallel", "arbitrary")),
        name="hgrn2",
    )(hg, hg, hgf, lbc, tri, bias, s0)
    return out, sf


def _hgrn2_bias():
    idx = np.arange(HG_CHUNK * HG_CHUNK)
    s, t = idx // HG_CHUNK, idx % HG_CHUNK
    fwd = np.where(t >= s, 0.0, MASK_NEG).astype(np.float32)
    bwd = np.where(t <= s, 0.0, MASK_NEG).astype(np.float32)
    return jnp.asarray(np.broadcast_to(np.stack([fwd, bwd])[:, :, None], (2, HG_CHUNK * HG_CHUNK, HG_W)))


DA_V = 128


def _attn_kernel(*refs, has_x, lam_init, kg, nq):
    if has_x:
        (q_ref, kx_ref, kc_ref, vx_ref, vc_ref, lp_ref, nw_ref, o_ref,
         s0_scr, s1_scr, m0_scr, m1_scr, a0_scr, a1_scr, va_scr) = refs
        lx = kx_ref.shape[1]
    else:
        q_ref, kc_ref, vc_ref, lp_ref, nw_ref, o_ref, s0_scr, s1_scr, m0_scr, m1_scr, a0_scr, a1_scr, va_scr = refs
        lx = 0
    tq = q_ref.shape[1]
    lctx = kc_ref.shape[1]
    t = pl.program_id(0)

    @pl.when(t == 0)
    def _():
        s1_scr[...] = jnp.zeros(s1_scr.shape, f32)
        m1_scr[...] = jnp.zeros(m1_scr.shape, f32)
        a1_scr[...] = jnp.ones(a1_scr.shape, f32)

    @pl.when((t == 0) | ((t - 1) % nq == 0))
    def _():
        def ones_col(n):
            lane = lax.broadcasted_iota(jnp.int32, (n, DA_V), 1)
            return jnp.where(lane == 0, 1.0, 0.0).astype(bf16)

        if has_x:
            va_scr[0:lx, 0:DA_V] = vx_ref[0]
            va_scr[0:lx, DA_V:2 * DA_V] = ones_col(lx)
        va_scr[lx:lx + lctx, 0:DA_V] = vc_ref[0]
        va_scr[lx:lx + lctx, DA_V:2 * DA_V] = ones_col(lctx)

    groups = ([(kx_ref, g * kg, kg, g * kg) for g in range(lx // kg)] if has_x else []) + [(kc_ref, 0, lctx, lx)]

    def step(sa_scr, ma_scr, aa_scr, sb_scr, mb_scr, ab_scr):
        q = q_ref[0]
        lane = lax.broadcasted_iota(jnp.int32, (tq, DA_V), 1)
        zero = jnp.zeros_like(q)
        qq = jnp.concatenate([jnp.where(lane < 64, q, zero), jnp.where(lane >= 64, q, zero)], axis=0)
        m128 = None
        for ref, r0, n, c0 in groups:
            s = _dot_nt(qq, ref[0, r0:r0 + n, :])
            sa_scr[:, c0:c0 + n] = s
            for c in range(n // LANES):
                blk = s[:, c * LANES:(c + 1) * LANES]
                m128 = blk if m128 is None else jnp.maximum(m128, blk)
        ma_scr[...] = jnp.broadcast_to(jnp.max(m128, axis=-1, keepdims=True), m128.shape)
        m_prev = mb_scr[...]
        acc = None
        for ref, r0, n, c0 in groups:
            m_rep = jnp.concatenate([m_prev] * (n // LANES), axis=1)
            p = jnp.exp(sb_scr[:, c0:c0 + n] - m_rep).astype(bf16)
            part = _dot(p, va_scr[c0:c0 + n, :])
            acc = part if acc is None else acc + part
        aa_scr[...] = acc
        acc = ab_scr[:, 0:DA_V]
        l = ab_scr[:, DA_V:DA_V + 1]
        lp = lp_ref[...]
        lam = (jnp.exp(jnp.sum(lp[0:1] * lp[1:2], axis=-1, keepdims=True))
               - jnp.exp(jnp.sum(lp[2:3] * lp[3:4], axis=-1, keepdims=True)) + lam_init)
        o = acc[0:tq] / l[0:tq] - lam * (acc[tq:2 * tq] / l[tq:2 * tq])
        ms = jnp.mean(o * o, axis=-1, keepdims=True)
        o_ref[0] = (o * lax.rsqrt(ms + EPS) * nw_ref[...] * (1.0 - lam_init)).astype(bf16)

    @pl.when(t % 2 == 0)
    def _():
        step(s0_scr, m0_scr, a0_scr, s1_scr, m1_scr, a1_scr)

    @pl.when(t % 2 == 1)
    def _():
        step(s1_scr, m1_scr, a1_scr, s0_scr, m0_scr, a0_scr)


def _attn(da_q, da_c, lam_p, nw, lam_init, tq):
    has_x = da_q is not da_c
    bsz, l, _ = da_q.shape
    lctx = da_c.shape[1]
    nq = l // tq
    nt = bsz * DA_H * nq
    lx = l if has_x else 0
    kg = _pick_tile(l, (1024, 512, 256, 128))

    def split(t):
        return t // (DA_H * nq), (t // nq) % DA_H, t % nq

    def cur(col0, whole):
        def index(t):
            b, h, i = split(jnp.minimum(t, nt - 1))
            return (b, 0 if whole else i, col0 + h)
        return index

    def prev(col0, whole, lag=1):
        def index(t):
            b, h, i = split(jnp.clip(t - lag, 0, nt - 1))
            return (b, 0 if whole else i, col0 + h)
        return index

    in_specs = [pl.BlockSpec((1, tq, DA_V), cur(0, False))]
    args = [da_q]
    if has_x:
        in_specs += [pl.BlockSpec((1, l, DA_V), cur(4, True))]
        args += [da_q]
    in_specs += [pl.BlockSpec((1, lctx, DA_V), cur(4, True))]
    args += [da_c]
    if has_x:
        in_specs += [pl.BlockSpec((1, l, DA_V), prev(8, True))]
        args += [da_q]
    in_specs += [
        pl.BlockSpec((1, lctx, DA_V), prev(8, True)),
        pl.BlockSpec((4, 64), lambda t: (0, 0)),
        pl.BlockSpec((1, DA_V), lambda t: (0, 0)),
    ]
    args += [da_c, lam_p, nw]
    return pl.pallas_call(
        functools.partial(_attn_kernel, has_x=has_x, lam_init=lam_init, kg=kg, nq=nq),
        grid=(nt + 2,),
        in_specs=in_specs,
        out_specs=pl.BlockSpec((1, tq, DA_V), prev(0, False, lag=2)),
        out_shape=jax.ShapeDtypeStruct((bsz, l, DA_H * DA_V), bf16),
        scratch_shapes=[
            pltpu.VMEM((2 * tq, lx + lctx), f32),
            pltpu.VMEM((2 * tq, lx + lctx), f32),
            pltpu.VMEM((2 * tq, LANES), f32),
            pltpu.VMEM((2 * tq, LANES), f32),
            pltpu.VMEM((2 * tq, 2 * DA_V), f32),
            pltpu.VMEM((2 * tq, 2 * DA_V), f32),
            pltpu.VMEM((lx + lctx, 2 * DA_V), bf16),
        ],
        compiler_params=_cparams(("arbitrary",)),
        name="attn_x" if has_x else "attn_c",
    )(*args)


def _head_ms(y, bd_b):
    return _dot2(y * y, bd_b) * (1.0 / 64.0)


def _outproj_kernel(x_ref, mod_ref, mlf_ref, mlb_ref, mlo_ref, hgf_ref, hgb_ref, hgg_ref, da_ref,
                    mlw_ref, hgw_ref, w_ref, nw_ref, o_ref):
    d = x_ref.shape[-1]
    rr = lax.broadcasted_iota(jnp.int32, (256, 256), 0) // 64
    cc = lax.broadcasted_iota(jnp.int32, (256, 256), 1) // 64
    bd_b = jnp.where(rr == cc, 1.0, 0.0).astype(bf16)
    ml = mlf_ref[0, 0] + mlb_ref[0, 0]
    ml = ml * lax.rsqrt(_head_ms(ml, bd_b) + EPS) * mlw_ref[...]
    ml = _sigmoid(mlo_ref[0].astype(f32)) * ml
    hg = hgf_ref[0, 0] + hgb_ref[0, 0]
    hg = hg * lax.rsqrt(_head_ms(hg, bd_b) + EPS) * hgw_ref[...]
    gg = hgg_ref[0].astype(f32)
    hg = hg * (gg * _sigmoid(gg))
    mix = (_dot(ml.astype(bf16), w_ref[0:256, :]) + _dot(hg.astype(bf16), w_ref[256:512, :])
           + _dot(da_ref[0], w_ref[512:1024, :]))
    ms = jnp.mean(mix * mix, axis=-1, keepdims=True)
    y = mix * lax.rsqrt(ms + EPS) * nw_ref[...]
    o_ref[0] = x_ref[0] + mod_ref[0][:, 2 * d:3 * d] * y


def _outproj(x, mod, mod_row0, per_batch, ml_out, ml, hg_out, hg, da_out, mlw, hgw, w, nw, tm, name):
    bsz, l, d = x.shape
    nt = l // tm
    mod_map = (lambda b, i: (mod_row0 + b, 0, 0)) if per_batch else (lambda b, i: (mod_row0, 0, 0))
    return pl.pallas_call(
        _outproj_kernel,
        grid=(bsz, nt),
        in_specs=[
            pl.BlockSpec((1, tm, d), lambda b, i: (b, i, 0)),
            pl.BlockSpec((1, 1, mod.shape[-1]), mod_map),
            pl.BlockSpec((1, 1, tm, 256), lambda b, i: (0, b, i, 0)),
            pl.BlockSpec((1, 1, tm, 256), lambda b, i: (1, b, i, 0)),
            pl.BlockSpec((1, tm, 256), lambda b, i: (b, i, 3)),
            pl.BlockSpec((1, 1, tm, 256), lambda b, i: (0, b, i, 0)),
            pl.BlockSpec((1, 1, tm, 256), lambda b, i: (1, b, i, 0)),
            pl.BlockSpec((1, tm, 256), lambda b, i: (b, i, 2)),
            pl.BlockSpec((1, tm, 512), lambda b, i: (b, i, 0)),
            pl.BlockSpec((1, 256), lambda b, i: (0, 0)),
            pl.BlockSpec((1, 256), lambda b, i: (0, 0)),
            pl.BlockSpec((d, d), lambda b, i: (0, 0)),
            pl.BlockSpec((1, d), lambda b, i: (0, 0)),
        ],
        out_specs=pl.BlockSpec((1, tm, d), lambda b, i: (b, i, 0)),
        out_shape=jax.ShapeDtypeStruct((bsz, l, d), f32),
        compiler_params=_cparams(("parallel", "parallel")),
        name=name,
    )(x, mod, ml_out, ml_out, ml, hg_out, hg_out, hg, da_out, mlw, hgw, w, nw)


def _ffn_kernel(*refs, moe):
    if moe:
        x_ref, mod_ref, nw1_ref, rw_ref, rb_ref, wg_ref, wu_ref, wd_ref, nw2_ref, o_ref, h_scr, acc_scr, gate_scr = refs
    else:
        x_ref, mod_ref, nw1_ref, wg_ref, wu_ref, wd_ref, nw2_ref, o_ref, h_scr, acc_scr = refs
    e = pl.program_id(2)
    j = pl.program_id(3)
    ne = pl.num_programs(2)
    nj = pl.num_programs(3)
    d = x_ref.shape[-1]
    tm = x_ref.shape[1]

    @pl.when((e == 0) & (j == 0))
    def _():
        x = x_ref[0]
        ms = jnp.mean(x * x, axis=-1, keepdims=True)
        h = x * lax.rsqrt(ms + EPS) * nw1_ref[...]
        mod = mod_ref[0]
        h = h * (1.0 + mod[:, 4 * d:5 * d]) + mod[:, 3 * d:4 * d]
        h_scr[...] = h.astype(bf16)
        acc_scr[...] = jnp.zeros_like(acc_scr)
        if moe:
            hh, hl = _split_hi_lo(h)
            rw = rw_ref[...]
            wh, wl = _split_hi_lo(rw)
            logits = _dot(hh, wh) + _dot(hl, wh) + _dot(hh, wl) + rb_ref[...]
            lane = lax.broadcasted_iota(jnp.int32, (tm, LANES), 1)
            real = lane < N_EXPERTS
            lg = jnp.where(real, logits, MASK_NEG)
            m1 = jnp.max(lg, axis=-1, keepdims=True)
            i1 = jnp.min(jnp.where(lg == m1, lane, LANES), axis=-1, keepdims=True)
            lg2 = jnp.where(lane == i1, MASK_NEG, lg)
            m2 = jnp.max(lg2, axis=-1, keepdims=True)
            i2 = jnp.min(jnp.where(lg2 == m2, lane, LANES), axis=-1, keepdims=True)
            e2 = jnp.exp(m2 - m1)
            g1 = 1.0 / (1.0 + e2)
            g2 = e2 / (1.0 + e2)
            gate_scr[...] = jnp.where(lane == i1, g1, 0.0) + jnp.where(lane == i2, g2, 0.0)

    hb = h_scr[...]
    a = _dot(hb, wg_ref[0])
    u = _dot(hb, wu_ref[0])
    act = (a * _sigmoid(a)) * u
    if moe:
        lane = lax.broadcasted_iota(jnp.int32, (tm, LANES), 1)
        ge = jnp.sum(jnp.where(lane == e, gate_scr[...], 0.0), axis=-1, keepdims=True)
        act = act * ge
    acc_scr[...] += _dot(act.astype(bf16), wd_ref[0])

    @pl.when((e == ne - 1) & (j == nj - 1))
    def _():
        y = acc_scr[...]
        ms = jnp.mean(y * y, axis=-1, keepdims=True)
        y = y * lax.rsqrt(ms + EPS) * nw2_ref[...]
        o_ref[0] = x_ref[0] + mod_ref[0][:, 5 * d:6 * d] * y


def _ffn(x, mod, mod_row0, per_batch, nw1, wg, wu, wd, nw2, tm, tf, router=None, name="ffn"):
    bsz, l, d = x.shape
    ne, _, ff = wg.shape
    nt = l // tm
    nf = ff // tf
    moe = router is not None
    mod_map = (lambda b, i, e, j: (mod_row0 + b, 0, 0)) if per_batch else (lambda b, i, e, j: (mod_row0, 0, 0))
    in_specs = [
        pl.BlockSpec((1, tm, d), lambda b, i, e, j: (b, i, 0)),
        pl.BlockSpec((1, 1, mod.shape[-1]), mod_map),
        pl.BlockSpec((1, d), lambda b, i, e, j: (0, 0)),
    ]
    args = [x, mod, nw1]
    scratch = [pltpu.VMEM((tm, d), bf16), pltpu.VMEM((tm, d), f32)]
    if moe:
        in_specs += [
            pl.BlockSpec((d, LANES), lambda b, i, e, j: (0, 0)),
            pl.BlockSpec((1, LANES), lambda b, i, e, j: (0, 0)),
        ]
        args += list(router)
        scratch += [pltpu.VMEM((tm, LANES), f32)]
    in_specs += [
        pl.BlockSpec((1, d, tf), lambda b, i, e, j: (e, 0, j)),
        pl.BlockSpec((1, d, tf), lambda b, i, e, j: (e, 0, j)),
        pl.BlockSpec((1, tf, d), lambda b, i, e, j: (e, j, 0)),
        pl.BlockSpec((1, d), lambda b, i, e, j: (0, 0)),
    ]
    args += [wg, wu, wd, nw2]
    return pl.pallas_call(
        functools.partial(_ffn_kernel, moe=moe),
        grid=(bsz, nt, ne, nf),
        in_specs=in_specs,
        out_specs=pl.BlockSpec((1, tm, d), lambda b, i, e, j: (b, i, 0)),
        out_shape=jax.ShapeDtypeStruct((bsz, l, d), f32),
        scratch_shapes=scratch,
        compiler_params=_cparams(("parallel", "parallel", "arbitrary", "arbitrary")),
        name=name,
    )(*args)


MOE_TM = 512
MOE_TS = 512
META_I1, META_I2, META_R1, META_R2, META_G1, META_G2 = range(6)


def _router_kernel(x_ref, mod_ref, nw_ref, rw_ref, rb_ref, tri_ref, h_ref, meta_ref, cnt_ref, base_scr):
    d = x_ref.shape[-1]
    tm = x_ref.shape[1]

    @pl.when((pl.program_id(0) == 0) & (pl.program_id(1) == 0))
    def _():
        base_scr[...] = jnp.zeros_like(base_scr)

    x = x_ref[0]
    ms = jnp.mean(x * x, axis=-1, keepdims=True)
    h = x * lax.rsqrt(ms + EPS) * nw_ref[...]
    mod = mod_ref[0]
    h = h * (1.0 + mod[:, 4 * d:5 * d]) + mod[:, 3 * d:4 * d]
    h_ref[0] = h
    hh, hl = _split_hi_lo(h)
    wh, wl = _split_hi_lo(rw_ref[...])
    logits = _dot(hh, wh) + _dot(hl, wh) + _dot(hh, wl) + rb_ref[...]
    lane = lax.broadcasted_iota(jnp.int32, (tm, LANES), 1)
    lg = jnp.where(lane < N_EXPERTS, logits, MASK_NEG)
    m1 = jnp.max(lg, axis=-1, keepdims=True)
    i1 = jnp.min(jnp.where(lg == m1, lane, LANES), axis=-1, keepdims=True)
    lg2 = jnp.where(lane == i1, MASK_NEG, lg)
    m2 = jnp.max(lg2, axis=-1, keepdims=True)
    i2 = jnp.min(jnp.where(lg2 == m2, lane, LANES), axis=-1, keepdims=True)
    e2 = jnp.exp(m2 - m1)
    g1 = 1.0 / (1.0 + e2)
    g2 = e2 / (1.0 + e2)
    oh = jnp.where((lane == i1) | (lane == i2), 1.0, 0.0)
    pos = base_scr[...] + _dot(tri_ref[...], oh.astype(bf16))
    r1 = jnp.sum(jnp.where(lane == i1, pos, 0.0), axis=-1, keepdims=True)
    r2 = jnp.sum(jnp.where(lane == i2, pos, 0.0), axis=-1, keepdims=True)
    meta = jnp.zeros((tm, LANES), f32)
    for k, val in ((META_I1, i1.astype(f32)), (META_I2, i2.astype(f32)), (META_R1, r1), (META_R2, r2),
                   (META_G1, g1), (META_G2, g2)):
        meta = jnp.where(lane == k, val, meta)
    meta_ref[0] = meta
    new_base = pos[tm - 1:tm, :] + oh[tm - 1:tm, :]
    base_scr[...] = new_base
    cnt_ref[...] = jnp.broadcast_to(new_base, cnt_ref.shape)


def _moe_route(x, mod, mod_row0, nw1, rw, rb, tm):
    bsz, l, d = x.shape
    nt = l // tm
    r = jnp.arange(tm)
    tri = (r[:, None] > r[None, :]).astype(bf16)
    return pl.pallas_call(
        _router_kernel,
        grid=(bsz, nt),
        in_specs=[
            pl.BlockSpec((1, tm, d), lambda b, i: (b, i, 0)),
            pl.BlockSpec((1, 1, mod.shape[-1]), lambda b, i: (mod_row0 + b, 0, 0)),
            pl.BlockSpec((1, d), lambda b, i: (0, 0)),
            pl.BlockSpec((d, LANES), lambda b, i: (0, 0)),
            pl.BlockSpec((1, LANES), lambda b, i: (0, 0)),
            pl.BlockSpec((tm, tm), lambda b, i: (0, 0)),
        ],
        out_specs=[
            pl.BlockSpec((1, tm, d), lambda b, i: (b, i, 0)),
            pl.BlockSpec((1, tm, LANES), lambda b, i: (b, i, 0)),
            pl.BlockSpec((8, LANES), lambda b, i: (0, 0)),
        ],
        out_shape=[
            jax.ShapeDtypeStruct((bsz, l, d), f32),
            jax.ShapeDtypeStruct((bsz, l, LANES), f32),
            jax.ShapeDtypeStruct((8, LANES), f32),
        ],
        scratch_shapes=[pltpu.VMEM((1, LANES), f32)],
        compiler_params=_cparams(("arbitrary", "arbitrary")),
        name="moe_route",
    )(x, mod, nw1, rw, rb, tri)


def _row_copy(src_ref, src_row, dst_ref, dst_row, sem):
    return pltpu.make_async_copy(src_ref.at[pl.ds(src_row, 1)], dst_ref.at[pl.ds(dst_row, 1)], sem)


MOE_NZERO = 2 * N_EXPERTS


def _scatter_kernel(dest_ref, h_ref, xs_ref, zero_scr, sem, zsem):
    i = pl.program_id(0)
    ts = h_ref.shape[0]
    tm = zero_scr.shape[0]
    base = i * (2 * ts)

    @pl.when(i == 0)
    def _():
        zero_scr[...] = jnp.zeros(zero_scr.shape, f32)
        zbase = pl.num_programs(0) * (2 * ts)

        def tile_copy(k):
            start = pl.multiple_of(dest_ref[zbase + k], tm)
            return pltpu.make_async_copy(zero_scr, xs_ref.at[pl.ds(start, tm)], zsem)

        def when_named(k, action):
            @pl.when(dest_ref[zbase + MOE_NZERO + k] == 1)
            def _():
                action(tile_copy(k))

        for k in range(MOE_NZERO):
            when_named(k, lambda cp: cp.start())
        for k in range(MOE_NZERO):
            when_named(k, lambda cp: cp.wait())

    def issue(r, carry):
        _row_copy(h_ref, r, xs_ref, dest_ref[base + r], sem).start(priority=0)
        _row_copy(h_ref, r, xs_ref, dest_ref[base + ts + r], sem).start(priority=1)
        return carry

    lax.fori_loop(0, ts, issue, 0, unroll=8)

    def drain(r, carry):
        _row_copy(h_ref, 0, xs_ref, 0, sem).wait()
        _row_copy(h_ref, 0, xs_ref, 0, sem).wait()
        return carry

    lax.fori_loop(0, ts, drain, 0, unroll=8)


def _moe_scatter(h2, dest, n_rows, ts, tm):
    t, d = h2.shape
    return pl.pallas_call(
        _scatter_kernel,
        grid_spec=pltpu.PrefetchScalarGridSpec(
            num_scalar_prefetch=1,
            grid=(t // ts,),
            in_specs=[pl.BlockSpec((ts, d), lambda i, dest: (i, 0))],
            out_specs=pl.BlockSpec(memory_space=pl.ANY),
            scratch_shapes=[pltpu.VMEM((tm, d), f32), pltpu.SemaphoreType.DMA, pltpu.SemaphoreType.DMA],
        ),
        out_shape=jax.ShapeDtypeStruct((n_rows, d), f32),
        compiler_params=_cparams(("arbitrary",)),
        name="moe_scatter",
    )(dest, h2)


def _expert_kernel(te_ref, tv_ref, x_ref, wg_ref, wu_ref, wd_ref, o_ref, acc_scr):
    del te_ref
    j = pl.program_id(0)
    f = pl.program_id(1)
    nf = pl.num_programs(1)
    del nf
    valid = tv_ref[j] == 1

    @pl.when((j == 0) & (f == 0))
    def _():
        acc_scr[...] = jnp.zeros_like(acc_scr)

    @pl.when(valid)
    def _():
        xb = x_ref[...].astype(bf16)
        a = _dot(xb, wg_ref[0])
        u = _dot(xb, wu_ref[0])
        part = _dot(((a * _sigmoid(a)) * u).astype(bf16), wd_ref[0])
        acc = jnp.where(f > 0, acc_scr[...], 0.0) + part
        acc_scr[...] = acc
        o_ref[...] = acc

    @pl.when(jnp.logical_not(valid))
    def _():
        o_ref[...] = jnp.zeros_like(o_ref)


def _moe_experts(xs, te, tv, wg, wu, wd, tm, tf):
    n_rows, d = xs.shape
    ff = wg.shape[-1]
    nf = ff // tf
    n_tiles = te.shape[0]

    def fsel(j, f, tv):
        return jnp.where(tv[j] == 1, f, nf - 1)

    return pl.pallas_call(
        _expert_kernel,
        grid_spec=pltpu.PrefetchScalarGridSpec(
            num_scalar_prefetch=2,
            grid=(n_tiles, nf),
            in_specs=[
                pl.BlockSpec((tm, d), lambda j, f, te, tv: (j, 0)),
                pl.BlockSpec((1, d, tf), lambda j, f, te, tv: (te[j], 0, fsel(j, f, tv))),
                pl.BlockSpec((1, d, tf), lambda j, f, te, tv: (te[j], 0, fsel(j, f, tv))),
                pl.BlockSpec((1, tf, d), lambda j, f, te, tv: (te[j], fsel(j, f, tv), 0)),
            ],
            out_specs=pl.BlockSpec((tm, d), lambda j, f, te, tv: (j, 0)),
            scratch_shapes=[pltpu.VMEM((tm, d), f32)],
        ),
        out_shape=jax.ShapeDtypeStruct((n_rows, d), f32),
        compiler_params=_cparams(("arbitrary", "arbitrary")),
        name="moe_experts",
    )(te, tv, xs, wg, wu, wd)


def _combine_kernel(dest_ref, x_ref, mod_ref, meta_ref, nw_ref, ys_ref, o_ref, y1_scr, y2_scr, sem):
    i = pl.program_id(0)
    ts = x_ref.shape[0]
    d = x_ref.shape[1]
    base = i * (2 * ts)

    def issue(r, carry):
        _row_copy(ys_ref, dest_ref[base + r], y1_scr, r, sem).start(priority=0)
        _row_copy(ys_ref, dest_ref[base + ts + r], y2_scr, r, sem).start(priority=1)
        return carry

    lax.fori_loop(0, ts, issue, 0, unroll=8)

    def drain(r, carry):
        _row_copy(ys_ref, 0, y1_scr, 0, sem).wait()
        _row_copy(ys_ref, 0, y2_scr, 0, sem).wait()
        return carry

    lax.fori_loop(0, ts, drain, 0, unroll=8)

    meta = meta_ref[...]
    y = meta[:, META_G1:META_G1 + 1] * y1_scr[...] + meta[:, META_G2:META_G2 + 1] * y2_scr[...]
    ms = jnp.mean(y * y, axis=-1, keepdims=True)
    y = y * lax.rsqrt(ms + EPS) * nw_ref[...]
    o_ref[...] = x_ref[...] + mod_ref[0][:, 5 * d:6 * d] * y


def _moe_combine(x2, mod, mod_row0, tokens_per_batch, meta2, nw2, ys, dest, ts):
    t, d = x2.shape
    per_b = tokens_per_batch // ts
    return pl.pallas_call(
        _combine_kernel,
        grid_spec=pltpu.PrefetchScalarGridSpec(
            num_scalar_prefetch=1,
            grid=(t // ts,),
            in_specs=[
                pl.BlockSpec((ts, d), lambda i, dest: (i, 0)),
                pl.BlockSpec((1, 1, mod.shape[-1]), lambda i, dest: (mod_row0 + i // per_b, 0, 0)),
                pl.BlockSpec((ts, LANES), lambda i, dest: (i, 0)),
                pl.BlockSpec((1, d), lambda i, dest: (0, 0)),
                pl.BlockSpec(memory_space=pl.ANY),
            ],
            out_specs=pl.BlockSpec((ts, d), lambda i, dest: (i, 0)),
            scratch_shapes=[pltpu.VMEM((ts, d), f32), pltpu.VMEM((ts, d), f32), pltpu.SemaphoreType.DMA],
        ),
        out_shape=jax.ShapeDtypeStruct((t, d), f32),
        compiler_params=_cparams(("arbitrary",)),
        name="moe_combine",
    )(dest, x2, mod, meta2, nw2, ys)


def _moe_sparse(x, mod, mod_row0, nw1, rw, rb, wg, wu, wd, nw2):
    bsz, l, d = x.shape
    t = bsz * l
    ne = wg.shape[0]
    tm_r = _pick_tile(l, (1024, 512, 256, 128))
    tm = min(MOE_TM, t)
    ts = _pick_tile(l, (MOE_TS, 256, 128))
    tf = _pick_tile(wg.shape[-1], (1792, 896, 512, 256, 128))

    h, meta, cnt = _moe_route(x, mod, mod_row0, nw1, rw, rb, tm_r)
    meta2 = meta.reshape(t, LANES)
    counts = cnt[0, :ne].astype(jnp.int32)
    tiles_e = (counts + tm - 1) // tm
    ends = jnp.cumsum(tiles_e)
    offset = (ends - tiles_e) * tm
    n_tiles = (2 * t) // tm + ne
    i1 = meta2[:, META_I1].astype(jnp.int32)
    i2 = meta2[:, META_I2].astype(jnp.int32)
    dest1 = jnp.take(offset, i1) + meta2[:, META_R1].astype(jnp.int32)
    dest2 = jnp.take(offset, i2) + meta2[:, META_R2].astype(jnp.int32)
    dest = jnp.stack([dest1.reshape(t // ts, ts), dest2.reshape(t // ts, ts)], axis=1).reshape(-1)
    jt = jnp.arange(n_tiles, dtype=jnp.int32)
    total = ends[ne - 1]
    tv = (jt < total).astype(jnp.int32)
    te_all = jnp.minimum(jnp.sum((jt[:, None] >= ends[None, :]).astype(jnp.int32), axis=1), ne - 1)
    te_last = jnp.take(te_all, jnp.maximum(total - 1, 0))
    te = jnp.where(tv == 1, te_all, te_last)
    n_rows = n_tiles * tm
    past = total + jnp.arange(ne, dtype=jnp.int32)
    zero_tiles = jnp.concatenate([jnp.maximum(ends - 1, 0), jnp.minimum(past, n_tiles - 1)])
    zero_flags = jnp.concatenate([tiles_e > 0, past < n_tiles]).astype(jnp.int32)
    dest = jnp.concatenate([dest, zero_tiles.astype(jnp.int32) * tm, zero_flags])

    xs = _moe_scatter(h.reshape(t, d), dest, n_rows, ts, tm)
    ys = _moe_experts(xs, te, tv, wg, wu, wd, tm, tf)
    out = _moe_combine(x.reshape(t, d), mod, mod_row0, l, meta2, nw2, ys, dest, ts)
    return out.reshape(bsz, l, d)


def _pick_tile(n, prefs):
    for t in prefs:
        if n % t == 0:
            return t
    return n


def kernel(x, c, ctx, c_ctx, w_mod, b_mod, norm_pre_mix, norm_post_mix, norm_pre_ffn, norm_post_ffn, w_in, w_out, ml_gate_b, ml_norm, hg_lb_logits, hg_norm, da_lambda, da_norm, ffn_w_gate, ffn_w_up, ffn_w_down, router_w, router_b, moe_w_gate, moe_w_up, moe_w_down):
    depth = w_mod.shape[0]
    bsz, l, d = x.shape
    lctx = ctx.shape[1]

    mod = _mod_vectors(c, c_ctx, w_mod, b_mod)
    rows = mod.shape[1]
    mod = mod.reshape(depth * rows, 1, 6 * d)

    lb_soft = jax.nn.softmax(hg_lb_logits.astype(f32), axis=0)
    lower_bounds = jnp.cumsum(lb_soft, axis=0) - lb_soft[0]

    tables = _rope_tables(l)
    ml_lc = min(ML_CHUNK, lctx)
    tri_ml_x = _tri_pair(min(ML_CHUNK, l), min(ML_SUB, l))
    tri_ml_c = _tri_pair(ml_lc, min(ML_SUB, ml_lc))
    tri_hg_x = _tri_pair(min(HG_GROUP, l), HG_CHUNK)
    tri_hg_c = _tri_pair(min(HG_GROUP, lctx), HG_CHUNK)
    hg_bias = _hgrn2_bias()
    ml_sel = _mlstm_sel()

    tm_x = _pick_tile(l, (512, 256, 128))
    tm_c = _pick_tile(lctx, (256, 128))
    tq = _pick_tile(l, (512, 256, 128))

    for layer in range(depth):
        need_ctx = layer < depth - 1
        lam_init = 0.8 - 0.6 * math.exp(-0.3 * layer)
        row0 = layer * rows
        w_in_l = _prep_w_in(w_in[layer])
        w_out_l = w_out[layer].astype(bf16)
        nw_pre = norm_pre_mix[layer].reshape(1, d)

        pc = _inproj(ctx, mod, row0 + bsz, False, nw_pre, w_in_l, None, tm_c)
        px = _inproj(x, mod, row0, True, nw_pre, w_in_l, tables, tm_x)
        ml_c, mlg_c, hg_c, hgf_c, da_c = pc
        ml_x, mlg_x, hg_x, hgf_x, da_x = px

        gb = ml_gate_b[layer].astype(f32)
        z = jnp.zeros((LANES - 8,), f32)
        gate_b2 = jnp.stack([jnp.concatenate([gb[0:8], z]), jnp.concatenate([gb[8:16], z])]).reshape(2, 1, LANES)
        s0 = (jnp.zeros((2, bsz, ML_H, LANES, 2 * LANES), f32), jnp.zeros((2, bsz, ML_H, 8, LANES), f32))
        mlo_c, s_ml = _mlstm(ml_c, mlg_c, gate_b2, tri_ml_c, ml_sel, s0)
        mlo_x, _ = _mlstm(ml_x, mlg_x, gate_b2, tri_ml_x, ml_sel, s_ml)

        lb = lower_bounds[layer]
        lbc = jnp.zeros((8, HG_W), f32)
        lbc = lbc.at[0].set(jnp.log(jnp.maximum(lb, LB_FLOOR))).at[1].set(jnp.log1p(-lb)).at[2].set(1.0 - lb)
        h0 = jnp.zeros((2, bsz, HG_W, HG_W // 2), f32)
        hgo_c, s_hg = _hgrn2(hg_c, hgf_c, lbc, tri_hg_c, hg_bias, h0)
        hgo_x, _ = _hgrn2(hg_x, hgf_x, lbc, tri_hg_x, hg_bias, s_hg)

        lam_p = da_lambda[layer].astype(f32)
        da_nw = da_norm[layer].reshape(1, DA_V)
        dao_x = _attn(da_x, da_c, lam_p, da_nw, lam_init, tq)

        mlw = ml_norm[layer].reshape(1, 256)
        hgw = hg_norm[layer].reshape(1, 256)
        nw_post = norm_post_mix[layer].reshape(1, d)
        x = _outproj(x, mod, row0, True, mlo_x, ml_x, hgo_x, hg_x, dao_x, mlw, hgw, w_out_l, nw_post,
                     _pick_tile(l, (1024, 512, 256, 128)), "outproj_x")
        if need_ctx:
            dao_c = _attn(da_c, da_c, lam_p, da_nw, lam_init, lctx)
            ctx = _outproj(ctx, mod, row0 + bsz, False, mlo_c, ml_c, hgo_c, hg_c, dao_c, mlw, hgw, w_out_l, nw_post, tm_c, "outproj_c")

        i = layer // 2
        nw1 = norm_pre_ffn[layer].reshape(1, d)
        nw2 = norm_post_ffn[layer].reshape(1, d)
        if layer % 2 == 0:
            wg = ffn_w_gate[i].astype(bf16)[None]
            wu = ffn_w_up[i].astype(bf16)[None]
            wd = ffn_w_down[i].astype(bf16)[None]
            router = None
        else:
            wg = moe_w_gate[i].astype(bf16)
            wu = moe_w_up[i].astype(bf16)
            wd = moe_w_down[i].astype(bf16)
            rw = jnp.zeros((d, LANES), f32).at[:, :N_EXPERTS].set(router_w[i])
            rb = jnp.zeros((1, LANES), f32).at[0, :N_EXPERTS].set(router_b[i])
            router = (rw, rb)
        ff = wg.shape[-1]
        tf = _pick_tile(ff, (896, 1408, 512, 256))
        tm_f = _pick_tile(l, (1024, 512, 256, 128) if tf <= 896 else (512, 256, 128))
        if router is None:
            x = _ffn(x, mod, row0, True, nw1, wg, wu, wd, nw2, tm_f, tf, None, name="ffn_x%d" % layer)
        else:
            x = _moe_sparse(x, mod, row0, nw1, rw, rb, wg, wu, wd, nw2)
        if need_ctx:
            ctx = _ffn(ctx, mod, row0 + bsz, False, nw1, wg, wu, wd, nw2, tm_c, tf, router, name="ffn_c%d" % layer)
    return x
```

```python
import functools
import math

import numpy as np
import jax
import jax.numpy as jnp
from jax import lax
from jax.experimental import pallas as pl
from jax.experimental.pallas import tpu as pltpu

f32 = jnp.float32
bf16 = jnp.bfloat16

ML_H = 4
HG_H = 4
DA_H = 4
GRID_W = 64
ROPE_BASE = 10000.0
N_EXPERTS = 8
EPS = 1e-6
MASK_NEG = -1e30
LB_FLOOR = 1e-30

LANES = 128
ML_CHUNK = 256
ML_SUB = 128
HG_CHUNK = 16
HG_GROUP = 1024
VMEM_LIMIT = 56 * 1024 * 1024


def _cparams(sem):
    return pltpu.CompilerParams(dimension_semantics=sem, vmem_limit_bytes=VMEM_LIMIT)


def _sigmoid(x):
    return 1.0 / (1.0 + jnp.exp(-x))


def _log_sigmoid(x):
    return jnp.minimum(x, 0.0) - jnp.log(1.0 + jnp.exp(-jnp.abs(x)))


def _split_hi_lo(a):
    hi = a.astype(bf16)
    lo = (a - hi.astype(f32)).astype(bf16)
    return hi, lo


def _dot(a, b):
    return jnp.dot(a, b, preferred_element_type=f32)


def _dot_nt(a, b):
    return lax.dot_general(a, b, (((1,), (1,)), ((), ())), preferred_element_type=f32)


def _dot_tn(a, b):
    return lax.dot_general(a, b, (((0,), (0,)), ((), ())), preferred_element_type=f32)


def _dot2(a_f32, b_bf16):
    hi, lo = _split_hi_lo(a_f32)
    return _dot(hi, b_bf16) + _dot(lo, b_bf16)


def _mod_kernel(s_ref, w_ref, b_ref, o_ref):
    s = s_ref[...]
    s = s * _sigmoid(s)
    o_ref[0] = _dot(s.astype(bf16), w_ref[0].astype(bf16)) + b_ref[0]


def _mod_vectors(c, c_ctx, w_mod, b_mod):
    depth, d, n = w_mod.shape
    bsz = c.shape[0]
    rows = 8
    s = jnp.zeros((rows, d), f32).at[:bsz].set(c).at[bsz].set(c_ctx)
    tn = n // 4
    out = pl.pallas_call(
        _mod_kernel,
        grid=(depth, n // tn),
        in_specs=[
            pl.BlockSpec((rows, d), lambda l, j: (0, 0)),
            pl.BlockSpec((1, d, tn), lambda l, j: (l, 0, j)),
            pl.BlockSpec((1, 1, tn), lambda l, j: (l, 0, j)),
        ],
        out_specs=pl.BlockSpec((1, rows, tn), lambda l, j: (l, 0, j)),
        out_shape=jax.ShapeDtypeStruct((depth, rows, n), f32),
        compiler_params=_cparams(("parallel", "parallel")),
        name="mod_vectors",
    )(s, w_mod, b_mod.reshape(depth, 1, n))
    return out


P_ML = (0, 1024)
P_MLG = (1024, 1280)
P_HG = (1280, 2048)
P_HGF = (2048, 2560)
P_DA = (2560, 4096)
P_TOTAL = 4096


def _prep_w_in(w):
    d = w.shape[0]
    ml = w[:, 0:1024]
    g = w[:, 1024:1040]
    z = jnp.zeros((d, LANES - 8), w.dtype)
    gates = jnp.concatenate([g[:, 0:8], z, g[:, 8:16], z], axis=1)
    hg_q = w[:, 1040:1296]
    hg_f = w[:, 1296:1808]
    hg_i = w[:, 1808:2064]
    hg_g = w[:, 2064:2320]
    da = w[:, 2320:3856]
    return jnp.concatenate([ml, gates, hg_q, hg_i, hg_g, hg_f, da], axis=1).astype(bf16)


def _inproj_kernel(*refs, rope):
    if rope:
        x_ref, mod_ref, nw_ref, w_ref, cos_ref, sin_ref = refs[:6]
        outs = refs[6:]
    else:
        x_ref, mod_ref, nw_ref, w_ref = refs[:4]
        outs = refs[4:]
    ml_ref, mlg_ref, hg_ref, hgf_ref, da_ref = outs
    d = x_ref.shape[-1]
    x = x_ref[0]
    ms = jnp.mean(x * x, axis=-1, keepdims=True)
    h = x * lax.rsqrt(ms + EPS) * nw_ref[...]
    mod = mod_ref[0]
    h = h * (1.0 + mod[:, d:2 * d]) + mod[:, 0:d]
    hb = h.astype(bf16)

    def proj(lo, hi):
        return _dot(hb, w_ref[:, lo:hi])

    ml_ref[0] = proj(*P_ML).astype(bf16)
    mlg_ref[0] = proj(*P_MLG)
    pq = proj(P_HG[0], P_HG[0] + 256)
    hg_ref[0, :, 0:256] = (pq * _sigmoid(pq)).astype(bf16)
    hg_ref[0, :, 256:768] = proj(P_HG[0] + 256, P_HG[1]).astype(bf16)
    hgf_ref[0] = proj(*P_HGF)
    q = proj(P_DA[0], P_DA[0] + 512)
    k = proj(P_DA[0] + 512, P_DA[0] + 1024)
    if rope:
        cos = jnp.concatenate([cos_ref[...]] * 4, axis=1)
        sin = jnp.concatenate([sin_ref[...]] * 4, axis=1)
        lane = lax.broadcasted_iota(jnp.int32, (1, 512), 1)
        first = (lane % 32) < 16

        def rot(t):
            sw = jnp.where(first, pltpu.roll(t, 512 - 16, 1), pltpu.roll(t, 16, 1))
            return t * cos + sw * sin

        q = rot(q)
        k = rot(k)
    da_ref[0, :, 0:512] = (q * 0.125).astype(bf16)
    da_ref[0, :, 512:1024] = k.astype(bf16)
    da_ref[0, :, 1024:1536] = proj(P_DA[0] + 1024, P_DA[1]).astype(bf16)


def _inproj(x, mod, mod_row0, per_batch, nw, w, tables, tm):
    bsz, l, d = x.shape
    rope = tables is not None
    nt = l // tm
    mod_map = (lambda b, i: (mod_row0 + b, 0, 0)) if per_batch else (lambda b, i: (mod_row0, 0, 0))
    in_specs = [
        pl.BlockSpec((1, tm, d), lambda b, i: (b, i, 0)),
        pl.BlockSpec((1, 1, mod.shape[-1]), mod_map),
        pl.BlockSpec((1, d), lambda b, i: (0, 0)),
        pl.BlockSpec((d, P_TOTAL), lambda b, i: (0, 0)),
    ]
    args = [x, mod, nw, w]
    if rope:
        in_specs += [pl.BlockSpec((tm, LANES), lambda b, i: (i, 0))] * 2
        args += list(tables)
    widths = (1024, 256, 768, 512, 1536)
    dtypes = (bf16, f32, bf16, f32, bf16)
    out_specs = [pl.BlockSpec((1, tm, wd), lambda b, i: (b, i, 0)) for wd in widths]
    out_shape = [jax.ShapeDtypeStruct((bsz, l, wd), dt) for wd, dt in zip(widths, dtypes)]
    return pl.pallas_call(
        functools.partial(_inproj_kernel, rope=rope),
        grid=(bsz, nt),
        in_specs=in_specs,
        out_specs=out_specs,
        out_shape=out_shape,
        compiler_params=_cparams(("parallel", "parallel")),
        name="inproj_x" if rope else "inproj_c",
    )(*args)


def _rope_tables(l):
    rows = l // GRID_W
    row = jnp.repeat(jnp.arange(rows), GRID_W).astype(f32)
    col = jnp.tile(jnp.arange(GRID_W), rows).astype(f32)
    n_freq = 16
    inv = ROPE_BASE ** (-jnp.arange(n_freq, dtype=f32) / n_freq)
    lane = jnp.arange(LANES)
    freq = inv[lane % n_freq]
    pos = jnp.where(((lane % 64) // 32 == 0)[None, :], row[:, None], col[:, None])
    ang = pos * freq[None, :]
    sign = jnp.where((lane % 32) < 16, -1.0, 1.0)[None, :]
    return jnp.cos(ang), jnp.sin(ang) * sign


ML_D = 64


def _mlstm_kernel(q_ref, k_ref, v_ref, g_ref, gb_ref, tri_ref, sel_ref, cn0_ref, m0_ref,
                  o_ref, cnf_ref, mf_ref, cn_scr, m_scr):
    d = pl.program_id(0)
    j = pl.program_id(2)
    nj = pl.num_programs(2)
    lc = q_ref.shape[1]

    @pl.when(j == 0)
    def _():
        cn_scr[...] = cn0_ref[0, 0]
        m_scr[...] = m0_ref[0, 0]

    sub = min(ML_SUB, lc)
    nsub = lc // sub

    def run(order, fwd):
        g = g_ref[0] + gb_ref[0]
        lf = _log_sigmoid(g)
        hi, lo = _split_hi_lo(lf)
        bc = _dot(tri_ref[0], hi) + _dot(tri_ref[0], lo)
        z = pltpu.roll(g, 4, 1) - bc
        zt = z.T
        sel = sel_ref[...]
        yrep = _dot2(bc, sel)
        zrep = _dot2(z, sel)
        r = lax.broadcasted_iota(jnp.int32, (sub, sub), 0)
        c = lax.broadcasted_iota(jnp.int32, (sub, sub), 1)
        lane = lax.broadcasted_iota(jnp.int32, (sub, LANES), 1)
        ones_b = jnp.ones((sub, LANES), bf16)
        mask = (r >= c) if fwd else (r <= c)
        last = sub - 1 if fwd else 0

        def rep(x, n):
            return jnp.concatenate([x] * n, axis=1)

        pre = {}
        for cs in order:
            rows = slice(cs * sub, (cs + 1) * sub)
            for h in range(ML_H):
                p, hh = divmod(h, 2)
                half = (lane < ML_D) if hh == 0 else (lane >= ML_D)
                qp = q_ref[0, rows, p * LANES:(p + 1) * LANES]
                kp = k_ref[0, rows, p * LANES:(p + 1) * LANES] * jnp.asarray(0.125, bf16)
                vp = v_ref[0, rows, p * LANES:(p + 1) * LANES]
                b_rep = yrep[rows, h * LANES:(h + 1) * LANES]
                blast = b_rep[last:last + 1, :]
                qm = jnp.where(half, qp, jnp.zeros_like(qp))
                vo = jnp.concatenate([jnp.where(half, vp, jnp.zeros_like(vp)), ones_b], axis=1)
                s_raw = _dot_nt(qm, kp)
                a_rep = blast + zrep[rows, h * LANES:(h + 1) * LANES]
                m_loc = jnp.max(a_rep, axis=0, keepdims=True)
                w = jnp.exp(a_rep - m_loc)
                kw = jnp.where(half, kp.astype(f32) * w, 0.0).astype(bf16)
                dcn = _dot_tn(kw, vo)
                pre[cs, h] = (b_rep, blast, qm, vo, s_raw, m_loc, dcn)

        for cs in order:
            rows = slice(cs * sub, (cs + 1) * sub)
            outs = []
            for h in range(ML_H):
                b_rep, blast, qm, vo, s_raw, m_loc, dcn = pre.pop((cs, h))
                m0 = m_scr[h, 0:1, :]
                cn = cn_scr[h]
                nd_inter = _dot(qm, cn.astype(bf16))
                z_row = zt[4 + h:5 + h, rows]
                dm = jnp.where(mask, rep(b_rep, sub // LANES) + z_row, MASK_NEG)
                inter = b_rep + m0
                m = jnp.maximum(inter, jnp.max(dm, axis=-1, keepdims=True))
                w_intra = jnp.exp(dm - rep(m, sub // LANES))
                w_inter = jnp.exp(inter - m)
                nd = _dot((s_raw * w_intra).astype(bf16), vo) + rep(w_inter, 2) * nd_inter
                den = nd[:, LANES:2 * LANES]
                outs.append(nd[:, 0:LANES] / jnp.maximum(jnp.abs(den), jnp.exp(-m)))

                m_new = jnp.maximum(blast + m0, m_loc)
                s_old = jnp.exp(blast + m0 - m_new)
                s_new = jnp.exp(m_loc - m_new)
                cn_scr[h] = rep(s_old, 2) * cn + rep(s_new, 2) * dcn
                m_scr[h] = jnp.broadcast_to(m_new, (8, LANES))
            for p in range(ML_H // 2):
                o_ref[0, 0, rows, p * LANES:(p + 1) * LANES] = outs[2 * p] + outs[2 * p + 1]

    @pl.when(d == 0)
    def _():
        run(range(nsub), True)

    @pl.when(d == 1)
    def _():
        run(range(nsub - 1, -1, -1), False)

    @pl.when(j == nj - 1)
    def _():
        cnf_ref[0, 0] = cn_scr[...]
        mf_ref[0, 0] = m_scr[...]


def _mlstm_sel():
    sel = np.zeros((LANES, ML_H * LANES), np.float32)
    for h in range(ML_H):
        sel[4 + h, h * LANES:(h + 1) * LANES] = 1.0
    return jnp.asarray(sel, dtype=bf16)


def _mlstm(ml, mlg, gate_b2, tri, sel, state):
    bsz, l, _ = ml.shape
    lc = min(ML_CHUNK, l)
    nj = l // lc
    cn0, m0 = state

    def tmap(col):
        return lambda d, b, j: (b, jnp.where(d == 0, j, nj - 1 - j), col)

    cn_spec = pl.BlockSpec((1, 1, ML_H, LANES, 2 * LANES), lambda d, b, j: (d, b, 0, 0, 0))
    m_spec = pl.BlockSpec((1, 1, ML_H, 8, LANES), lambda d, b, j: (d, b, 0, 0, 0))
    out, cnf, mf = pl.pallas_call(
        _mlstm_kernel,
        grid=(2, bsz, nj),
        in_specs=[
            pl.BlockSpec((1, lc, 256), tmap(0)),
            pl.BlockSpec((1, lc, 256), tmap(1)),
            pl.BlockSpec((1, lc, 256), tmap(2)),
            pl.BlockSpec((1, lc, LANES), lambda d, b, j: (b, jnp.where(d == 0, j, nj - 1 - j), d)),
            pl.BlockSpec((1, 1, LANES), lambda d, b, j: (d, 0, 0)),
            pl.BlockSpec((1, lc, lc), lambda d, b, j: (d, 0, 0)),
            pl.BlockSpec((LANES, ML_H * LANES), lambda d, b, j: (0, 0)),
            cn_spec,
            m_spec,
        ],
        out_specs=[
            pl.BlockSpec((1, 1, lc, 256), lambda d, b, j: (d, b, jnp.where(d == 0, j, nj - 1 - j), 0)),
            cn_spec,
            m_spec,
        ],
        out_shape=[
            jax.ShapeDtypeStruct((2, bsz, l, 256), f32),
            jax.ShapeDtypeStruct((2, bsz, ML_H, LANES, 2 * LANES), f32),
            jax.ShapeDtypeStruct((2, bsz, ML_H, 8, LANES), f32),
        ],
        scratch_shapes=[pltpu.VMEM((ML_H, LANES, 2 * LANES), f32), pltpu.VMEM((ML_H, 8, LANES), f32)],
        compiler_params=_cparams(("parallel", "parallel", "arbitrary")),
        name="mlstm",
    )(ml, ml, ml, mlg, gate_b2, tri, sel, cn0, m0)
    return out, (cnf, mf)


def _tri_pair(n, block):
    r = np.arange(n)[:, None]
    c = np.arange(n)[None, :]
    same = (r // block) == (c // block)
    lower = (same & (r >= c)).astype(np.float32)
    upper = (same & (r <= c)).astype(np.float32)
    return jnp.asarray(np.stack([lower, upper]), dtype=bf16)


HG_W = 256
HG_FACTOR_MAX = 80.0


def _hgrn2_kernel(q_ref, v_ref, f_ref, lbc_ref, tri_ref, bias_ref, s0_ref, o_ref, sf_ref, st_ref, p_scr):
    d = pl.program_id(0)
    j = pl.program_id(2)
    nj = pl.num_programs(2)
    lg = q_ref.shape[1]
    nck = lg // HG_CHUNK
    hw = HG_W // 2

    @pl.when(j == 0)
    def _():
        st_ref[...] = s0_ref[0, 0]

    fr = f_ref[0]
    log_lb = lbc_ref[0:1, :]
    log_1m_lb = lbc_ref[1:2, :]
    one_m_lb = lbc_ref[2:3, :]
    t = jnp.exp(-jnp.abs(fr))
    b = log_1m_lb + (jnp.minimum(fr, 0.0) - jnp.log(1.0 + t))
    kk = one_m_lb * (jnp.where(fr >= 0.0, t, 1.0) / (1.0 + t))
    log_f = jnp.maximum(log_lb, b) + jnp.log(1.0 + jnp.exp(-jnp.abs(log_lb - b)))
    hi, lo = _split_hi_lo(log_f)
    g = _dot(tri_ref[0], hi) + _dot(tri_ref[0], lo)
    q = q_ref[0].astype(f32)
    qe = (q * jnp.exp(g)).astype(bf16)

    rr = lax.broadcasted_iota(jnp.int32, (hw, hw), 0) // 64
    cc = lax.broadcasted_iota(jnp.int32, (hw, hw), 1) // 64
    same_head = rr == cc
    bd_b = jnp.where(same_head, 1.0, 0.0).astype(bf16)

    sub = 8
    nsub = HG_CHUNK // sub

    lane_head = lax.broadcasted_iota(jnp.int32, (HG_CHUNK, HG_W), 1) // 64
    row_t = lax.broadcasted_iota(jnp.int32, (HG_H * HG_CHUNK, HG_CHUNK), 0) % HG_CHUNK
    col_s = lax.broadcasted_iota(jnp.int32, (HG_H * HG_CHUNK, HG_CHUNK), 1)

    def run(order, last_row, fwd, factored):
        causal = (row_t >= col_s) if fwd else (row_t <= col_s)

        def block_kind(s, tb):
            lo, hi = tb * sub, tb * sub + sub - 1
            if fwd:
                return "all" if lo >= s else ("none" if hi < s else "some")
            return "all" if hi <= s else ("none" if lo > s else "some")

        def build_p(c):
            r0 = c * HG_CHUNK
            gc = g[r0:r0 + HG_CHUNK]
            kc = kk[r0:r0 + HG_CHUNK]
            qc = q[r0:r0 + HG_CHUNK]
            for s in range(HG_CHUNK):
                blocks = []
                for tb in range(nsub):
                    kind = block_kind(s, tb)
                    if kind == "none":
                        blocks.append(jnp.zeros((sub, HG_W), f32))
                        continue
                    diff = gc[tb * sub:(tb + 1) * sub] - gc[s:s + 1, :]
                    if kind == "some":
                        diff = diff + bias_ref[0, s * HG_CHUNK + tb * sub:s * HG_CHUNK + (tb + 1) * sub, :]
                    blocks.append(qc[tb * sub:(tb + 1) * sub] * kc[s:s + 1, :] * jnp.exp(diff))
                p_scr[c, s * HG_CHUNK:(s + 1) * HG_CHUNK, :] = jnp.concatenate(blocks, axis=0).astype(bf16)

        def intra_direct(c):
            r0 = c * HG_CHUNK
            vc = v_ref[0, r0:r0 + HG_CHUNK, :].astype(f32)
            abc = jnp.concatenate([_dot(p_scr[c, :, 0:hw], bd_b), _dot(p_scr[c, :, hw:HG_W], bd_b)], axis=1)
            o_blocks = []
            for tb in range(nsub):
                acc = None
                for s in range(HG_CHUNK):
                    if block_kind(s, tb) == "none":
                        continue
                    term = abc[s * HG_CHUNK + tb * sub:s * HG_CHUNK + (tb + 1) * sub, :] * vc[s:s + 1, :]
                    acc = term if acc is None else acc + term
                o_blocks.append(acc)
            return jnp.concatenate(o_blocks, axis=0)

        def scores_factored(c):
            r0 = c * HG_CHUNK
            qec = qe[r0:r0 + HG_CHUNK]
            kx = (kk[r0:r0 + HG_CHUNK] * jnp.exp(-g[r0:r0 + HG_CHUNK])).astype(bf16)
            qs = jnp.concatenate([jnp.where(lane_head == h, qec, jnp.zeros_like(qec)) for h in range(HG_H)], axis=0)
            return jnp.where(causal, _dot_nt(qs, kx), 0.0).astype(bf16)

        def intra_factored(c, a):
            o_all = _dot(a, v_ref[0, c * HG_CHUNK:(c + 1) * HG_CHUNK, :])
            o = None
            for h in range(HG_H):
                term = jnp.where(lane_head == h, o_all[h * HG_CHUNK:(h + 1) * HG_CHUNK], 0.0)
                o = term if o is None else o + term
            return o

        def state_delta(c):
            r0 = c * HG_CHUNK
            gc = g[r0:r0 + HG_CHUNK]
            vcb = v_ref[0, r0:r0 + HG_CHUNK, :]
            g_last = gc[last_row:last_row + 1, :]
            kt = (kk[r0:r0 + HG_CHUNK] * jnp.exp(g_last - gc)).astype(bf16)
            return (jnp.exp(g_last), _dot_tn(vcb[:, 0:hw], kt[:, 0:hw]), _dot_tn(vcb[:, hw:HG_W], kt[:, hw:HG_W]))

        def finish(c, o, delta):
            r0 = c * HG_CHUNK
            dec, d_lo, d_hi = delta
            st = st_ref[...]
            stb = st.astype(bf16)
            qec = qe[r0:r0 + HG_CHUNK]
            o = o + jnp.concatenate([_dot_nt(qec[:, 0:hw], stb[0:hw]), _dot_nt(qec[:, hw:HG_W], stb[hw:HG_W])], axis=1)
            o_ref[0, 0, r0:r0 + HG_CHUNK, :] = o
            st_ref[0:hw, :] = st[0:hw] * dec[:, 0:hw] + jnp.where(same_head, d_lo, 0.0)
            st_ref[hw:HG_W, :] = st[hw:HG_W] * dec[:, hw:HG_W] + jnp.where(same_head, d_hi, 0.0)

        order = list(order)
        if factored:
            n = len(order)
            stage_a, stage_b = {}, {}
            for i in range(n + 2):
                if i < n:
                    stage_a[order[i]] = (scores_factored(order[i]), state_delta(order[i]))
                if 0 <= i - 1 < n:
                    c = order[i - 1]
                    stage_b[c] = intra_factored(c, stage_a[c][0])
                if 0 <= i - 2 < n:
                    c = order[i - 2]
                    finish(c, stage_b.pop(c), stage_a.pop(c)[1])
        else:
            for c in order:
                build_p(c)
                finish(c, intra_direct(c), state_delta(c))

    can_factor = jnp.max(-g) < HG_FACTOR_MAX
    for direction, (order, last_row) in enumerate(((range(nck), HG_CHUNK - 1), (range(nck - 1, -1, -1), 0))):
        for factored in (True, False):
            @pl.when((d == direction) & (can_factor if factored else jnp.logical_not(can_factor)))
            def _(order=order, last_row=last_row, direction=direction, factored=factored):
                run(order, last_row, direction == 0, factored)

    @pl.when(j == nj - 1)
    def _():
        sf_ref[0, 0] = st_ref[...]


def _hgrn2(hg, hgf, lbc, tri, bias, s0):
    bsz, l, _ = hg.shape
    lg = min(HG_GROUP, l)
    nj = l // lg
    hw = HG_W // 2

    def tmap(col):
        return lambda d, b, j: (b, jnp.where(d == 0, j, nj - 1 - j), col)

    out, sf = pl.pallas_call(
        _hgrn2_kernel,
        grid=(2, bsz, nj),
        in_specs=[
            pl.BlockSpec((1, lg, HG_W), tmap(0)),
            pl.BlockSpec((1, lg, HG_W), tmap(1)),
            pl.BlockSpec((1, lg, HG_W), lambda d, b, j: (b, jnp.where(d == 0, j, nj - 1 - j), d)),
            pl.BlockSpec((8, HG_W), lambda d, b, j: (0, 0)),
            pl.BlockSpec((1, lg, lg), lambda d, b, j: (d, 0, 0)),
            pl.BlockSpec((1, HG_CHUNK * HG_CHUNK, HG_W), lambda d, b, j: (d, 0, 0)),
            pl.BlockSpec((1, 1, HG_W, hw), lambda d, b, j: (d, b, 0, 0)),
        ],
        out_specs=[
            pl.BlockSpec((1, 1, lg, HG_W), lambda d, b, j: (d, b, jnp.where(d == 0, j, nj - 1 - j), 0)),
            pl.BlockSpec((1, 1, HG_W, hw), lambda d, b, j: (d, b, 0, 0)),
        ],
        out_shape=[
            jax.ShapeDtypeStruct((2, bsz, l, HG_W), f32),
            jax.ShapeDtypeStruct((2, bsz, HG_W, hw), f32),
        ],
        scratch_shapes=[
            pltpu.VMEM((HG_W, hw), f32),
            pltpu.VMEM((lg // HG_CHUNK, HG_CHUNK * HG_CHUNK, HG_W), bf16),
        ],
        compiler_params=_cparams(("parallel", "parallel", "arbitrary")),
        name="hgrn2",
    )(hg, hg, hgf, lbc, tri, bias, s0)
    return out, sf


def _hgrn2_bias():
    idx = np.arange(HG_CHUNK * HG_CHUNK)
    s, t = idx // HG_CHUNK, idx % HG_CHUNK
    fwd = np.where(t >= s, 0.0, MASK_NEG).astype(np.float32)
    bwd = np.where(t <= s, 0.0, MASK_NEG).astype(np.float32)
    return jnp.asarray(np.broadcast_to(np.stack([fwd, bwd])[:, :, None], (2, HG_CHUNK * HG_CHUNK, HG_W)))


DA_V = 128


def _attn_kernel(*refs, has_x, lam_init, kg, nq):
    if has_x:
        (q_ref, kx_ref, kc_ref, vx_ref, vc_ref, lp_ref, nw_ref, o_ref,
         s0_scr, s1_scr, m0_scr, m1_scr, a0_scr, a1_scr, va_scr) = refs
        lx = kx_ref.shape[1]
    else:
        q_ref, kc_ref, vc_ref, lp_ref, nw_ref, o_ref, s0_scr, s1_scr, m0_scr, m1_scr, a0_scr, a1_scr, va_scr = refs
        lx = 0
    tq = q_ref.shape[1]
    lctx = kc_ref.shape[1]
    t = pl.program_id(0)

    @pl.when(t == 0)
    def _():
        s1_scr[...] = jnp.zeros(s1_scr.shape, f32)
        m1_scr[...] = jnp.zeros(m1_scr.shape, f32)
        a1_scr[...] = jnp.ones(a1_scr.shape, f32)

    @pl.when((t == 0) | ((t - 1) % nq == 0))
    def _():
        def ones_col(n):
            lane = lax.broadcasted_iota(jnp.int32, (n, DA_V), 1)
            return jnp.where(lane == 0, 1.0, 0.0).astype(bf16)

        if has_x:
            va_scr[0:lx, 0:DA_V] = vx_ref[0]
            va_scr[0:lx, DA_V:2 * DA_V] = ones_col(lx)
        va_scr[lx:lx + lctx, 0:DA_V] = vc_ref[0]
        va_scr[lx:lx + lctx, DA_V:2 * DA_V] = ones_col(lctx)

    groups = ([(kx_ref, g * kg, kg, g * kg) for g in range(lx // kg)] if has_x else []) + [(kc_ref, 0, lctx, lx)]

    def step(sa_scr, ma_scr, aa_scr, sb_scr, mb_scr, ab_scr):
        q = q_ref[0]
        lane = lax.broadcasted_iota(jnp.int32, (tq, DA_V), 1)
        zero = jnp.zeros_like(q)
        qq = jnp.concatenate([jnp.where(lane < 64, q, zero), jnp.where(lane >= 64, q, zero)], axis=0)
        m128 = None
        for ref, r0, n, c0 in groups:
            s = _dot_nt(qq, ref[0, r0:r0 + n, :])
            sa_scr[:, c0:c0 + n] = s
            for c in range(n // LANES):
                blk = s[:, c * LANES:(c + 1) * LANES]
                m128 = blk if m128 is None else jnp.maximum(m128, blk)
        ma_scr[...] = jnp.broadcast_to(jnp.max(m128, axis=-1, keepdims=True), m128.shape)
        m_prev = mb_scr[...]
        acc = None
        for ref, r0, n, c0 in groups:
            m_rep = jnp.concatenate([m_prev] * (n // LANES), axis=1)
            p = jnp.exp(sb_scr[:, c0:c0 + n] - m_rep).astype(bf16)
            part = _dot(p, va_scr[c0:c0 + n, :])
            acc = part if acc is None else acc + part
        aa_scr[...] = acc
        acc = ab_scr[:, 0:DA_V]
        l = ab_scr[:, DA_V:DA_V + 1]
        lp = lp_ref[...]
        lam = (jnp.exp(jnp.sum(lp[0:1] * lp[1:2], axis=-1, keepdims=True))
               - jnp.exp(jnp.sum(lp[2:3] * lp[3:4], axis=-1, keepdims=True)) + lam_init)
        o = acc[0:tq] / l[0:tq] - lam * (acc[tq:2 * tq] / l[tq:2 * tq])
        ms = jnp.mean(o * o, axis=-1, keepdims=True)
        o_ref[0] = (o * lax.rsqrt(ms + EPS) * nw_ref[...] * (1.0 - lam_init)).astype(bf16)

    @pl.when(t % 2 == 0)
    def _():
        step(s0_scr, m0_scr, a0_scr, s1_scr, m1_scr, a1_scr)

    @pl.when(t % 2 == 1)
    def _():
        step(s1_scr, m1_scr, a1_scr, s0_scr, m0_scr, a0_scr)


def _attn(da_q, da_c, lam_p, nw, lam_init, tq):
    has_x = da_q is not da_c
    bsz, l, _ = da_q.shape
    lctx = da_c.shape[1]
    nq = l // tq
    nt = bsz * DA_H * nq
    lx = l if has_x else 0
    kg = _pick_tile(l, (1024, 512, 256, 128))

    def split(t):
        return t // (DA_H * nq), (t // nq) % DA_H, t % nq

    def cur(col0, whole):
        def index(t):
            b, h, i = split(jnp.minimum(t, nt - 1))
            return (b, 0 if whole else i, col0 + h)
        return index

    def prev(col0, whole, lag=1):
        def index(t):
            b, h, i = split(jnp.clip(t - lag, 0, nt - 1))
            return (b, 0 if whole else i, col0 + h)
        return index

    in_specs = [pl.BlockSpec((1, tq, DA_V), cur(0, False))]
    args = [da_q]
    if has_x:
        in_specs += [pl.BlockSpec((1, l, DA_V), cur(4, True))]
        args += [da_q]
    in_specs += [pl.BlockSpec((1, lctx, DA_V), cur(4, True))]
    args += [da_c]
    if has_x:
        in_specs += [pl.BlockSpec((1, l, DA_V), prev(8, True))]
        args += [da_q]
    in_specs += [
        pl.BlockSpec((1, lctx, DA_V), prev(8, True)),
        pl.BlockSpec((4, 64), lambda t: (0, 0)),
        pl.BlockSpec((1, DA_V), lambda t: (0, 0)),
    ]
    args += [da_c, lam_p, nw]
    return pl.pallas_call(
        functools.partial(_attn_kernel, has_x=has_x, lam_init=lam_init, kg=kg, nq=nq),
        grid=(nt + 2,),
        in_specs=in_specs,
        out_specs=pl.BlockSpec((1, tq, DA_V), prev(0, False, lag=2)),
        out_shape=jax.ShapeDtypeStruct((bsz, l, DA_H * DA_V), bf16),
        scratch_shapes=[
            pltpu.VMEM((2 * tq, lx + lctx), f32),
            pltpu.VMEM((2 * tq, lx + lctx), f32),
            pltpu.VMEM((2 * tq, LANES), f32),
            pltpu.VMEM((2 * tq, LANES), f32),
            pltpu.VMEM((2 * tq, 2 * DA_V), f32),
            pltpu.VMEM((2 * tq, 2 * DA_V), f32),
            pltpu.VMEM((lx + lctx, 2 * DA_V), bf16),
        ],
        compiler_params=_cparams(("arbitrary",)),
        name="attn_x" if has_x else "attn_c",
    )(*args)


def _head_ms(y, bd_b):
    return _dot2(y * y, bd_b) * (1.0 / 64.0)


def _outproj_kernel(x_ref, mod_ref, mlf_ref, mlb_ref, mlo_ref, hgf_ref, hgb_ref, hgg_ref, da_ref,
                    mlw_ref, hgw_ref, w_ref, nw_ref, o_ref):
    d = x_ref.shape[-1]
    rr = lax.broadcasted_iota(jnp.int32, (256, 256), 0) // 64
    cc = lax.broadcasted_iota(jnp.int32, (256, 256), 1) // 64
    bd_b = jnp.where(rr == cc, 1.0, 0.0).astype(bf16)
    ml = mlf_ref[0, 0] + mlb_ref[0, 0]
    ml = ml * lax.rsqrt(_head_ms(ml, bd_b) + EPS) * mlw_ref[...]
    ml = _sigmoid(mlo_ref[0].astype(f32)) * ml
    hg = hgf_ref[0, 0] + hgb_ref[0, 0]
    hg = hg * lax.rsqrt(_head_ms(hg, bd_b) + EPS) * hgw_ref[...]
    gg = hgg_ref[0].astype(f32)
    hg = hg * (gg * _sigmoid(gg))
    mix = (_dot(ml.astype(bf16), w_ref[0:256, :]) + _dot(hg.astype(bf16), w_ref[256:512, :])
           + _dot(da_ref[0], w_ref[512:1024, :]))
    ms = jnp.mean(mix * mix, axis=-1, keepdims=True)
    y = mix * lax.rsqrt(ms + EPS) * nw_ref[...]
    o_ref[0] = x_ref[0] + mod_ref[0][:, 2 * d:3 * d] * y


def _outproj(x, mod, mod_row0, per_batch, ml_out, ml, hg_out, hg, da_out, mlw, hgw, w, nw, tm, name):
    bsz, l, d = x.shape
    nt = l // tm
    mod_map = (lambda b, i: (mod_row0 + b, 0, 0)) if per_batch else (lambda b, i: (mod_row0, 0, 0))
    return pl.pallas_call(
        _outproj_kernel,
        grid=(bsz, nt),
        in_specs=[
            pl.BlockSpec((1, tm, d), lambda b, i: (b, i, 0)),
            pl.BlockSpec((1, 1, mod.shape[-1]), mod_map),
            pl.BlockSpec((1, 1, tm, 256), lambda b, i: (0, b, i, 0)),
            pl.BlockSpec((1, 1, tm, 256), lambda b, i: (1, b, i, 0)),
            pl.BlockSpec((1, tm, 256), lambda b, i: (b, i, 3)),
            pl.BlockSpec((1, 1, tm, 256), lambda b, i: (0, b, i, 0)),
            pl.BlockSpec((1, 1, tm, 256), lambda b, i: (1, b, i, 0)),
            pl.BlockSpec((1, tm, 256), lambda b, i: (b, i, 2)),
            pl.BlockSpec((1, tm, 512), lambda b, i: (b, i, 0)),
            pl.BlockSpec((1, 256), lambda b, i: (0, 0)),
            pl.BlockSpec((1, 256), lambda b, i: (0, 0)),
            pl.BlockSpec((d, d), lambda b, i: (0, 0)),
            pl.BlockSpec((1, d), lambda b, i: (0, 0)),
        ],
        out_specs=pl.BlockSpec((1, tm, d), lambda b, i: (b, i, 0)),
        out_shape=jax.ShapeDtypeStruct((bsz, l, d), f32),
        compiler_params=_cparams(("parallel", "parallel")),
        name=name,
    )(x, mod, ml_out, ml_out, ml, hg_out, hg_out, hg, da_out, mlw, hgw, w, nw)


def _ffn_kernel(*refs, moe):
    if moe:
        x_ref, mod_ref, nw1_ref, rw_ref, rb_ref, wg_ref, wu_ref, wd_ref, nw2_ref, o_ref, h_scr, acc_scr, gate_scr = refs
    else:
        x_ref, mod_ref, nw1_ref, wg_ref, wu_ref, wd_ref, nw2_ref, o_ref, h_scr, acc_scr = refs
    e = pl.program_id(2)
    j = pl.program_id(3)
    ne = pl.num_programs(2)
    nj = pl.num_programs(3)
    d = x_ref.shape[-1]
    tm = x_ref.shape[1]

    @pl.when((e == 0) & (j == 0))
    def _():
        x = x_ref[0]
        ms = jnp.mean(x * x, axis=-1, keepdims=True)
        h = x * lax.rsqrt(ms + EPS) * nw1_ref[...]
        mod = mod_ref[0]
        h = h * (1.0 + mod[:, 4 * d:5 * d]) + mod[:, 3 * d:4 * d]
        h_scr[...] = h.astype(bf16)
        acc_scr[...] = jnp.zeros_like(acc_scr)
        if moe:
            hh, hl = _split_hi_lo(h)
            rw = rw_ref[...]
            wh, wl = _split_hi_lo(rw)
            logits = _dot(hh, wh) + _dot(hl, wh) + _dot(hh, wl) + rb_ref[...]
            lane = lax.broadcasted_iota(jnp.int32, (tm, LANES), 1)
            real = lane < N_EXPERTS
            lg = jnp.where(real, logits, MASK_NEG)
            m1 = jnp.max(lg, axis=-1, keepdims=True)
            i1 = jnp.min(jnp.where(lg == m1, lane, LANES), axis=-1, keepdims=True)
            lg2 = jnp.where(lane == i1, MASK_NEG, lg)
            m2 = jnp.max(lg2, axis=-1, keepdims=True)
            i2 = jnp.min(jnp.where(lg2 == m2, lane, LANES), axis=-1, keepdims=True)
            e2 = jnp.exp(m2 - m1)
            g1 = 1.0 / (1.0 + e2)
            g2 = e2 / (1.0 + e2)
            gate_scr[...] = jnp.where(lane == i1, g1, 0.0) + jnp.where(lane == i2, g2, 0.0)

    hb = h_scr[...]
    a = _dot(hb, wg_ref[0])
    u = _dot(hb, wu_ref[0])
    act = (a * _sigmoid(a)) * u
    if moe:
        lane = lax.broadcasted_iota(jnp.int32, (tm, LANES), 1)
        ge = jnp.sum(jnp.where(lane == e, gate_scr[...], 0.0), axis=-1, keepdims=True)
        act = act * ge
    acc_scr[...] += _dot(act.astype(bf16), wd_ref[0])

    @pl.when((e == ne - 1) & (j == nj - 1))
    def _():
        y = acc_scr[...]
        ms = jnp.mean(y * y, axis=-1, keepdims=True)
        y = y * lax.rsqrt(ms + EPS) * nw2_ref[...]
        o_ref[0] = x_ref[0] + mod_ref[0][:, 5 * d:6 * d] * y


def _ffn(x, mod, mod_row0, per_batch, nw1, wg, wu, wd, nw2, tm, tf, router=None, name="ffn"):
    bsz, l, d = x.shape
    ne, _, ff = wg.shape
    nt = l // tm
    nf = ff // tf
    moe = router is not None
    mod_map = (lambda b, i, e, j: (mod_row0 + b, 0, 0)) if per_batch else (lambda b, i, e, j: (mod_row0, 0, 0))
    in_specs = [
        pl.BlockSpec((1, tm, d), lambda b, i, e, j: (b, i, 0)),
        pl.BlockSpec((1, 1, mod.shape[-1]), mod_map),
        pl.BlockSpec((1, d), lambda b, i, e, j: (0, 0)),
    ]
    args = [x, mod, nw1]
    scratch = [pltpu.VMEM((tm, d), bf16), pltpu.VMEM((tm, d), f32)]
    if moe:
        in_specs += [
            pl.BlockSpec((d, LANES), lambda b, i, e, j: (0, 0)),
            pl.BlockSpec((1, LANES), lambda b, i, e, j: (0, 0)),
        ]
        args += list(router)
        scratch += [pltpu.VMEM((tm, LANES), f32)]
    in_specs += [
        pl.BlockSpec((1, d, tf), lambda b, i, e, j: (e, 0, j)),
        pl.BlockSpec((1, d, tf), lambda b, i, e, j: (e, 0, j)),
        pl.BlockSpec((1, tf, d), lambda b, i, e, j: (e, j, 0)),
        pl.BlockSpec((1, d), lambda b, i, e, j: (0, 0)),
    ]
    args += [wg, wu, wd, nw2]
    return pl.pallas_call(
        functools.partial(_ffn_kernel, moe=moe),
        grid=(bsz, nt, ne, nf),
        in_specs=in_specs,
        out_specs=pl.BlockSpec((1, tm, d), lambda b, i, e, j: (b, i, 0)),
        out_shape=jax.ShapeDtypeStruct((bsz, l, d), f32),
        scratch_shapes=scratch,
        compiler_params=_cparams(("parallel", "parallel", "arbitrary", "arbitrary")),
        name=name,
    )(*args)


MOE_TM = 512
MOE_TS = 512
META_I1, META_I2, META_R1, META_R2, META_G1, META_G2 = range(6)


def _router_kernel(x_ref, mod_ref, nw_ref, rw_ref, rb_ref, tri_ref, h_ref, meta_ref, cnt_ref, base_scr):
    d = x_ref.shape[-1]
    tm = x_ref.shape[1]

    @pl.when((pl.program_id(0) == 0) & (pl.program_id(1) == 0))
    def _():
        base_scr[...] = jnp.zeros_like(base_scr)

    x = x_ref[0]
    ms = jnp.mean(x * x, axis=-1, keepdims=True)
    h = x * lax.rsqrt(ms + EPS) * nw_ref[...]
    mod = mod_ref[0]
    h = h * (1.0 + mod[:, 4 * d:5 * d]) + mod[:, 3 * d:4 * d]
    h_ref[0] = h
    hh, hl = _split_hi_lo(h)
    wh, wl = _split_hi_lo(rw_ref[...])
    logits = _dot(hh, wh) + _dot(hl, wh) + _dot(hh, wl) + rb_ref[...]
    lane = lax.broadcasted_iota(jnp.int32, (tm, LANES), 1)
    lg = jnp.where(lane < N_EXPERTS, logits, MASK_NEG)
    m1 = jnp.max(lg, axis=-1, keepdims=True)
    i1 = jnp.min(jnp.where(lg == m1, lane, LANES), axis=-1, keepdims=True)
    lg2 = jnp.where(lane == i1, MASK_NEG, lg)
    m2 = jnp.max(lg2, axis=-1, keepdims=True)
    i2 = jnp.min(jnp.where(lg2 == m2, lane, LANES), axis=-1, keepdims=True)
    e2 = jnp.exp(m2 - m1)
    g1 = 1.0 / (1.0 + e2)
    g2 = e2 / (1.0 + e2)
    oh = jnp.where((lane == i1) | (lane == i2), 1.0, 0.0)
    pos = base_scr[...] + _dot(tri_ref[...], oh.astype(bf16))
    r1 = jnp.sum(jnp.where(lane == i1, pos, 0.0), axis=-1, keepdims=True)
    r2 = jnp.sum(jnp.where(lane == i2, pos, 0.0), axis=-1, keepdims=True)
    meta = jnp.zeros((tm, LANES), f32)
    for k, val in ((META_I1, i1.astype(f32)), (META_I2, i2.astype(f32)), (META_R1, r1), (META_R2, r2),
                   (META_G1, g1), (META_G2, g2)):
        meta = jnp.where(lane == k, val, meta)
    meta_ref[0] = meta
    new_base = pos[tm - 1:tm, :] + oh[tm - 1:tm, :]
    base_scr[...] = new_base
    cnt_ref[...] = jnp.broadcast_to(new_base, cnt_ref.shape)


def _moe_route(x, mod, mod_row0, nw1, rw, rb, tm):
    bsz, l, d = x.shape
    nt = l // tm
    r = jnp.arange(tm)
    tri = (r[:, None] > r[None, :]).astype(bf16)
    return pl.pallas_call(
        _router_kernel,
        grid=(bsz, nt),
        in_specs=[
            pl.BlockSpec((1, tm, d), lambda b, i: (b, i, 0)),
            pl.BlockSpec((1, 1, mod.shape[-1]), lambda b, i: (mod_row0 + b, 0, 0)),
            pl.BlockSpec((1, d), lambda b, i: (0, 0)),
            pl.BlockSpec((d, LANES), lambda b, i: (0, 0)),
            pl.BlockSpec((1, LANES), lambda b, i: (0, 0)),
            pl.BlockSpec((tm, tm), lambda b, i: (0, 0)),
        ],
        out_specs=[
            pl.BlockSpec((1, tm, d), lambda b, i: (b, i, 0)),
            pl.BlockSpec((1, tm, LANES), lambda b, i: (b, i, 0)),
            pl.BlockSpec((8, LANES), lambda b, i: (0, 0)),
        ],
        out_shape=[
            jax.ShapeDtypeStruct((bsz, l, d), f32),
            jax.ShapeDtypeStruct((bsz, l, LANES), f32),
            jax.ShapeDtypeStruct((8, LANES), f32),
        ],
        scratch_shapes=[pltpu.VMEM((1, LANES), f32)],
        compiler_params=_cparams(("arbitrary", "arbitrary")),
        name="moe_route",
    )(x, mod, nw1, rw, rb, tri)


def _row_copy(src_ref, src_row, dst_ref, dst_row, sem):
    return pltpu.make_async_copy(src_ref.at[pl.ds(src_row, 1)], dst_ref.at[pl.ds(dst_row, 1)], sem)


MOE_NZERO = 2 * N_EXPERTS


def _scatter_kernel(dest_ref, h_ref, xs_ref, zero_scr, sem, zsem):
    i = pl.program_id(0)
    ts = h_ref.shape[0]
    tm = zero_scr.shape[0]
    base = i * (2 * ts)

    @pl.when(i == 0)
    def _():
        zero_scr[...] = jnp.zeros(zero_scr.shape, f32)
        zbase = pl.num_programs(0) * (2 * ts)

        def tile_copy(k):
            start = pl.multiple_of(dest_ref[zbase + k], tm)
            return pltpu.make_async_copy(zero_scr, xs_ref.at[pl.ds(start, tm)], zsem)

        def when_named(k, action):
            @pl.when(dest_ref[zbase + MOE_NZERO + k] == 1)
            def _():
                action(tile_copy(k))

        for k in range(MOE_NZERO):
            when_named(k, lambda cp: cp.start())
        for k in range(MOE_NZERO):
            when_named(k, lambda cp: cp.wait())

    def issue(r, carry):
        _row_copy(h_ref, r, xs_ref, dest_ref[base + r], sem).start(priority=0)
        _row_copy(h_ref, r, xs_ref, dest_ref[base + ts + r], sem).start(priority=1)
        return carry

    lax.fori_loop(0, ts, issue, 0, unroll=8)

    def drain(r, carry):
        _row_copy(h_ref, 0, xs_ref, 0, sem).wait()
        _row_copy(h_ref, 0, xs_ref, 0, sem).wait()
        return carry

    lax.fori_loop(0, ts, drain, 0, unroll=8)


def _moe_scatter(h2, dest, n_rows, ts, tm):
    t, d = h2.shape
    return pl.pallas_call(
        _scatter_kernel,
        grid_spec=pltpu.PrefetchScalarGridSpec(
            num_scalar_prefetch=1,
            grid=(t // ts,),
            in_specs=[pl.BlockSpec((ts, d), lambda i, dest: (i, 0))],
            out_specs=pl.BlockSpec(memory_space=pl.ANY),
            scratch_shapes=[pltpu.VMEM((tm, d), f32), pltpu.SemaphoreType.DMA, pltpu.SemaphoreType.DMA],
        ),
        out_shape=jax.ShapeDtypeStruct((n_rows, d), f32),
        compiler_params=_cparams(("arbitrary",)),
        name="moe_scatter",
    )(dest, h2)


def _expert_kernel(te_ref, tv_ref, x_ref, wg_ref, wu_ref, wd_ref, o_ref, acc_scr):
    del te_ref
    j = pl.program_id(0)
    f = pl.program_id(1)
    nf = pl.num_programs(1)
    del nf
    valid = tv_ref[j] == 1

    @pl.when((j == 0) & (f == 0))
    def _():
        acc_scr[...] = jnp.zeros_like(acc_scr)

    @pl.when(valid)
    def _():
        xb = x_ref[...].astype(bf16)
        a = _dot(xb, wg_ref[0])
        u = _dot(xb, wu_ref[0])
        part = _dot(((a * _sigmoid(a)) * u).astype(bf16), wd_ref[0])
        acc = jnp.where(f > 0, acc_scr[...], 0.0) + part
        acc_scr[...] = acc
        o_ref[...] = acc

    @pl.when(jnp.logical_not(valid))
    def _():
        o_ref[...] = jnp.zeros_like(o_ref)


def _moe_experts(xs, te, tv, wg, wu, wd, tm, tf):
    n_rows, d = xs.shape
    ff = wg.shape[-1]
    nf = ff // tf
    n_tiles = te.shape[0]

    def fsel(j, f, tv):
        return jnp.where(tv[j] == 1, f, nf - 1)

    return pl.pallas_call(
        _expert_kernel,
        grid_spec=pltpu.PrefetchScalarGridSpec(
            num_scalar_prefetch=2,
            grid=(n_tiles, nf),
            in_specs=[
                pl.BlockSpec((tm, d), lambda j, f, te, tv: (j, 0)),
                pl.BlockSpec((1, d, tf), lambda j, f, te, tv: (te[j], 0, fsel(j, f, tv))),
                pl.BlockSpec((1, d, tf), lambda j, f, te, tv: (te[j], 0, fsel(j, f, tv))),
                pl.BlockSpec((1, tf, d), lambda j, f, te, tv: (te[j], fsel(j, f, tv), 0)),
            ],
            out_specs=pl.BlockSpec((tm, d), lambda j, f, te, tv: (j, 0)),
            scratch_shapes=[pltpu.VMEM((tm, d), f32)],
        ),
        out_shape=jax.ShapeDtypeStruct((n_rows, d), f32),
        compiler_params=_cparams(("arbitrary", "arbitrary")),
        name="moe_experts",
    )(te, tv, xs, wg, wu, wd)


def _combine_kernel(dest_ref, x_ref, mod_ref, meta_ref, nw_ref, ys_ref, o_ref, y1_scr, y2_scr, sem):
    i = pl.program_id(0)
    ts = x_ref.shape[0]
    d = x_ref.shape[1]
    base = i * (2 * ts)

    def issue(r, carry):
        _row_copy(ys_ref, dest_ref[base + r], y1_scr, r, sem).start(priority=0)
        _row_copy(ys_ref, dest_ref[base + ts + r], y2_scr, r, sem).start(priority=1)
        return carry

    lax.fori_loop(0, ts, issue, 0, unroll=8)

    def drain(r, carry):
        _row_copy(ys_ref, 0, y1_scr, 0, sem).wait()
        _row_copy(ys_ref, 0, y2_scr, 0, sem).wait()
        return carry

    lax.fori_loop(0, ts, drain, 0, unroll=8)

    meta = meta_ref[...]
    y = meta[:, META_G1:META_G1 + 1] * y1_scr[...] + meta[:, META_G2:META_G2 + 1] * y2_scr[...]
    ms = jnp.mean(y * y, axis=-1, keepdims=True)
    y = y * lax.rsqrt(ms + EPS) * nw_ref[...]
    o_ref[...] = x_ref[...] + mod_ref[0][:, 5 * d:6 * d] * y


def _moe_combine(x2, mod, mod_row0, tokens_per_batch, meta2, nw2, ys, dest, ts):
    t, d = x2.shape
    per_b = tokens_per_batch // ts
    return pl.pallas_call(
        _combine_kernel,
        grid_spec=pltpu.PrefetchScalarGridSpec(
            num_scalar_prefetch=1,
            grid=(t // ts,),
            in_specs=[
                pl.BlockSpec((ts, d), lambda i, dest: (i, 0)),
                pl.BlockSpec((1, 1, mod.shape[-1]), lambda i, dest: (mod_row0 + i // per_b, 0, 0)),
                pl.BlockSpec((ts, LANES), lambda i, dest: (i, 0)),
                pl.BlockSpec((1, d), lambda i, dest: (0, 0)),
                pl.BlockSpec(memory_space=pl.ANY),
            ],
            out_specs=pl.BlockSpec((ts, d), lambda i, dest: (i, 0)),
            scratch_shapes=[pltpu.VMEM((ts, d), f32), pltpu.VMEM((ts, d), f32), pltpu.SemaphoreType.DMA],
        ),
        out_shape=jax.ShapeDtypeStruct((t, d), f32),
        compiler_params=_cparams(("arbitrary",)),
        name="moe_combine",
    )(dest, x2, mod, meta2, nw2, ys)


def _moe_sparse(x, mod, mod_row0, nw1, rw, rb, wg, wu, wd, nw2):
    bsz, l, d = x.shape
    t = bsz * l
    ne = wg.shape[0]
    tm_r = _pick_tile(l, (1024, 512, 256, 128))
    tm = min(MOE_TM, t)
    ts = _pick_tile(l, (MOE_TS, 256, 128))
    tf = _pick_tile(wg.shape[-1], (1792, 896, 512, 256, 128))

    h, meta, cnt = _moe_route(x, mod, mod_row0, nw1, rw, rb, tm_r)
    meta2 = meta.reshape(t, LANES)
    counts = cnt[0, :ne].astype(jnp.int32)
    tiles_e = (counts + tm - 1) // tm
    ends = jnp.cumsum(tiles_e)
    offset = (ends - tiles_e) * tm
    n_tiles = (2 * t) // tm + ne
    i1 = meta2[:, META_I1].astype(jnp.int32)
    i2 = meta2[:, META_I2].astype(jnp.int32)
    dest1 = jnp.take(offset, i1) + meta2[:, META_R1].astype(jnp.int32)
    dest2 = jnp.take(offset, i2) + meta2[:, META_R2].astype(jnp.int32)
    dest = jnp.stack([dest1.reshape(t // ts, ts), dest2.reshape(t // ts, ts)], axis=1).reshape(-1)
    jt = jnp.arange(n_tiles, dtype=jnp.int32)
    total = ends[ne - 1]
    tv = (jt < total).astype(jnp.int32)
    te_all = jnp.minimum(jnp.sum((jt[:, None] >= ends[None, :]).astype(jnp.int32), axis=1), ne - 1)
    te_last = jnp.take(te_all, jnp.maximum(total - 1, 0))
    te = jnp.where(tv == 1, te_all, te_last)
    n_rows = n_tiles * tm
    past = total + jnp.arange(ne, dtype=jnp.int32)
    zero_tiles = jnp.concatenate([jnp.maximum(ends - 1, 0), jnp.minimum(past, n_tiles - 1)])
    zero_flags = jnp.concatenate([tiles_e > 0, past < n_tiles]).astype(jnp.int32)
    dest = jnp.concatenate([dest, zero_tiles.astype(jnp.int32) * tm, zero_flags])

    xs = _moe_scatter(h.reshape(t, d), dest, n_rows, ts, tm)
    ys = _moe_experts(xs, te, tv, wg, wu, wd, tm, tf)
    out = _moe_combine(x.reshape(t, d), mod, mod_row0, l, meta2, nw2, ys, dest, ts)
    return out.reshape(bsz, l, d)


def _pick_tile(n, prefs):
    for t in prefs:
        if n % t == 0:
            return t
    return n


def kernel(x, c, ctx, c_ctx, w_mod, b_mod, norm_pre_mix, norm_post_mix, norm_pre_ffn, norm_post_ffn, w_in, w_out, ml_gate_b, ml_norm, hg_lb_logits, hg_norm, da_lambda, da_norm, ffn_w_gate, ffn_w_up, ffn_w_down, router_w, router_b, moe_w_gate, moe_w_up, moe_w_down):
    depth = w_mod.shape[0]
    bsz, l, d = x.shape
    lctx = ctx.shape[1]

    mod = _mod_vectors(c, c_ctx, w_mod, b_mod)
    rows = mod.shape[1]
    mod = mod.reshape(depth * rows, 1, 6 * d)

    lb_soft = jax.nn.softmax(hg_lb_logits.astype(f32), axis=0)
    lower_bounds = jnp.cumsum(lb_soft, axis=0) - lb_soft[0]

    tables = _rope_tables(l)
    ml_lc = min(ML_CHUNK, lctx)
    tri_ml_x = _tri_pair(min(ML_CHUNK, l), min(ML_SUB, l))
    tri_ml_c = _tri_pair(ml_lc, min(ML_SUB, ml_lc))
    tri_hg_x = _tri_pair(min(HG_GROUP, l), HG_CHUNK)
    tri_hg_c = _tri_pair(min(HG_GROUP, lctx), HG_CHUNK)
    hg_bias = _hgrn2_bias()
    ml_sel = _mlstm_sel()

    tm_x = _pick_tile(l, (512, 256, 128))
    tm_c = _pick_tile(lctx, (256, 128))
    tq = _pick_tile(l, (512, 256, 128))

    for layer in range(depth):
        need_ctx = layer < depth - 1
        lam_init = 0.8 - 0.6 * math.exp(-0.3 * layer)
        row0 = layer * rows
        w_in_l = _prep_w_in(w_in[layer])
        w_out_l = w_out[layer].astype(bf16)
        nw_pre = norm_pre_mix[layer].reshape(1, d)

        pc = _inproj(ctx, mod, row0 + bsz, False, nw_pre, w_in_l, None, tm_c)
        px = _inproj(x, mod, row0, True, nw_pre, w_in_l, tables, tm_x)
        ml_c, mlg_c, hg_c, hgf_c, da_c = pc
        ml_x, mlg_x, hg_x, hgf_x, da_x = px

        gb = ml_gate_b[layer].astype(f32)
        z = jnp.zeros((LANES - 8,), f32)
        gate_b2 = jnp.stack([jnp.concatenate([gb[0:8], z]), jnp.concatenate([gb[8:16], z])]).reshape(2, 1, LANES)
        s0 = (jnp.zeros((2, bsz, ML_H, LANES, 2 * LANES), f32), jnp.zeros((2, bsz, ML_H, 8, LANES), f32))
        mlo_c, s_ml = _mlstm(ml_c, mlg_c, gate_b2, tri_ml_c, ml_sel, s0)
        mlo_x, _ = _mlstm(ml_x, mlg_x, gate_b2, tri_ml_x, ml_sel, s_ml)

        lb = lower_bounds[layer]
        lbc = jnp.zeros((8, HG_W), f32)
        lbc = lbc.at[0].set(jnp.log(jnp.maximum(lb, LB_FLOOR))).at[1].set(jnp.log1p(-lb)).at[2].set(1.0 - lb)
        h0 = jnp.zeros((2, bsz, HG_W, HG_W // 2), f32)
        hgo_c, s_hg = _hgrn2(hg_c, hgf_c, lbc, tri_hg_c, hg_bias, h0)
        hgo_x, _ = _hgrn2(hg_x, hgf_x, lbc, tri_hg_x, hg_bias, s_hg)

        lam_p = da_lambda[layer].astype(f32)
        da_nw = da_norm[layer].reshape(1, DA_V)
        dao_x = _attn(da_x, da_c, lam_p, da_nw, lam_init, tq)

        mlw = ml_norm[layer].reshape(1, 256)
        hgw = hg_norm[layer].reshape(1, 256)
        nw_post = norm_post_mix[layer].reshape(1, d)
        x = _outproj(x, mod, row0, True, mlo_x, ml_x, hgo_x, hg_x, dao_x, mlw, hgw, w_out_l, nw_post,
                     _pick_tile(l, (1024, 512, 256, 128)), "outproj_x")
        if need_ctx:
            dao_c = _attn(da_c, da_c, lam_p, da_nw, lam_init, lctx)
            ctx = _outproj(ctx, mod, row0 + bsz, False, mlo_c, ml_c, hgo_c, hg_c, dao_c, mlw, hgw, w_out_l, nw_post, tm_c, "outproj_c")

        i = layer // 2
        nw1 = norm_pre_ffn[layer].reshape(1, d)
        nw2 = norm_post_ffn[layer].reshape(1, d)
        if layer % 2 == 0:
            wg = ffn_w_gate[i].astype(bf16)[None]
            wu = ffn_w_up[i].astype(bf16)[None]
            wd = ffn_w_down[i].astype(bf16)[None]
            router = None
        else:
            wg = moe_w_gate[i].astype(bf16)
            wu = moe_w_up[i].astype(bf16)
            wd = moe_w_down[i].astype(bf16)
            rw = jnp.zeros((d, LANES), f32).at[:, :N_EXPERTS].set(router_w[i])
            rb = jnp.zeros((1, LANES), f32).at[0, :N_EXPERTS].set(router_b[i])
            router = (rw, rb)
        ff = wg.shape[-1]
        tf = _pick_tile(ff, (896, 1408, 512, 256))
        tm_f = _pick_tile(l, (1024, 512, 256, 128) if tf <= 896 else (512, 256, 128))
        if router is None:
            x = _ffn(x, mod, row0, True, nw1, wg, wu, wd, nw2, tm_f, tf, None, name="ffn_x%d" % layer)
        else:
            x = _moe_sparse(x, mod, row0, nw1, rw, rb, wg, wu, wd, nw2)
        if need_ctx:
            ctx = _ffn(ctx, mod, row0 + bsz, False, nw1, wg, wu, wd, nw2, tm_c, tf, router, name="ffn_c%d" % layer)
    return x
```

```python
import functools
import math

import numpy as np
import jax
import jax.numpy as jnp
from jax import lax
from jax.experimental import pallas as pl
from jax.experimental.pallas import tpu as pltpu

f32 = jnp.float32
bf16 = jnp.bfloat16

ML_H = 4
HG_H = 4
DA_H = 4
GRID_W = 64
ROPE_BASE = 10000.0
N_EXPERTS = 8
EPS = 1e-6
MASK_NEG = -1e30
LB_FLOOR = 1e-30

LANES = 128
ML_CHUNK = 256
ML_SUB = 128
HG_CHUNK = 16
HG_GROUP = 1024
VMEM_LIMIT = 56 * 1024 * 1024


def _cparams(sem):
    return pltpu.CompilerParams(dimension_semantics=sem, vmem_limit_bytes=VMEM_LIMIT)


def _sigmoid(x):
    return 1.0 / (1.0 + jnp.exp(-x))


def _log_sigmoid(x):
    return jnp.minimum(x, 0.0) - jnp.log(1.0 + jnp.exp(-jnp.abs(x)))


def _split_hi_lo(a):
    hi = a.astype(bf16)
    lo = (a - hi.astype(f32)).astype(bf16)
    return hi, lo


def _dot(a, b):
    return jnp.dot(a, b, preferred_element_type=f32)


def _dot_nt(a, b):
    return lax.dot_general(a, b, (((1,), (1,)), ((), ())), preferred_element_type=f32)


def _dot_tn(a, b):
    return lax.dot_general(a, b, (((0,), (0,)), ((), ())), preferred_element_type=f32)


def _dot2(a_f32, b_bf16):
    hi, lo = _split_hi_lo(a_f32)
    return _dot(hi, b_bf16) + _dot(lo, b_bf16)


def _mod_kernel(s_ref, w_ref, b_ref, o_ref):
    s = s_ref[...]
    s = s * _sigmoid(s)
    o_ref[0] = _dot(s.astype(bf16), w_ref[0].astype(bf16)) + b_ref[0]


def _mod_vectors(c, c_ctx, w_mod, b_mod):
    depth, d, n = w_mod.shape
    bsz = c.shape[0]
    rows = 8
    s = jnp.zeros((rows, d), f32).at[:bsz].set(c).at[bsz].set(c_ctx)
    tn = n // 4
    out = pl.pallas_call(
        _mod_kernel,
        grid=(depth, n // tn),
        in_specs=[
            pl.BlockSpec((rows, d), lambda l, j: (0, 0)),
            pl.BlockSpec((1, d, tn), lambda l, j: (l, 0, j)),
            pl.BlockSpec((1, 1, tn), lambda l, j: (l, 0, j)),
        ],
        out_specs=pl.BlockSpec((1, rows, tn), lambda l, j: (l, 0, j)),
        out_shape=jax.ShapeDtypeStruct((depth, rows, n), f32),
        compiler_params=_cparams(("parallel", "parallel")),
        name="mod_vectors",
    )(s, w_mod, b_mod.reshape(depth, 1, n))
    return out


P_ML = (0, 1024)
P_MLG = (1024, 1280)
P_HG = (1280, 2048)
P_HGF = (2048, 2560)
P_DA = (2560, 4096)
P_TOTAL = 4096


def _prep_w_in(w):
    d = w.shape[0]
    ml = w[:, 0:1024]
    g = w[:, 1024:1040]
    z = jnp.zeros((d, LANES - 8), w.dtype)
    gates = jnp.concatenate([g[:, 0:8], z, g[:, 8:16], z], axis=1)
    hg_q = w[:, 1040:1296]
    hg_f = w[:, 1296:1808]
    hg_i = w[:, 1808:2064]
    hg_g = w[:, 2064:2320]
    da = w[:, 2320:3856]
    return jnp.concatenate([ml, gates, hg_q, hg_i, hg_g, hg_f, da], axis=1).astype(bf16)


def _inproj_kernel(*refs, rope):
    if rope:
        x_ref, mod_ref, nw_ref, w_ref, cos_ref, sin_ref = refs[:6]
        outs = refs[6:]
    else:
        x_ref, mod_ref, nw_ref, w_ref = refs[:4]
        outs = refs[4:]
    ml_ref, mlg_ref, hg_ref, hgf_ref, da_ref = outs
    d = x_ref.shape[-1]
    x = x_ref[0]
    ms = jnp.mean(x * x, axis=-1, keepdims=True)
    h = x * lax.rsqrt(ms + EPS) * nw_ref[...]
    mod = mod_ref[0]
    h = h * (1.0 + mod[:, d:2 * d]) + mod[:, 0:d]
    hb = h.astype(bf16)

    def proj(lo, hi):
        return _dot(hb, w_ref[:, lo:hi])

    ml_ref[0] = proj(*P_ML).astype(bf16)
    mlg_ref[0] = proj(*P_MLG)
    pq = proj(P_HG[0], P_HG[0] + 256)
    hg_ref[0, :, 0:256] = (pq * _sigmoid(pq)).astype(bf16)
    hg_ref[0, :, 256:768] = proj(P_HG[0] + 256, P_HG[1]).astype(bf16)
    hgf_ref[0] = proj(*P_HGF)
    q = proj(P_DA[0], P_DA[0] + 512)
    k = proj(P_DA[0] + 512, P_DA[0] + 1024)
    if rope:
        cos = jnp.concatenate([cos_ref[...]] * 4, axis=1)
        sin = jnp.concatenate([sin_ref[...]] * 4, axis=1)
        lane = lax.broadcasted_iota(jnp.int32, (1, 512), 1)
        first = (lane % 32) < 16

        def rot(t):
            sw = jnp.where(first, pltpu.roll(t, 512 - 16, 1), pltpu.roll(t, 16, 1))
            return t * cos + sw * sin

        q = rot(q)
        k = rot(k)
    da_ref[0, :, 0:512] = (q * 0.125).astype(bf16)
    da_ref[0, :, 512:1024] = k.astype(bf16)
    da_ref[0, :, 1024:1536] = proj(P_DA[0] + 1024, P_DA[1]).astype(bf16)


def _inproj(x, mod, mod_row0, per_batch, nw, w, tables, tm):
    bsz, l, d = x.shape
    rope = tables is not None
    nt = l // tm
    mod_map = (lambda b, i: (mod_row0 + b, 0, 0)) if per_batch else (lambda b, i: (mod_row0, 0, 0))
    in_specs = [
        pl.BlockSpec((1, tm, d), lambda b, i: (b, i, 0)),
        pl.BlockSpec((1, 1, mod.shape[-1]), mod_map),
        pl.BlockSpec((1, d), lambda b, i: (0, 0)),
        pl.BlockSpec((d, P_TOTAL), lambda b, i: (0, 0), pipeline_mode=pl.Buffered(1)),
    ]
    args = [x, mod, nw, w]
    if rope:
        in_specs += [pl.BlockSpec((tm, LANES), lambda b, i: (i, 0))] * 2
        args += list(tables)
    widths = (1024, 256, 768, 512, 1536)
    dtypes = (bf16, f32, bf16, f32, bf16)
    out_specs = [pl.BlockSpec((1, tm, wd), lambda b, i: (b, i, 0)) for wd in widths]
    out_shape = [jax.ShapeDtypeStruct((bsz, l, wd), dt) for wd, dt in zip(widths, dtypes)]
    return pl.pallas_call(
        functools.partial(_inproj_kernel, rope=rope),
        grid=(bsz, nt),
        in_specs=in_specs,
        out_specs=out_specs,
        out_shape=out_shape,
        compiler_params=_cparams(("parallel", "parallel")),
        name="inproj_x" if rope else "inproj_c",
    )(*args)


def _rope_tables(l):
    rows = l // GRID_W
    row = jnp.repeat(jnp.arange(rows), GRID_W).astype(f32)
    col = jnp.tile(jnp.arange(GRID_W), rows).astype(f32)
    n_freq = 16
    inv = ROPE_BASE ** (-jnp.arange(n_freq, dtype=f32) / n_freq)
    lane = jnp.arange(LANES)
    freq = inv[lane % n_freq]
    pos = jnp.where(((lane % 64) // 32 == 0)[None, :], row[:, None], col[:, None])
    ang = pos * freq[None, :]
    sign = jnp.where((lane % 32) < 16, -1.0, 1.0)[None, :]
    return jnp.cos(ang), jnp.sin(ang) * sign


ML_D = 64


def _mlstm_kernel(q_ref, k_ref, v_ref, g_ref, gb_ref, tri_ref, sel_ref, cn0_ref, m0_ref,
                  o_ref, cnf_ref, mf_ref, cn_scr, m_scr):
    d = pl.program_id(0)
    j = pl.program_id(2)
    nj = pl.num_programs(2)
    lc = q_ref.shape[1]

    @pl.when(j == 0)
    def _():
        cn_scr[...] = cn0_ref[0, 0]
        m_scr[...] = m0_ref[0, 0]

    sub = min(ML_SUB, lc)
    nsub = lc // sub

    def run(order, fwd):
        g = g_ref[0] + gb_ref[0]
        lf = _log_sigmoid(g)
        hi, lo = _split_hi_lo(lf)
        bc = _dot(tri_ref[0], hi) + _dot(tri_ref[0], lo)
        z = pltpu.roll(g, 4, 1) - bc
        zt = z.T
        sel = sel_ref[...]
        yrep = _dot2(bc, sel)
        zrep = _dot2(z, sel)
        r = lax.broadcasted_iota(jnp.int32, (sub, sub), 0)
        c = lax.broadcasted_iota(jnp.int32, (sub, sub), 1)
        lane = lax.broadcasted_iota(jnp.int32, (sub, LANES), 1)
        ones_b = jnp.ones((sub, LANES), bf16)
        mask = (r >= c) if fwd else (r <= c)
        last = sub - 1 if fwd else 0

        def rep(x, n):
            return jnp.concatenate([x] * n, axis=1)

        pre = {}
        for cs in order:
            rows = slice(cs * sub, (cs + 1) * sub)
            for h in range(ML_H):
                p, hh = divmod(h, 2)
                half = (lane < ML_D) if hh == 0 else (lane >= ML_D)
                qp = q_ref[0, rows, p * LANES:(p + 1) * LANES]
                kp = k_ref[0, rows, p * LANES:(p + 1) * LANES] * jnp.asarray(0.125, bf16)
                vp = v_ref[0, rows, p * LANES:(p + 1) * LANES]
                b_rep = yrep[rows, h * LANES:(h + 1) * LANES]
                blast = b_rep[last:last + 1, :]
                qm = jnp.where(half, qp, jnp.zeros_like(qp))
                vo = jnp.concatenate([jnp.where(half, vp, jnp.zeros_like(vp)), ones_b], axis=1)
                s_raw = _dot_nt(qm, kp)
                a_rep = blast + zrep[rows, h * LANES:(h + 1) * LANES]
                m_loc = jnp.max(a_rep, axis=0, keepdims=True)
                w = jnp.exp(a_rep - m_loc)
                kw = jnp.where(half, kp.astype(f32) * w, 0.0).astype(bf16)
                dcn = _dot_tn(kw, vo)
                pre[cs, h] = (b_rep, blast, qm, vo, s_raw, m_loc, dcn)

        for cs in order:
            rows = slice(cs * sub, (cs + 1) * sub)
            outs = []
            for h in range(ML_H):
                b_rep, blast, qm, vo, s_raw, m_loc, dcn = pre.pop((cs, h))
                m0 = m_scr[h, 0:1, :]
                cn = cn_scr[h]
                nd_inter = _dot(qm, cn.astype(bf16))
                z_row = zt[4 + h:5 + h, rows]
                dm = jnp.where(mask, rep(b_rep, sub // LANES) + z_row, MASK_NEG)
                inter = b_rep + m0
                m = jnp.maximum(inter, jnp.max(dm, axis=-1, keepdims=True))
                w_intra = jnp.exp(dm - rep(m, sub // LANES))
                w_inter = jnp.exp(inter - m)
                nd = _dot((s_raw * w_intra).astype(bf16), vo) + rep(w_inter, 2) * nd_inter
                den = nd[:, LANES:2 * LANES]
                outs.append(nd[:, 0:LANES] / jnp.maximum(jnp.abs(den), jnp.exp(-m)))

                m_new = jnp.maximum(blast + m0, m_loc)
                s_old = jnp.exp(blast + m0 - m_new)
                s_new = jnp.exp(m_loc - m_new)
                cn_scr[h] = rep(s_old, 2) * cn + rep(s_new, 2) * dcn
                m_scr[h] = jnp.broadcast_to(m_new, (8, LANES))
            for p in range(ML_H // 2):
                o_ref[0, 0, rows, p * LANES:(p + 1) * LANES] = outs[2 * p] + outs[2 * p + 1]

    @pl.when(d == 0)
    def _():
        run(range(nsub), True)

    @pl.when(d == 1)
    def _():
        run(range(nsub - 1, -1, -1), False)

    @pl.when(j == nj - 1)
    def _():
        cnf_ref[0, 0] = cn_scr[...]
        mf_ref[0, 0] = m_scr[...]


def _mlstm_sel():
    sel = np.zeros((LANES, ML_H * LANES), np.float32)
    for h in range(ML_H):
        sel[4 + h, h * LANES:(h + 1) * LANES] = 1.0
    return jnp.asarray(sel, dtype=bf16)


def _mlstm(ml, mlg, gate_b2, tri, sel, state):
    bsz, l, _ = ml.shape
    lc = min(ML_CHUNK, l)
    nj = l // lc
    cn0, m0 = state

    def tmap(col):
        return lambda d, b, j: (b, jnp.where(d == 0, j, nj - 1 - j), col)

    cn_spec = pl.BlockSpec((1, 1, ML_H, LANES, 2 * LANES), lambda d, b, j: (d, b, 0, 0, 0))
    m_spec = pl.BlockSpec((1, 1, ML_H, 8, LANES), lambda d, b, j: (d, b, 0, 0, 0))
    out, cnf, mf = pl.pallas_call(
        _mlstm_kernel,
        grid=(2, bsz, nj),
        in_specs=[
            pl.BlockSpec((1, lc, 256), tmap(0)),
            pl.BlockSpec((1, lc, 256), tmap(1)),
            pl.BlockSpec((1, lc, 256), tmap(2)),
            pl.BlockSpec((1, lc, LANES), lambda d, b, j: (b, jnp.where(d == 0, j, nj - 1 - j), d)),
            pl.BlockSpec((1, 1, LANES), lambda d, b, j: (d, 0, 0)),
            pl.BlockSpec((1, lc, lc), lambda d, b, j: (d, 0, 0)),
            pl.BlockSpec((LANES, ML_H * LANES), lambda d, b, j: (0, 0)),
            cn_spec,
            m_spec,
        ],
        out_specs=[
            pl.BlockSpec((1, 1, lc, 256), lambda d, b, j: (d, b, jnp.where(d == 0, j, nj - 1 - j), 0)),
            cn_spec,
            m_spec,
        ],
        out_shape=[
            jax.ShapeDtypeStruct((2, bsz, l, 256), f32),
            jax.ShapeDtypeStruct((2, bsz, ML_H, LANES, 2 * LANES), f32),
            jax.ShapeDtypeStruct((2, bsz, ML_H, 8, LANES), f32),
        ],
        scratch_shapes=[pltpu.VMEM((ML_H, LANES, 2 * LANES), f32), pltpu.VMEM((ML_H, 8, LANES), f32)],
        compiler_params=_cparams(("parallel", "parallel", "arbitrary")),
        name="mlstm",
    )(ml, ml, ml, mlg, gate_b2, tri, sel, cn0, m0)
    return out, (cnf, mf)


def _tri_pair(n, block):
    r = np.arange(n)[:, None]
    c = np.arange(n)[None, :]
    same = (r // block) == (c // block)
    lower = (same & (r >= c)).astype(np.float32)
    upper = (same & (r <= c)).astype(np.float32)
    return jnp.asarray(np.stack([lower, upper]), dtype=bf16)


HG_W = 256
HG_FACTOR_MAX = 80.0


def _hgrn2_kernel(q_ref, v_ref, f_ref, lbc_ref, tri_ref, bias_ref, s0_ref, o_ref, sf_ref, st_ref, p_scr):
    d = pl.program_id(0)
    j = pl.program_id(2)
    nj = pl.num_programs(2)
    lg = q_ref.shape[1]
    nck = lg // HG_CHUNK
    hw = HG_W // 2

    @pl.when(j == 0)
    def _():
        st_ref[...] = s0_ref[0, 0]

    fr = f_ref[0]
    log_lb = lbc_ref[0:1, :]
    log_1m_lb = lbc_ref[1:2, :]
    one_m_lb = lbc_ref[2:3, :]
    t = jnp.exp(-jnp.abs(fr))
    b = log_1m_lb + (jnp.minimum(fr, 0.0) - jnp.log(1.0 + t))
    kk = one_m_lb * (jnp.where(fr >= 0.0, t, 1.0) / (1.0 + t))
    log_f = jnp.maximum(log_lb, b) + jnp.log(1.0 + jnp.exp(-jnp.abs(log_lb - b)))
    hi, lo = _split_hi_lo(log_f)
    g = _dot(tri_ref[0], hi) + _dot(tri_ref[0], lo)
    q = q_ref[0].astype(f32)
    qe = (q * jnp.exp(g)).astype(bf16)

    rr = lax.broadcasted_iota(jnp.int32, (hw, hw), 0) // 64
    cc = lax.broadcasted_iota(jnp.int32, (hw, hw), 1) // 64
    same_head = rr == cc
    bd_b = jnp.where(same_head, 1.0, 0.0).astype(bf16)

    sub = 8
    nsub = HG_CHUNK // sub

    lane_head = lax.broadcasted_iota(jnp.int32, (HG_CHUNK, HG_W), 1) // 64
    row_t = lax.broadcasted_iota(jnp.int32, (HG_H * HG_CHUNK, HG_CHUNK), 0) % HG_CHUNK
    col_s = lax.broadcasted_iota(jnp.int32, (HG_H * HG_CHUNK, HG_CHUNK), 1)

    def run(order, last_row, fwd, factored):
        causal = (row_t >= col_s) if fwd else (row_t <= col_s)

        def block_kind(s, tb):
            lo, hi = tb * sub, tb * sub + sub - 1
            if fwd:
                return "all" if lo >= s else ("none" if hi < s else "some")
            return "all" if hi <= s else ("none" if lo > s else "some")

        def build_p(c):
            r0 = c * HG_CHUNK
            gc = g[r0:r0 + HG_CHUNK]
            kc = kk[r0:r0 + HG_CHUNK]
            qc = q[r0:r0 + HG_CHUNK]
            for s in range(HG_CHUNK):
                blocks = []
                for tb in range(nsub):
                    kind = block_kind(s, tb)
                    if kind == "none":
                        blocks.append(jnp.zeros((sub, HG_W), f32))
                        continue
                    diff = gc[tb * sub:(tb + 1) * sub] - gc[s:s + 1, :]
                    if kind == "some":
                        diff = diff + bias_ref[0, s * HG_CHUNK + tb * sub:s * HG_CHUNK + (tb + 1) * sub, :]
                    blocks.append(qc[tb * sub:(tb + 1) * sub] * kc[s:s + 1, :] * jnp.exp(diff))
                p_scr[c, s * HG_CHUNK:(s + 1) * HG_CHUNK, :] = jnp.concatenate(blocks, axis=0).astype(bf16)

        def intra_direct(c):
            r0 = c * HG_CHUNK
            vc = v_ref[0, r0:r0 + HG_CHUNK, :].astype(f32)
            abc = jnp.concatenate([_dot(p_scr[c, :, 0:hw], bd_b), _dot(p_scr[c, :, hw:HG_W], bd_b)], axis=1)
            o_blocks = []
            for tb in range(nsub):
                acc = None
                for s in range(HG_CHUNK):
                    if block_kind(s, tb) == "none":
                        continue
                    term = abc[s * HG_CHUNK + tb * sub:s * HG_CHUNK + (tb + 1) * sub, :] * vc[s:s + 1, :]
                    acc = term if acc is None else acc + term
                o_blocks.append(acc)
            return jnp.concatenate(o_blocks, axis=0)

        def scores_factored(c):
            r0 = c * HG_CHUNK
            qec = qe[r0:r0 + HG_CHUNK]
            kx = (kk[r0:r0 + HG_CHUNK] * jnp.exp(-g[r0:r0 + HG_CHUNK])).astype(bf16)
            qs = jnp.concatenate([jnp.where(lane_head == h, qec, jnp.zeros_like(qec)) for h in range(HG_H)], axis=0)
            return jnp.where(causal, _dot_nt(qs, kx), 0.0).astype(bf16)

        def intra_factored(c, a):
            o_all = _dot(a, v_ref[0, c * HG_CHUNK:(c + 1) * HG_CHUNK, :])
            o = None
            for h in range(HG_H):
                term = jnp.where(lane_head == h, o_all[h * HG_CHUNK:(h + 1) * HG_CHUNK], 0.0)
                o = term if o is None else o + term
            return o

        def state_delta(c):
            r0 = c * HG_CHUNK
            gc = g[r0:r0 + HG_CHUNK]
            vcb = v_ref[0, r0:r0 + HG_CHUNK, :]
            g_last = gc[last_row:last_row + 1, :]
            kt = (kk[r0:r0 + HG_CHUNK] * jnp.exp(g_last - gc)).astype(bf16)
            return (jnp.exp(g_last), _dot_tn(vcb[:, 0:hw], kt[:, 0:hw]), _dot_tn(vcb[:, hw:HG_W], kt[:, hw:HG_W]))

        def finish(c, o, delta):
            r0 = c * HG_CHUNK
            dec, d_lo, d_hi = delta
            st = st_ref[...]
            stb = st.astype(bf16)
            qec = qe[r0:r0 + HG_CHUNK]
            o = o + jnp.concatenate([_dot_nt(qec[:, 0:hw], stb[0:hw]), _dot_nt(qec[:, hw:HG_W], stb[hw:HG_W])], axis=1)
            o_ref[0, 0, r0:r0 + HG_CHUNK, :] = o
            st_ref[0:hw, :] = st[0:hw] * dec[:, 0:hw] + jnp.where(same_head, d_lo, 0.0)
            st_ref[hw:HG_W, :] = st[hw:HG_W] * dec[:, hw:HG_W] + jnp.where(same_head, d_hi, 0.0)

        order = list(order)
        if factored:
            n = len(order)
            stage_a, stage_b = {}, {}
            for i in range(n + 2):
                if i < n:
                    stage_a[order[i]] = (scores_factored(order[i]), state_delta(order[i]))
                if 0 <= i - 1 < n:
                    c = order[i - 1]
                    stage_b[c] = intra_factored(c, stage_a[c][0])
                if 0 <= i - 2 < n:
                    c = order[i - 2]
                    finish(c, stage_b.pop(c), stage_a.pop(c)[1])
        else:
            for c in order:
                build_p(c)
                finish(c, intra_direct(c), state_delta(c))

    can_factor = jnp.max(-g) < HG_FACTOR_MAX
    for direction, (order, last_row) in enumerate(((range(nck), HG_CHUNK - 1), (range(nck - 1, -1, -1), 0))):
        for factored in (True, False):
            @pl.when((d == direction) & (can_factor if factored else jnp.logical_not(can_factor)))
            def _(order=order, last_row=last_row, direction=direction, factored=factored):
                run(order, last_row, direction == 0, factored)

    @pl.when(j == nj - 1)
    def _():
        sf_ref[0, 0] = st_ref[...]


def _hgrn2(hg, hgf, lbc, tri, bias, s0):
    bsz, l, _ = hg.shape
    lg = min(HG_GROUP, l)
    nj = l // lg
    hw = HG_W // 2

    def tmap(col):
        return lambda d, b, j: (b, jnp.where(d == 0, j, nj - 1 - j), col)

    out, sf = pl.pallas_call(
        _hgrn2_kernel,
        grid=(2, bsz, nj),
        in_specs=[
            pl.BlockSpec((1, lg, HG_W), tmap(0)),
            pl.BlockSpec((1, lg, HG_W), tmap(1)),
            pl.BlockSpec((1, lg, HG_W), lambda d, b, j: (b, jnp.where(d == 0, j, nj - 1 - j), d)),
            pl.BlockSpec((8, HG_W), lambda d, b, j: (0, 0)),
            pl.BlockSpec((1, lg, lg), lambda d, b, j: (d, 0, 0)),
            pl.BlockSpec((1, HG_CHUNK * HG_CHUNK, HG_W), lambda d, b, j: (d, 0, 0)),
            pl.BlockSpec((1, 1, HG_W, hw), lambda d, b, j: (d, b, 0, 0)),
        ],
        out_specs=[
            pl.BlockSpec((1, 1, lg, HG_W), lambda d, b, j: (d, b, jnp.where(d == 0, j, nj - 1 - j), 0)),
            pl.BlockSpec((1, 1, HG_W, hw), lambda d, b, j: (d, b, 0, 0)),
        ],
        out_shape=[
            jax.ShapeDtypeStruct((2, bsz, l, HG_W), f32),
            jax.ShapeDtypeStruct((2, bsz, HG_W, hw), f32),
        ],
        scratch_shapes=[
            pltpu.VMEM((HG_W, hw), f32),
            pltpu.VMEM((lg // HG_CHUNK, HG_CHUNK * HG_CHUNK, HG_W), bf16),
        ],
        compiler_params=_cparams(("parallel", "parallel", "arbitrary")),
        name="hgrn2",
    )(hg, hg, hgf, lbc, tri, bias, s0)
    return out, sf


def _hgrn2_bias():
    idx = np.arange(HG_CHUNK * HG_CHUNK)
    s, t = idx // HG_CHUNK, idx % HG_CHUNK
    fwd = np.where(t >= s, 0.0, MASK_NEG).astype(np.float32)
    bwd = np.where(t <= s, 0.0, MASK_NEG).astype(np.float32)
    return jnp.asarray(np.broadcast_to(np.stack([fwd, bwd])[:, :, None], (2, HG_CHUNK * HG_CHUNK, HG_W)))


DA_V = 128


def _attn_kernel(*refs, has_x, lam_init, kg, nq):
    if has_x:
        (q_ref, kx_ref, kc_ref, vx_ref, vc_ref, lp_ref, nw_ref, o_ref,
         s0_scr, s1_scr, m0_scr, m1_scr, a0_scr, a1_scr, va_scr) = refs
        lx = kx_ref.shape[1]
    else:
        q_ref, kc_ref, vc_ref, lp_ref, nw_ref, o_ref, s0_scr, s1_scr, m0_scr, m1_scr, a0_scr, a1_scr, va_scr = refs
        lx = 0
    tq = q_ref.shape[1]
    lctx = kc_ref.shape[1]
    t = pl.program_id(0)

    @pl.when(t == 0)
    def _():
        s1_scr[...] = jnp.zeros(s1_scr.shape, f32)
        m1_scr[...] = jnp.zeros(m1_scr.shape, f32)
        a1_scr[...] = jnp.ones(a1_scr.shape, f32)

    @pl.when((t == 0) | ((t - 1) % nq == 0))
    def _():
        def ones_col(n):
            lane = lax.broadcasted_iota(jnp.int32, (n, DA_V), 1)
            return jnp.where(lane == 0, 1.0, 0.0).astype(bf16)

        if has_x:
            va_scr[0:lx, 0:DA_V] = vx_ref[0]
            va_scr[0:lx, DA_V:2 * DA_V] = ones_col(lx)
        va_scr[lx:lx + lctx, 0:DA_V] = vc_ref[0]
        va_scr[lx:lx + lctx, DA_V:2 * DA_V] = ones_col(lctx)

    groups = ([(kx_ref, g * kg, kg, g * kg) for g in range(lx // kg)] if has_x else []) + [(kc_ref, 0, lctx, lx)]

    def step(sa_scr, ma_scr, aa_scr, sb_scr, mb_scr, ab_scr):
        q = q_ref[0]
        lane = lax.broadcasted_iota(jnp.int32, (tq, DA_V), 1)
        zero = jnp.zeros_like(q)
        qq = jnp.concatenate([jnp.where(lane < 64, q, zero), jnp.where(lane >= 64, q, zero)], axis=0)
        m128 = None
        for ref, r0, n, c0 in groups:
            s = _dot_nt(qq, ref[0, r0:r0 + n, :])
            sa_scr[:, c0:c0 + n] = s
            for c in range(n // LANES):
                blk = s[:, c * LANES:(c + 1) * LANES]
                m128 = blk if m128 is None else jnp.maximum(m128, blk)
        ma_scr[...] = jnp.broadcast_to(jnp.max(m128, axis=-1, keepdims=True), m128.shape)
        m_prev = mb_scr[...]
        acc = None
        for ref, r0, n, c0 in groups:
            m_rep = jnp.concatenate([m_prev] * (n // LANES), axis=1)
            p = jnp.exp(sb_scr[:, c0:c0 + n] - m_rep).astype(bf16)
            part = _dot(p, va_scr[c0:c0 + n, :])
            acc = part if acc is None else acc + part
        aa_scr[...] = acc
        acc = ab_scr[:, 0:DA_V]
        l = ab_scr[:, DA_V:DA_V + 1]
        lp = lp_ref[...]
        lam = (jnp.exp(jnp.sum(lp[0:1] * lp[1:2], axis=-1, keepdims=True))
               - jnp.exp(jnp.sum(lp[2:3] * lp[3:4], axis=-1, keepdims=True)) + lam_init)
        o = acc[0:tq] / l[0:tq] - lam * (acc[tq:2 * tq] / l[tq:2 * tq])
        ms = jnp.mean(o * o, axis=-1, keepdims=True)
        o_ref[0] = (o * lax.rsqrt(ms + EPS) * nw_ref[...] * (1.0 - lam_init)).astype(bf16)

    @pl.when(t % 2 == 0)
    def _():
        step(s0_scr, m0_scr, a0_scr, s1_scr, m1_scr, a1_scr)

    @pl.when(t % 2 == 1)
    def _():
        step(s1_scr, m1_scr, a1_scr, s0_scr, m0_scr, a0_scr)


def _attn(da_q, da_c, lam_p, nw, lam_init, tq):
    has_x = da_q is not da_c
    bsz, l, _ = da_q.shape
    lctx = da_c.shape[1]
    nq = l // tq
    nt = bsz * DA_H * nq
    lx = l if has_x else 0
    kg = _pick_tile(l, (1024, 512, 256, 128))

    def split(t):
        return t // (DA_H * nq), (t // nq) % DA_H, t % nq

    def cur(col0, whole):
        def index(t):
            b, h, i = split(jnp.minimum(t, nt - 1))
            return (b, 0 if whole else i, col0 + h)
        return index

    def prev(col0, whole, lag=1):
        def index(t):
            b, h, i = split(jnp.clip(t - lag, 0, nt - 1))
            return (b, 0 if whole else i, col0 + h)
        return index

    in_specs = [pl.BlockSpec((1, tq, DA_V), cur(0, False))]
    args = [da_q]
    if has_x:
        in_specs += [pl.BlockSpec((1, l, DA_V), cur(4, True))]
        args += [da_q]
    in_specs += [pl.BlockSpec((1, lctx, DA_V), cur(4, True))]
    args += [da_c]
    if has_x:
        in_specs += [pl.BlockSpec((1, l, DA_V), prev(8, True))]
        args += [da_q]
    in_specs += [
        pl.BlockSpec((1, lctx, DA_V), prev(8, True)),
        pl.BlockSpec((4, 64), lambda t: (0, 0)),
        pl.BlockSpec((1, DA_V), lambda t: (0, 0)),
    ]
    args += [da_c, lam_p, nw]
    return pl.pallas_call(
        functools.partial(_attn_kernel, has_x=has_x, lam_init=lam_init, kg=kg, nq=nq),
        grid=(nt + 2,),
        in_specs=in_specs,
        out_specs=pl.BlockSpec((1, tq, DA_V), prev(0, False, lag=2)),
        out_shape=jax.ShapeDtypeStruct((bsz, l, DA_H * DA_V), bf16),
        scratch_shapes=[
            pltpu.VMEM((2 * tq, lx + lctx), f32),
            pltpu.VMEM((2 * tq, lx + lctx), f32),
            pltpu.VMEM((2 * tq, LANES), f32),
            pltpu.VMEM((2 * tq, LANES), f32),
            pltpu.VMEM((2 * tq, 2 * DA_V), f32),
            pltpu.VMEM((2 * tq, 2 * DA_V), f32),
            pltpu.VMEM((lx + lctx, 2 * DA_V), bf16),
        ],
        compiler_params=_cparams(("arbitrary",)),
        name="attn_x" if has_x else "attn_c",
    )(*args)


def _head_ms(y, bd_b):
    return _dot2(y * y, bd_b) * (1.0 / 64.0)


def _outproj_kernel(x_ref, mod_ref, mlf_ref, mlb_ref, mlo_ref, hgf_ref, hgb_ref, hgg_ref, da_ref,
                    mlw_ref, hgw_ref, w_ref, nw_ref, o_ref):
    d = x_ref.shape[-1]
    rr = lax.broadcasted_iota(jnp.int32, (256, 256), 0) // 64
    cc = lax.broadcasted_iota(jnp.int32, (256, 256), 1) // 64
    bd_b = jnp.where(rr == cc, 1.0, 0.0).astype(bf16)
    ml = mlf_ref[0, 0] + mlb_ref[0, 0]
    ml = ml * lax.rsqrt(_head_ms(ml, bd_b) + EPS) * mlw_ref[...]
    ml = _sigmoid(mlo_ref[0].astype(f32)) * ml
    hg = hgf_ref[0, 0] + hgb_ref[0, 0]
    hg = hg * lax.rsqrt(_head_ms(hg, bd_b) + EPS) * hgw_ref[...]
    gg = hgg_ref[0].astype(f32)
    hg = hg * (gg * _sigmoid(gg))
    mix = (_dot(ml.astype(bf16), w_ref[0:256, :]) + _dot(hg.astype(bf16), w_ref[256:512, :])
           + _dot(da_ref[0], w_ref[512:1024, :]))
    ms = jnp.mean(mix * mix, axis=-1, keepdims=True)
    y = mix * lax.rsqrt(ms + EPS) * nw_ref[...]
    o_ref[0] = x_ref[0] + mod_ref[0][:, 2 * d:3 * d] * y


def _outproj(x, mod, mod_row0, per_batch, ml_out, ml, hg_out, hg, da_out, mlw, hgw, w, nw, tm, name):
    bsz, l, d = x.shape
    nt = l // tm
    mod_map = (lambda b, i: (mod_row0 + b, 0, 0)) if per_batch else (lambda b, i: (mod_row0, 0, 0))
    return pl.pallas_call(
        _outproj_kernel,
        grid=(bsz, nt),
        in_specs=[
            pl.BlockSpec((1, tm, d), lambda b, i: (b, i, 0)),
            pl.BlockSpec((1, 1, mod.shape[-1]), mod_map),
            pl.BlockSpec((1, 1, tm, 256), lambda b, i: (0, b, i, 0)),
            pl.BlockSpec((1, 1, tm, 256), lambda b, i: (1, b, i, 0)),
            pl.BlockSpec((1, tm, 256), lambda b, i: (b, i, 3)),
            pl.BlockSpec((1, 1, tm, 256), lambda b, i: (0, b, i, 0)),
            pl.BlockSpec((1, 1, tm, 256), lambda b, i: (1, b, i, 0)),
            pl.BlockSpec((1, tm, 256), lambda b, i: (b, i, 2)),
            pl.BlockSpec((1, tm, 512), lambda b, i: (b, i, 0)),
            pl.BlockSpec((1, 256), lambda b, i: (0, 0)),
            pl.BlockSpec((1, 256), lambda b, i: (0, 0)),
            pl.BlockSpec((d, d), lambda b, i: (0, 0)),
            pl.BlockSpec((1, d), lambda b, i: (0, 0)),
        ],
        out_specs=pl.BlockSpec((1, tm, d), lambda b, i: (b, i, 0)),
        out_shape=jax.ShapeDtypeStruct((bsz, l, d), f32),
        compiler_params=_cparams(("parallel", "parallel")),
        name=name,
    )(x, mod, ml_out, ml_out, ml, hg_out, hg_out, hg, da_out, mlw, hgw, w, nw)


def _ffn_kernel(*refs, moe):
    if moe:
        x_ref, mod_ref, nw1_ref, rw_ref, rb_ref, wg_ref, wu_ref, wd_ref, nw2_ref, o_ref, h_scr, acc_scr, gate_scr = refs
    else:
        x_ref, mod_ref, nw1_ref, wg_ref, wu_ref, wd_ref, nw2_ref, o_ref, h_scr, acc_scr = refs
    e = pl.program_id(2)
    j = pl.program_id(3)
    ne = pl.num_programs(2)
    nj = pl.num_programs(3)
    d = x_ref.shape[-1]
    tm = x_ref.shape[1]

    @pl.when((e == 0) & (j == 0))
    def _():
        x = x_ref[0]
        ms = jnp.mean(x * x, axis=-1, keepdims=True)
        h = x * lax.rsqrt(ms + EPS) * nw1_ref[...]
        mod = mod_ref[0]
        h = h * (1.0 + mod[:, 4 * d:5 * d]) + mod[:, 3 * d:4 * d]
        h_scr[...] = h.astype(bf16)
        acc_scr[...] = jnp.zeros_like(acc_scr)
        if moe:
            hh, hl = _split_hi_lo(h)
            rw = rw_ref[...]
            wh, wl = _split_hi_lo(rw)
            logits = _dot(hh, wh) + _dot(hl, wh) + _dot(hh, wl) + rb_ref[...]
            lane = lax.broadcasted_iota(jnp.int32, (tm, LANES), 1)
            real = lane < N_EXPERTS
            lg = jnp.where(real, logits, MASK_NEG)
            m1 = jnp.max(lg, axis=-1, keepdims=True)
            i1 = jnp.min(jnp.where(lg == m1, lane, LANES), axis=-1, keepdims=True)
            lg2 = jnp.where(lane == i1, MASK_NEG, lg)
            m2 = jnp.max(lg2, axis=-1, keepdims=True)
            i2 = jnp.min(jnp.where(lg2 == m2, lane, LANES), axis=-1, keepdims=True)
            e2 = jnp.exp(m2 - m1)
            g1 = 1.0 / (1.0 + e2)
            g2 = e2 / (1.0 + e2)
            gate_scr[...] = jnp.where(lane == i1, g1, 0.0) + jnp.where(lane == i2, g2, 0.0)

    hb = h_scr[...]
    a = _dot(hb, wg_ref[0])
    u = _dot(hb, wu_ref[0])
    act = (a * _sigmoid(a)) * u
    if moe:
        lane = lax.broadcasted_iota(jnp.int32, (tm, LANES), 1)
        ge = jnp.sum(jnp.where(lane == e, gate_scr[...], 0.0), axis=-1, keepdims=True)
        act = act * ge
    acc_scr[...] += _dot(act.astype(bf16), wd_ref[0])

    @pl.when((e == ne - 1) & (j == nj - 1))
    def _():
        y = acc_scr[...]
        ms = jnp.mean(y * y, axis=-1, keepdims=True)
        y = y * lax.rsqrt(ms + EPS) * nw2_ref[...]
        o_ref[0] = x_ref[0] + mod_ref[0][:, 5 * d:6 * d] * y


def _ffn(x, mod, mod_row0, per_batch, nw1, wg, wu, wd, nw2, tm, tf, router=None, name="ffn"):
    bsz, l, d = x.shape
    ne, _, ff = wg.shape
    nt = l // tm
    nf = ff // tf
    moe = router is not None
    mod_map = (lambda b, i, e, j: (mod_row0 + b, 0, 0)) if per_batch else (lambda b, i, e, j: (mod_row0, 0, 0))
    in_specs = [
        pl.BlockSpec((1, tm, d), lambda b, i, e, j: (b, i, 0)),
        pl.BlockSpec((1, 1, mod.shape[-1]), mod_map),
        pl.BlockSpec((1, d), lambda b, i, e, j: (0, 0)),
    ]
    args = [x, mod, nw1]
    scratch = [pltpu.VMEM((tm, d), bf16), pltpu.VMEM((tm, d), f32)]
    if moe:
        in_specs += [
            pl.BlockSpec((d, LANES), lambda b, i, e, j: (0, 0)),
            pl.BlockSpec((1, LANES), lambda b, i, e, j: (0, 0)),
        ]
        args += list(router)
        scratch += [pltpu.VMEM((tm, LANES), f32)]
    in_specs += [
        pl.BlockSpec((1, d, tf), lambda b, i, e, j: (e, 0, j)),
        pl.BlockSpec((1, d, tf), lambda b, i, e, j: (e, 0, j)),
        pl.BlockSpec((1, tf, d), lambda b, i, e, j: (e, j, 0)),
        pl.BlockSpec((1, d), lambda b, i, e, j: (0, 0)),
    ]
    args += [wg, wu, wd, nw2]
    return pl.pallas_call(
        functools.partial(_ffn_kernel, moe=moe),
        grid=(bsz, nt, ne, nf),
        in_specs=in_specs,
        out_specs=pl.BlockSpec((1, tm, d), lambda b, i, e, j: (b, i, 0)),
        out_shape=jax.ShapeDtypeStruct((bsz, l, d), f32),
        scratch_shapes=scratch,
        compiler_params=_cparams(("parallel", "parallel", "arbitrary", "arbitrary")),
        name=name,
    )(*args)


MOE_TM = 512
MOE_TS = 512
META_I1, META_I2, META_R1, META_R2, META_G1, META_G2 = range(6)


def _router_kernel(x_ref, mod_ref, nw_ref, rw_ref, rb_ref, tri_ref, h_ref, meta_ref, cnt_ref, base_scr):
    d = x_ref.shape[-1]
    tm = x_ref.shape[1]

    @pl.when((pl.program_id(0) == 0) & (pl.program_id(1) == 0))
    def _():
        base_scr[...] = jnp.zeros_like(base_scr)

    x = x_ref[0]
    ms = jnp.mean(x * x, axis=-1, keepdims=True)
    h = x * lax.rsqrt(ms + EPS) * nw_ref[...]
    mod = mod_ref[0]
    h = h * (1.0 + mod[:, 4 * d:5 * d]) + mod[:, 3 * d:4 * d]
    h_ref[0] = h
    hh, hl = _split_hi_lo(h)
    wh, wl = _split_hi_lo(rw_ref[...])
    logits = _dot(hh, wh) + _dot(hl, wh) + _dot(hh, wl) + rb_ref[...]
    lane = lax.broadcasted_iota(jnp.int32, (tm, LANES), 1)
    lg = jnp.where(lane < N_EXPERTS, logits, MASK_NEG)
    m1 = jnp.max(lg, axis=-1, keepdims=True)
    i1 = jnp.min(jnp.where(lg == m1, lane, LANES), axis=-1, keepdims=True)
    lg2 = jnp.where(lane == i1, MASK_NEG, lg)
    m2 = jnp.max(lg2, axis=-1, keepdims=True)
    i2 = jnp.min(jnp.where(lg2 == m2, lane, LANES), axis=-1, keepdims=True)
    e2 = jnp.exp(m2 - m1)
    g1 = 1.0 / (1.0 + e2)
    g2 = e2 / (1.0 + e2)
    oh = jnp.where((lane == i1) | (lane == i2), 1.0, 0.0)
    pos = base_scr[...] + _dot(tri_ref[...], oh.astype(bf16))
    r1 = jnp.sum(jnp.where(lane == i1, pos, 0.0), axis=-1, keepdims=True)
    r2 = jnp.sum(jnp.where(lane == i2, pos, 0.0), axis=-1, keepdims=True)
    meta = jnp.zeros((tm, LANES), f32)
    for k, val in ((META_I1, i1.astype(f32)), (META_I2, i2.astype(f32)), (META_R1, r1), (META_R2, r2),
                   (META_G1, g1), (META_G2, g2)):
        meta = jnp.where(lane == k, val, meta)
    meta_ref[0] = meta
    new_base = pos[tm - 1:tm, :] + oh[tm - 1:tm, :]
    base_scr[...] = new_base
    cnt_ref[...] = jnp.broadcast_to(new_base, cnt_ref.shape)


def _moe_route(x, mod, mod_row0, nw1, rw, rb, tm):
    bsz, l, d = x.shape
    nt = l // tm
    r = jnp.arange(tm)
    tri = (r[:, None] > r[None, :]).astype(bf16)
    return pl.pallas_call(
        _router_kernel,
        grid=(bsz, nt),
        in_specs=[
            pl.BlockSpec((1, tm, d), lambda b, i: (b, i, 0)),
            pl.BlockSpec((1, 1, mod.shape[-1]), lambda b, i: (mod_row0 + b, 0, 0)),
            pl.BlockSpec((1, d), lambda b, i: (0, 0)),
            pl.BlockSpec((d, LANES), lambda b, i: (0, 0)),
            pl.BlockSpec((1, LANES), lambda b, i: (0, 0)),
            pl.BlockSpec((tm, tm), lambda b, i: (0, 0)),
        ],
        out_specs=[
            pl.BlockSpec((1, tm, d), lambda b, i: (b, i, 0)),
            pl.BlockSpec((1, tm, LANES), lambda b, i: (b, i, 0)),
            pl.BlockSpec((8, LANES), lambda b, i: (0, 0)),
        ],
        out_shape=[
            jax.ShapeDtypeStruct((bsz, l, d), f32),
            jax.ShapeDtypeStruct((bsz, l, LANES), f32),
            jax.ShapeDtypeStruct((8, LANES), f32),
        ],
        scratch_shapes=[pltpu.VMEM((1, LANES), f32)],
        compiler_params=_cparams(("arbitrary", "arbitrary")),
        name="moe_route",
    )(x, mod, nw1, rw, rb, tri)


def _row_copy(src_ref, src_row, dst_ref, dst_row, sem):
    return pltpu.make_async_copy(src_ref.at[pl.ds(src_row, 1)], dst_ref.at[pl.ds(dst_row, 1)], sem)


MOE_NZERO = 2 * N_EXPERTS


def _scatter_kernel(dest_ref, h_ref, xs_ref, zero_scr, sem, zsem):
    i = pl.program_id(0)
    ts = h_ref.shape[0]
    tm = zero_scr.shape[0]
    base = i * (2 * ts)

    @pl.when(i == 0)
    def _():
        zero_scr[...] = jnp.zeros(zero_scr.shape, f32)
        zbase = pl.num_programs(0) * (2 * ts)

        def tile_copy(k):
            start = pl.multiple_of(dest_ref[zbase + k], tm)
            return pltpu.make_async_copy(zero_scr, xs_ref.at[pl.ds(start, tm)], zsem)

        def when_named(k, action):
            @pl.when(dest_ref[zbase + MOE_NZERO + k] == 1)
            def _():
                action(tile_copy(k))

        for k in range(MOE_NZERO):
            when_named(k, lambda cp: cp.start())
        for k in range(MOE_NZERO):
            when_named(k, lambda cp: cp.wait())

    def issue(r, carry):
        _row_copy(h_ref, r, xs_ref, dest_ref[base + r], sem).start(priority=0)
        _row_copy(h_ref, r, xs_ref, dest_ref[base + ts + r], sem).start(priority=1)
        return carry

    lax.fori_loop(0, ts, issue, 0, unroll=8)

    def drain(r, carry):
        _row_copy(h_ref, 0, xs_ref, 0, sem).wait()
        _row_copy(h_ref, 0, xs_ref, 0, sem).wait()
        return carry

    lax.fori_loop(0, ts, drain, 0, unroll=8)


def _moe_scatter(h2, dest, n_rows, ts, tm):
    t, d = h2.shape
    return pl.pallas_call(
        _scatter_kernel,
        grid_spec=pltpu.PrefetchScalarGridSpec(
            num_scalar_prefetch=1,
            grid=(t // ts,),
            in_specs=[pl.BlockSpec((ts, d), lambda i, dest: (i, 0))],
            out_specs=pl.BlockSpec(memory_space=pl.ANY),
            scratch_shapes=[pltpu.VMEM((tm, d), f32), pltpu.SemaphoreType.DMA, pltpu.SemaphoreType.DMA],
        ),
        out_shape=jax.ShapeDtypeStruct((n_rows, d), f32),
        compiler_params=_cparams(("arbitrary",)),
        name="moe_scatter",
    )(dest, h2)


def _expert_kernel(te_ref, tv_ref, x_ref, wg_ref, wu_ref, wd_ref, o_ref, acc_scr):
    del te_ref
    j = pl.program_id(0)
    f = pl.program_id(1)
    nf = pl.num_programs(1)
    del nf
    valid = tv_ref[j] == 1

    @pl.when((j == 0) & (f == 0))
    def _():
        acc_scr[...] = jnp.zeros_like(acc_scr)

    @pl.when(valid)
    def _():
        xb = x_ref[...].astype(bf16)
        a = _dot(xb, wg_ref[0])
        u = _dot(xb, wu_ref[0])
        part = _dot(((a * _sigmoid(a)) * u).astype(bf16), wd_ref[0])
        acc = jnp.where(f > 0, acc_scr[...], 0.0) + part
        acc_scr[...] = acc
        o_ref[...] = acc

    @pl.when(jnp.logical_not(valid))
    def _():
        o_ref[...] = jnp.zeros_like(o_ref)


def _moe_experts(xs, te, tv, wg, wu, wd, tm, tf):
    n_rows, d = xs.shape
    ff = wg.shape[-1]
    nf = ff // tf
    n_tiles = te.shape[0]

    def fsel(j, f, tv):
        return jnp.where(tv[j] == 1, f, nf - 1)

    return pl.pallas_call(
        _expert_kernel,
        grid_spec=pltpu.PrefetchScalarGridSpec(
            num_scalar_prefetch=2,
            grid=(n_tiles, nf),
            in_specs=[
                pl.BlockSpec((tm, d), lambda j, f, te, tv: (j, 0)),
                pl.BlockSpec((1, d, tf), lambda j, f, te, tv: (te[j], 0, fsel(j, f, tv))),
                pl.BlockSpec((1, d, tf), lambda j, f, te, tv: (te[j], 0, fsel(j, f, tv))),
                pl.BlockSpec((1, tf, d), lambda j, f, te, tv: (te[j], fsel(j, f, tv), 0)),
            ],
            out_specs=pl.BlockSpec((tm, d), lambda j, f, te, tv: (j, 0)),
            scratch_shapes=[pltpu.VMEM((tm, d), f32)],
        ),
        out_shape=jax.ShapeDtypeStruct((n_rows, d), f32),
        compiler_params=_cparams(("arbitrary", "arbitrary")),
        name="moe_experts",
    )(te, tv, xs, wg, wu, wd)


def _combine_kernel(dest_ref, x_ref, mod_ref, meta_ref, nw_ref, ys_ref, o_ref, y1_scr, y2_scr, sem):
    i = pl.program_id(0)
    ts = x_ref.shape[0]
    d = x_ref.shape[1]
    base = i * (2 * ts)

    def issue(r, carry):
        _row_copy(ys_ref, dest_ref[base + r], y1_scr, r, sem).start(priority=0)
        _row_copy(ys_ref, dest_ref[base + ts + r], y2_scr, r, sem).start(priority=1)
        return carry

    lax.fori_loop(0, ts, issue, 0, unroll=8)

    def drain(r, carry):
        _row_copy(ys_ref, 0, y1_scr, 0, sem).wait()
        _row_copy(ys_ref, 0, y2_scr, 0, sem).wait()
        return carry

    lax.fori_loop(0, ts, drain, 0, unroll=8)

    meta = meta_ref[...]
    y = meta[:, META_G1:META_G1 + 1] * y1_scr[...] + meta[:, META_G2:META_G2 + 1] * y2_scr[...]
    ms = jnp.mean(y * y, axis=-1, keepdims=True)
    y = y * lax.rsqrt(ms + EPS) * nw_ref[...]
    o_ref[...] = x_ref[...] + mod_ref[0][:, 5 * d:6 * d] * y


def _moe_combine(x2, mod, mod_row0, tokens_per_batch, meta2, nw2, ys, dest, ts):
    t, d = x2.shape
    per_b = tokens_per_batch // ts
    return pl.pallas_call(
        _combine_kernel,
        grid_spec=pltpu.PrefetchScalarGridSpec(
            num_scalar_prefetch=1,
            grid=(t // ts,),
            in_specs=[
                pl.BlockSpec((ts, d), lambda i, dest: (i, 0)),
                pl.BlockSpec((1, 1, mod.shape[-1]), lambda i, dest: (mod_row0 + i // per_b, 0, 0)),
                pl.BlockSpec((ts, LANES), lambda i, dest: (i, 0)),
                pl.BlockSpec((1, d), lambda i, dest: (0, 0)),
                pl.BlockSpec(memory_space=pl.ANY),
            ],
            out_specs=pl.BlockSpec((ts, d), lambda i, dest: (i, 0)),
            scratch_shapes=[pltpu.VMEM((ts, d), f32), pltpu.VMEM((ts, d), f32), pltpu.SemaphoreType.DMA],
        ),
        out_shape=jax.ShapeDtypeStruct((t, d), f32),
        compiler_params=_cparams(("arbitrary",)),
        name="moe_combine",
    )(dest, x2, mod, meta2, nw2, ys)


def _moe_sparse(x, mod, mod_row0, nw1, rw, rb, wg, wu, wd, nw2):
    bsz, l, d = x.shape
    t = bsz * l
    ne = wg.shape[0]
    tm_r = _pick_tile(l, (1024, 512, 256, 128))
    tm = min(MOE_TM, t)
    ts = _pick_tile(l, (MOE_TS, 256, 128))
    tf = _pick_tile(wg.shape[-1], (1792, 896, 512, 256, 128))

    h, meta, cnt = _moe_route(x, mod, mod_row0, nw1, rw, rb, tm_r)
    meta2 = meta.reshape(t, LANES)
    counts = cnt[0, :ne].astype(jnp.int32)
    tiles_e = (counts + tm - 1) // tm
    ends = jnp.cumsum(tiles_e)
    offset = (ends - tiles_e) * tm
    n_tiles = (2 * t) // tm + ne
    i1 = meta2[:, META_I1].astype(jnp.int32)
    i2 = meta2[:, META_I2].astype(jnp.int32)
    dest1 = jnp.take(offset, i1) + meta2[:, META_R1].astype(jnp.int32)
    dest2 = jnp.take(offset, i2) + meta2[:, META_R2].astype(jnp.int32)
    dest = jnp.stack([dest1.reshape(t // ts, ts), dest2.reshape(t // ts, ts)], axis=1).reshape(-1)
    jt = jnp.arange(n_tiles, dtype=jnp.int32)
    total = ends[ne - 1]
    tv = (jt < total).astype(jnp.int32)
    te_all = jnp.minimum(jnp.sum((jt[:, None] >= ends[None, :]).astype(jnp.int32), axis=1), ne - 1)
    te_last = jnp.take(te_all, jnp.maximum(total - 1, 0))
    te = jnp.where(tv == 1, te_all, te_last)
    n_rows = n_tiles * tm
    past = total + jnp.arange(ne, dtype=jnp.int32)
    zero_tiles = jnp.concatenate([jnp.maximum(ends - 1, 0), jnp.minimum(past, n_tiles - 1)])
    zero_flags = jnp.concatenate([tiles_e > 0, past < n_tiles]).astype(jnp.int32)
    dest = jnp.concatenate([dest, zero_tiles.astype(jnp.int32) * tm, zero_flags])

    xs = _moe_scatter(h.reshape(t, d), dest, n_rows, ts, tm)
    ys = _moe_experts(xs, te, tv, wg, wu, wd, tm, tf)
    out = _moe_combine(x.reshape(t, d), mod, mod_row0, l, meta2, nw2, ys, dest, ts)
    return out.reshape(bsz, l, d)


def _pick_tile(n, prefs):
    for t in prefs:
        if n % t == 0:
            return t
    return n


def kernel(x, c, ctx, c_ctx, w_mod, b_mod, norm_pre_mix, norm_post_mix, norm_pre_ffn, norm_post_ffn, w_in, w_out, ml_gate_b, ml_norm, hg_lb_logits, hg_norm, da_lambda, da_norm, ffn_w_gate, ffn_w_up, ffn_w_down, router_w, router_b, moe_w_gate, moe_w_up, moe_w_down):
    depth = w_mod.shape[0]
    bsz, l, d = x.shape
    lctx = ctx.shape[1]

    mod = _mod_vectors(c, c_ctx, w_mod, b_mod)
    rows = mod.shape[1]
    mod = mod.reshape(depth * rows, 1, 6 * d)

    lb_soft = jax.nn.softmax(hg_lb_logits.astype(f32), axis=0)
    lower_bounds = jnp.cumsum(lb_soft, axis=0) - lb_soft[0]

    tables = _rope_tables(l)
    ml_lc = min(ML_CHUNK, lctx)
    tri_ml_x = _tri_pair(min(ML_CHUNK, l), min(ML_SUB, l))
    tri_ml_c = _tri_pair(ml_lc, min(ML_SUB, ml_lc))
    tri_hg_x = _tri_pair(min(HG_GROUP, l), HG_CHUNK)
    tri_hg_c = _tri_pair(min(HG_GROUP, lctx), HG_CHUNK)
    hg_bias = _hgrn2_bias()
    ml_sel = _mlstm_sel()

    tm_x = _pick_tile(l, (1024, 512, 256, 128))
    tm_c = _pick_tile(lctx, (256, 128))
    tq = _pick_tile(l, (512, 256, 128))

    for layer in range(depth):
        need_ctx = layer < depth - 1
        lam_init = 0.8 - 0.6 * math.exp(-0.3 * layer)
        row0 = layer * rows
        w_in_l = _prep_w_in(w_in[layer])
        w_out_l = w_out[layer].astype(bf16)
        nw_pre = norm_pre_mix[layer].reshape(1, d)

        pc = _inproj(ctx, mod, row0 + bsz, False, nw_pre, w_in_l, None, tm_c)
        px = _inproj(x, mod, row0, True, nw_pre, w_in_l, tables, tm_x)
        ml_c, mlg_c, hg_c, hgf_c, da_c = pc
        ml_x, mlg_x, hg_x, hgf_x, da_x = px

        gb = ml_gate_b[layer].astype(f32)
        z = jnp.zeros((LANES - 8,), f32)
        gate_b2 = jnp.stack([jnp.concatenate([gb[0:8], z]), jnp.concatenate([gb[8:16], z])]).reshape(2, 1, LANES)
        s0 = (jnp.zeros((2, bsz, ML_H, LANES, 2 * LANES), f32), jnp.zeros((2, bsz, ML_H, 8, LANES), f32))
        mlo_c, s_ml = _mlstm(ml_c, mlg_c, gate_b2, tri_ml_c, ml_sel, s0)
        mlo_x, _ = _mlstm(ml_x, mlg_x, gate_b2, tri_ml_x, ml_sel, s_ml)

        lb = lower_bounds[layer]
        lbc = jnp.zeros((8, HG_W), f32)
        lbc = lbc.at[0].set(jnp.log(jnp.maximum(lb, LB_FLOOR))).at[1].set(jnp.log1p(-lb)).at[2].set(1.0 - lb)
        h0 = jnp.zeros((2, bsz, HG_W, HG_W // 2), f32)
        hgo_c, s_hg = _hgrn2(hg_c, hgf_c, lbc, tri_hg_c, hg_bias, h0)
        hgo_x, _ = _hgrn2(hg_x, hgf_x, lbc, tri_hg_x, hg_bias, s_hg)

        lam_p = da_lambda[layer].astype(f32)
        da_nw = da_norm[layer].reshape(1, DA_V)
        dao_x = _attn(da_x, da_c, lam_p, da_nw, lam_init, tq)

        mlw = ml_norm[layer].reshape(1, 256)
        hgw = hg_norm[layer].reshape(1, 256)
        nw_post = norm_post_mix[layer].reshape(1, d)
        x = _outproj(x, mod, row0, True, mlo_x, ml_x, hgo_x, hg_x, dao_x, mlw, hgw, w_out_l, nw_post,
                     _pick_tile(l, (1024, 512, 256, 128)), "outproj_x")
        if need_ctx:
            dao_c = _attn(da_c, da_c, lam_p, da_nw, lam_init, lctx)
            ctx = _outproj(ctx, mod, row0 + bsz, False, mlo_c, ml_c, hgo_c, hg_c, dao_c, mlw, hgw, w_out_l, nw_post, tm_c, "outproj_c")

        i = layer // 2
        nw1 = norm_pre_ffn[layer].reshape(1, d)
        nw2 = norm_post_ffn[layer].reshape(1, d)
        if layer % 2 == 0:
            wg = ffn_w_gate[i].astype(bf16)[None]
            wu = ffn_w_up[i].astype(bf16)[None]
            wd = ffn_w_down[i].astype(bf16)[None]
            router = None
        else:
            wg = moe_w_gate[i].astype(bf16)
            wu = moe_w_up[i].astype(bf16)
            wd = moe_w_down[i].astype(bf16)
            rw = jnp.zeros((d, LANES), f32).at[:, :N_EXPERTS].set(router_w[i])
            rb = jnp.zeros((1, LANES), f32).at[0, :N_EXPERTS].set(router_b[i])
            router = (rw, rb)
        ff = wg.shape[-1]
        tf = _pick_tile(ff, (896, 1408, 512, 256))
        tm_f = _pick_tile(l, (1024, 512, 256, 128) if tf <= 896 else (512, 256, 128))
        if router is None:
            x = _ffn(x, mod, row0, True, nw1, wg, wu, wd, nw2, tm_f, tf, None, name="ffn_x%d" % layer)
        else:
            x = _moe_sparse(x, mod, row0, nw1, rw, rb, wg, wu, wd, nw2)
        if need_ctx:
            ctx = _ffn(ctx, mod, row0 + bsz, False, nw1, wg, wu, wd, nw2, tm_c, tf, router, name="ffn_c%d" % layer)
    return x
```
